```python
import math
import jax
import jax.numpy as jnp
from jax import lax
import numpy as np

D_MODEL = 2048
BATCH = 8
SEQ = 2048
DEPTH = 4

D_MIX = D_MODEL
HEAD_DIM = 64
SSD_WIDTH = D_MIX // 2
RWKV_WIDTH = D_MIX - SSD_WIDTH
SSD_HEADS = SSD_WIDTH // HEAD_DIM
SSD_GROUPS = 2
SSD_HEADS_PER_GROUP = SSD_HEADS // SSD_GROUPS
SSD_STATE = 128
SSD_CONV_WIDTH = 4
SSD_CHUNK = 128
SSD_CONV_DIM = SSD_WIDTH + 2 * SSD_GROUPS * SSD_STATE
RWKV_HEADS = RWKV_WIDTH // HEAD_DIM
W_LORA = 64
A_LORA = 64
V_LORA = 32
G_LORA = 160
RWKV_SHIFT_DIM = 3 * RWKV_WIDTH + W_LORA + A_LORA + G_LORA
IN_DIM = SSD_WIDTH + SSD_CONV_DIM + SSD_HEADS + RWKV_SHIFT_DIM
D_FF = 11 * D_MODEL // 4
N_EXPERTS = 8
TOP_K = 2
D_FF_EXPERT = D_FF // 2
N_DENSE = (DEPTH + 1) // 2
N_MOE = DEPTH // 2
DEEPNORM_ALPHA = (2 * DEPTH) ** 0.25
DEEPNORM_BETA = (8 * DEPTH) ** -0.25
LN_EPS = 1e-5
RMS_EPS = 1e-5
RWKV_GN_EPS = 64e-5
L2_EPS = 1e-12

kernel_name = "hymba_ssd_rwkv7_deepnorm_moe"


def _split(u, sizes):
    idx = []
    acc = 0
    for s in sizes:
        acc += s
        idx.append(acc)
    return jnp.split(u, idx, axis=-1)


def layer_norm(x, g, b):
    xf = x.astype(jnp.float32)
    mu = jnp.mean(xf, axis=-1, keepdims=True)
    var = jnp.mean(jnp.square(xf - mu), axis=-1, keepdims=True)
    y = (xf - mu) * lax.rsqrt(var + LN_EPS)
    return (y * g + b).astype(x.dtype)


def token_shift(u):
    return jnp.pad(u[:, :-1], ((0, 0), (1, 0), (0, 0)))


def causal_depthwise_conv(u, w, b):
    k = w.shape[0]
    out = lax.conv_general_dilated(
        u, w[:, None, :].astype(u.dtype), window_strides=(1,), padding=[(k - 1, 0)],
        dimension_numbers=("NWC", "WIO", "NWC"), feature_group_count=u.shape[-1])
    return out + b


def ssd_chunked_scan(xdt, dA, bm, cm):
    bsz, t, g, e, p = xdt.shape
    n = bm.shape[-1]
    L = SSD_CHUNK
    nc = t // L
    xc = xdt.reshape(bsz, nc, L, g, e, p)
    bc = bm.reshape(bsz, nc, L, g, n)
    cc = cm.reshape(bsz, nc, L, g, n)
    a = jnp.transpose(dA.reshape(bsz, nc, L, g, e), (0, 1, 3, 4, 2))
    a_cum = jnp.cumsum(a, axis=-1)
    causal = jnp.tril(jnp.ones((L, L), dtype=bool))
    seg = a_cum[..., :, None] - a_cum[..., None, :]
    decay_ls = jnp.exp(jnp.where(causal, seg, -jnp.inf))
    cb = jnp.einsum("bclgn,bcsgn->bcgls", cc, bc)
    y_diag = jnp.einsum("bcgels,bcsgep->bclgep", cb[:, :, :, None] * decay_ls, xc)
    decay_to_end = jnp.exp(a_cum[..., -1:] - a_cum)
    chunk_states = jnp.einsum("bcsgn,bcges,bcsgep->bcgepn", bc, decay_to_end, xc)
    chunk_decay = jnp.exp(a_cum[..., -1])

    def step(h, inp):
        dec, st = inp
        return h * dec[..., None, None] + st, h

    h0 = jnp.zeros((bsz, g, e, p, n), xdt.dtype)
    _, h_prev = lax.scan(step, h0, (jnp.moveaxis(chunk_decay, 1, 0), jnp.moveaxis(chunk_states, 1, 0)))
    h_prev = jnp.moveaxis(h_prev, 0, 1)
    y_off = jnp.einsum("bclgn,bcgepn,bcgel->bclgep", cc, h_prev, jnp.exp(a_cum))
    return (y_diag + y_off).reshape(bsz, t, g, e, p)


def ssd_mixer(z, xbc, dt_raw, conv_w, conv_b, dt_bias, a_log, d_skip, norm_w):
    f32 = jnp.float32
    bsz, t, _ = z.shape
    xbc = jax.nn.silu(causal_depthwise_conv(xbc, conv_w, conv_b))
    xs, bm, cm = _split(xbc, (SSD_WIDTH, SSD_GROUPS * SSD_STATE))
    xs = xs.astype(f32).reshape(bsz, t, SSD_GROUPS, SSD_HEADS_PER_GROUP, HEAD_DIM)
    bm = bm.astype(f32).reshape(bsz, t, SSD_GROUPS, SSD_STATE)
    cm = cm.astype(f32).reshape(bsz, t, SSD_GROUPS, SSD_STATE)
    dt = jax.nn.softplus(dt_raw.astype(f32) + dt_bias.astype(f32)).reshape(bsz, t, SSD_GROUPS, SSD_HEADS_PER_GROUP)
    a = -jnp.exp(a_log.astype(f32)).reshape(SSD_GROUPS, SSD_HEADS_PER_GROUP)
    y = ssd_chunked_scan(xs * dt[..., None], dt * a, bm, cm)
    y = y + xs * d_skip.astype(f32).reshape(SSD_GROUPS, SSD_HEADS_PER_GROUP, 1)
    u = y.reshape(bsz, t, SSD_WIDTH) * jax.nn.silu(z.astype(f32))
    u = u.reshape(bsz, t, SSD_GROUPS, SSD_WIDTH // SSD_GROUPS)
    u = u * lax.rsqrt(jnp.mean(jnp.square(u), axis=-1, keepdims=True) + RMS_EPS)
    return (u.reshape(bsz, t, SSD_WIDTH) * norm_w).astype(z.dtype)


def wkv7_scan(r, w, k, v, a, b):
    bsz, t, h, n = r.shape

    def step(s, inp):
        r_t, w_t, k_t, v_t, a_t, b_t = inp
        sa = jnp.einsum("bhij,bhj->bhi", s, a_t)
        s = s * w_t[:, :, None, :] + sa[..., None] * b_t[:, :, None, :] + v_t[..., None] * k_t[:, :, None, :]
        return s, jnp.einsum("bhij,bhj->bhi", s, r_t)

    s0 = jnp.zeros((bsz, h, n, n), jnp.float32)
    seq = tuple(jnp.moveaxis(u, 1, 0) for u in (r, w, k, v, a, b))
    _, y = lax.scan(step, s0, seq)
    return jnp.moveaxis(y, 0, 1)


def rwkv7_mixer(p, mix, w0, w_up, a0, a_up, g_up, k_k, k_a, r_k, ln_w, ln_b, v_res):
    f32 = jnp.float32
    bsz, t, _ = p.shape
    p = p + (token_shift(p) - p) * mix
    r, k, v, w_lo, a_lo, g_lo = _split(p, (RWKV_WIDTH, RWKV_WIDTH, RWKV_WIDTH, W_LORA, A_LORA))
    w = -jax.nn.softplus(-(w0 + jnp.tanh(w_lo) @ w_up).astype(f32)) - 0.5
    decay = jnp.exp(-jnp.exp(w))
    if v_res is None:
        v_first = v
    else:
        v_lo, v_mix, v0, v_up, v_first = v_res
        v_lo = v_lo + (token_shift(v_lo) - v_lo) * v_mix
        v = v + (v_first - v) * jax.nn.sigmoid(v0 + v_lo @ v_up)
    a = jax.nn.sigmoid(a0 + a_lo @ a_up)
    g = jax.nn.sigmoid(g_lo) @ g_up

    def heads(u):
        return u.astype(f32).reshape(bsz, t, RWKV_HEADS, HEAD_DIM)

    kk = heads(k * k_k)
    kk = kk / jnp.maximum(jnp.linalg.norm(kk, axis=-1, keepdims=True), L2_EPS)
    k = k * (1 + (a - 1) * k_a)
    rh, kh, vh, ah = heads(r), heads(k), heads(v), heads(a)
    y = wkv7_scan(rh, heads(decay), kh, vh, -kk, kk * ah)
    mu = jnp.mean(y, axis=-1, keepdims=True)
    var = jnp.mean(jnp.square(y - mu), axis=-1, keepdims=True)
    y = ((y - mu) * lax.rsqrt(var + RWKV_GN_EPS)).reshape(bsz, t, RWKV_WIDTH) * ln_w + ln_b
    bonus = jnp.sum(rh * kh * r_k, axis=-1, keepdims=True) * vh
    y = y + bonus.reshape(bsz, t, RWKV_WIDTH)
    return (y * g).astype(p.dtype), v_first


def swiglu(h, w1, w3, w2):
    return (jax.nn.silu(h @ w1) * (h @ w3)) @ w2


def moe_swiglu(h, router, w1, w3, w2):
    bsz, t, d = h.shape
    tok = h.reshape(bsz * t, d)
    logits = (tok @ router).astype(jnp.float32)
    top_logits, top_idx = lax.top_k(logits, TOP_K)
    gates = jax.nn.softmax(top_logits, axis=-1)
    combine = jnp.sum(jax.nn.one_hot(top_idx, N_EXPERTS, dtype=jnp.float32) * gates[..., None], axis=1).astype(h.dtype)
    out = jnp.zeros_like(tok)
    for e in range(N_EXPERTS):
        out = out + combine[:, e:e + 1] * swiglu(tok, w1[e], w3[e], w2[e])
    return out.reshape(bsz, t, d)


def setup_inputs(seed: int = 0) -> dict:
    key = jax.random.key(seed)
    ks = iter(jax.random.split(key, 64))
    f32 = jnp.float32

    def nrm(shape, scale):
        return scale * jax.random.normal(next(ks), shape, f32)

    def gain(shape):
        return 1.0 + nrm(shape, 0.02)

    x = nrm((BATCH, SEQ, D_MODEL), 1.0)
    w_in = nrm((DEPTH, D_MODEL, IN_DIM), D_MODEL ** -0.5)
    w_in_vres = nrm((DEPTH - 1, D_MODEL, V_LORA), D_MODEL ** -0.5)
    ssd_conv_w = nrm((DEPTH, SSD_CONV_WIDTH, SSD_CONV_DIM), SSD_CONV_WIDTH ** -0.5)
    ssd_conv_b = nrm((DEPTH, SSD_CONV_DIM), 0.02)
    dt0 = jnp.exp(jax.random.uniform(next(ks), (DEPTH, SSD_HEADS), f32, minval=math.log(1e-3), maxval=math.log(1e-1)))
    ssd_dt_bias = dt0 + jnp.log(-jnp.expm1(-dt0))
    ssd_a_log = jnp.log(jax.random.uniform(next(ks), (DEPTH, SSD_HEADS), f32, minval=1.0, maxval=16.0))
    ssd_d = gain((DEPTH, SSD_HEADS))
    ssd_norm_w = gain((DEPTH, SSD_WIDTH))
    rw_mix = jax.random.uniform(next(ks), (DEPTH, RWKV_SHIFT_DIM), f32)
    rw_vres_mix = jax.random.uniform(next(ks), (DEPTH - 1, V_LORA), f32)
    ratio = jnp.linspace(0.0, 1.0, RWKV_WIDTH, dtype=f32)
    rw_w0 = (-6.0 + 5.0 * ratio ** 0.85 + 0.5)[None, :] + nrm((DEPTH, RWKV_WIDTH), 0.05)
    rw_w_up = nrm((DEPTH, W_LORA, RWKV_WIDTH), 0.5 * W_LORA ** -0.5)
    rw_a0 = nrm((DEPTH, RWKV_WIDTH), 0.1)
    rw_a_up = nrm((DEPTH, A_LORA, RWKV_WIDTH), A_LORA ** -0.5)
    rw_v0 = nrm((DEPTH - 1, RWKV_WIDTH), 0.1)
    rw_v_up = nrm((DEPTH - 1, V_LORA, RWKV_WIDTH), V_LORA ** -0.5)
    rw_g_up = nrm((DEPTH, G_LORA, RWKV_WIDTH), G_LORA ** -0.5)
    rw_k_k = 0.85 + nrm((DEPTH, RWKV_WIDTH), 0.02)
    rw_k_a = gain((DEPTH, RWKV_WIDTH))
    rw_r_k = nrm((DEPTH, RWKV_HEADS, HEAD_DIM), 0.1)
    rw_ln_w = gain((DEPTH, RWKV_WIDTH))
    rw_ln_b = nrm((DEPTH, RWKV_WIDTH), 0.02)
    w_out = nrm((DEPTH, D_MIX, D_MODEL), DEEPNORM_BETA * D_MIX ** -0.5)
    ln1_g = gain((DEPTH, D_MODEL))
    ln1_b = nrm((DEPTH, D_MODEL), 0.02)
    ln2_g = gain((DEPTH, D_MODEL))
    ln2_b = nrm((DEPTH, D_MODEL), 0.02)
    ffn_w1 = nrm((N_DENSE, D_MODEL, D_FF), D_MODEL ** -0.5)
    ffn_w3 = nrm((N_DENSE, D_MODEL, D_FF), D_MODEL ** -0.5)
    ffn_w2 = nrm((N_DENSE, D_FF, D_MODEL), DEEPNORM_BETA * D_FF ** -0.5)
    moe_router = nrm((N_MOE, D_MODEL, N_EXPERTS), D_MODEL ** -0.5)
    moe_w1 = nrm((N_MOE, N_EXPERTS, D_MODEL, D_FF_EXPERT), D_MODEL ** -0.5)
    moe_w3 = nrm((N_MOE, N_EXPERTS, D_MODEL, D_FF_EXPERT), D_MODEL ** -0.5)
    moe_w2 = nrm((N_MOE, N_EXPERTS, D_FF_EXPERT, D_MODEL), DEEPNORM_BETA * D_FF_EXPERT ** -0.5)
    return {
        "x": x, "w_in": w_in, "w_in_vres": w_in_vres,
        "ssd_conv_w": ssd_conv_w, "ssd_conv_b": ssd_conv_b, "ssd_dt_bias": ssd_dt_bias,
        "ssd_a_log": ssd_a_log, "ssd_d": ssd_d, "ssd_norm_w": ssd_norm_w,
        "rw_mix": rw_mix, "rw_vres_mix": rw_vres_mix, "rw_w0": rw_w0, "rw_w_up": rw_w_up,
        "rw_a0": rw_a0, "rw_a_up": rw_a_up, "rw_v0": rw_v0, "rw_v_up": rw_v_up,
        "rw_g_up": rw_g_up, "rw_k_k": rw_k_k, "rw_k_a": rw_k_a, "rw_r_k": rw_r_k,
        "rw_ln_w": rw_ln_w, "rw_ln_b": rw_ln_b, "w_out": w_out,
        "ln1_g": ln1_g, "ln1_b": ln1_b, "ln2_g": ln2_g, "ln2_b": ln2_b,
        "ffn_w1": ffn_w1, "ffn_w3": ffn_w3, "ffn_w2": ffn_w2,
        "moe_router": moe_router, "moe_w1": moe_w1, "moe_w3": moe_w3, "moe_w2": moe_w2,
    }


def reference(x, w_in, w_in_vres, ssd_conv_w, ssd_conv_b, ssd_dt_bias, ssd_a_log, ssd_d, ssd_norm_w,
              rw_mix, rw_vres_mix, rw_w0, rw_w_up, rw_a0, rw_a_up, rw_v0, rw_v_up, rw_g_up,
              rw_k_k, rw_k_a, rw_r_k, rw_ln_w, rw_ln_b, w_out, ln1_g, ln1_b, ln2_g, ln2_b,
              ffn_w1, ffn_w3, ffn_w2, moe_router, moe_w1, moe_w3, moe_w2):
    v_first = None
    for l in range(DEPTH):
        if l == 0:
            w_comb = w_in[0]
        else:
            w_comb = jnp.concatenate([w_in[l], w_in_vres[l - 1]], axis=-1)
        proj = jnp.einsum("btd,dc->btc", x, w_comb)
        z, xbc, dt_raw, rw_p, v_lo = _split(proj, (SSD_WIDTH, SSD_CONV_DIM, SSD_HEADS, RWKV_SHIFT_DIM))
        y_ssd = ssd_mixer(z, xbc, dt_raw, ssd_conv_w[l], ssd_conv_b[l], ssd_dt_bias[l], ssd_a_log[l],
                          ssd_d[l], ssd_norm_w[l])
        if l == 0:
            v_res = None
        else:
            v_res = (v_lo, rw_vres_mix[l - 1], rw_v0[l - 1], rw_v_up[l - 1], v_first)
        y_rw, v_first = rwkv7_mixer(rw_p, rw_mix[l], rw_w0[l], rw_w_up[l], rw_a0[l], rw_a_up[l], rw_g_up[l],
                                    rw_k_k[l], rw_k_a[l], rw_r_k[l], rw_ln_w[l], rw_ln_b[l], v_res)
        mixed = jnp.einsum("btc,cd->btd", jnp.concatenate([y_ssd, y_rw], axis=-1), w_out[l])
        x = layer_norm(DEEPNORM_ALPHA * x + mixed, ln1_g[l], ln1_b[l])
        if l % 2 == 0:
            f = swiglu(x, ffn_w1[l // 2], ffn_w3[l // 2], ffn_w2[l // 2])
        else:
            f = moe_swiglu(x, moe_router[l // 2], moe_w1[l // 2], moe_w3[l // 2], moe_w2[l // 2])
        x = layer_norm(DEEPNORM_ALPHA * x + f, ln2_g[l], ln2_b[l])
    return x
```

```python
import functools

import jax
import jax.numpy as jnp
from jax import lax
from jax.experimental import pallas as pl
from jax.experimental.pallas import tpu as pltpu

F32 = jnp.float32
BF16 = jnp.bfloat16
HIGHEST = lax.Precision.HIGHEST

D_MODEL = 2048
HEAD_DIM = 64
SSD_WIDTH = 1024
RWKV_WIDTH = 1024
SSD_HEADS = SSD_WIDTH // HEAD_DIM
SSD_GROUPS = 2
SSD_STATE = 128
SSD_CONV_WIDTH = 4
RWKV_HEADS = RWKV_WIDTH // HEAD_DIM
W_LORA = 64
A_LORA = 64
V_LORA = 32
G_LORA = 160
N_EXPERTS = 8
DEPTH = 4
DEEPNORM_ALPHA = (2 * DEPTH) ** 0.25
LN_EPS = 1e-5
RMS_EPS = 1e-5
RWKV_GN_EPS = 64e-5
L2_EPS = 1e-12

LANES = 128
SUBLANES = 8
VMEM_LIMIT = 56 * 1024 * 1024

CHUNK = 128
HALO = SUBLANES

COLBLK = 1024
BLK_Z, BLK_XS, BLK_R, BLK_K, BLK_V = 0, 1, 2, 3, 4
SMALLBLK = 512
BLK_BC = 10
BLK_MISC = 11
PROJ_WIDTH = 6 * COLBLK
MISC_WA = 128
MISC_GV = 256


def _params(sem, vmem=VMEM_LIMIT):
    return pltpu.CompilerParams(dimension_semantics=sem, vmem_limit_bytes=vmem)


def _dot(a, b, precision=None):
    return jnp.dot(a, b, preferred_element_type=F32, precision=precision)


def _dot_nt(a, b):
    return lax.dot_general(a, b, (((1,), (1,)), ((), ())), preferred_element_type=F32)


def _bf(x):
    return x.astype(BF16)


def _softplus(x):
    return jnp.maximum(x, 0.0) + jnp.log1p(jnp.exp(-jnp.abs(x)))


def _layer_norm(y, g, b):
    mu = jnp.mean(y, axis=-1, keepdims=True)
    d = y - mu
    var = jnp.mean(d * d, axis=-1, keepdims=True)
    return d * lax.rsqrt(var + LN_EPS) * g + b


def _mm_body(x_ref, w_ref, o_ref):
    o_ref[...] = _dot(x_ref[...], w_ref[...]).astype(o_ref.dtype)


def _matmul(x, w, bm, bn, out_dtype):
    m, k = x.shape
    n = w.shape[1]
    return pl.pallas_call(
        _mm_body,
        grid=(m // bm, n // bn),
        in_specs=[pl.BlockSpec((bm, k), lambda i, j: (i, 0)),
                  pl.BlockSpec((k, bn), lambda i, j: (0, j))],
        out_specs=pl.BlockSpec((bm, bn), lambda i, j: (i, j)),
        out_shape=jax.ShapeDtypeStruct((m, n), out_dtype),
        compiler_params=_params(("parallel", "parallel")),
        name="in_proj",
    )(x, w)


def _ssd_body(z_ref, xs_ref, bc_ref, misc_ref, cwx_ref, cbx_ref, cwb_ref, cbb_ref, dtb_ref,
              alog_ref, dskip_ref, nw_ref, e_ref, o_ref, xs_buf, bc_buf, h_ref, y_ref):
    L = CHUNK
    c = pl.program_id(1)

    @pl.when(c == 0)
    def _init():
        xs_buf[0:HALO, :] = jnp.zeros((HALO, xs_buf.shape[1]), F32)
        bc_buf[0:HALO, :] = jnp.zeros((HALO, bc_buf.shape[1]), F32)
        h_ref[...] = jnp.zeros_like(h_ref)

    xs_buf[HALO:HALO + L, :] = xs_ref[...]
    bc_buf[HALO:HALO + L, :] = bc_ref[...]

    def conv_silu(buf, w_ref, b_ref):
        acc = b_ref[...]
        for i in range(SSD_CONV_WIDTH):
            acc = acc + buf[pl.ds(HALO - (SSD_CONV_WIDTH - 1) + i, L), :] * w_ref[i:i + 1, :]
        return acc * jax.nn.sigmoid(acc)

    xs = conv_silu(xs_buf, cwx_ref, cbx_ref)
    bcv = conv_silu(bc_buf, cwb_ref, cbb_ref)
    xs_buf[0:HALO, :] = xs_buf[L:L + HALO, :]
    bc_buf[0:HALO, :] = bc_buf[L:L + HALO, :]

    lane = lax.broadcasted_iota(jnp.int32, (1, LANES), 1)
    dt = _softplus(misc_ref[:, 0:LANES] + dtb_ref[...])
    a_neg = jnp.where(lane < SSD_HEADS, -jnp.exp(alog_ref[...]), 0.0)
    d_a = dt * a_neg
    row = lax.broadcasted_iota(jnp.int32, (L, L), 0)
    col = lax.broadcasted_iota(jnp.int32, (L, L), 1)
    causal = row >= col
    a_cum = _dot(causal.astype(F32), d_a, HIGHEST)
    a_cum_t = a_cum.T
    a_last = a_cum[L - 1:L, :]
    per_head = jnp.concatenate([dt, jnp.exp(a_last - a_cum), jnp.exp(a_cum)], axis=0)
    full = _dot(per_head, e_ref[...], HIGHEST)
    dt_f, dte_f, ea_f = full[0:L], full[L:2 * L], full[2 * L:3 * L]
    xdt = xs * dt_f
    xdt_b = _bf(xdt)
    xdte_b = _bf(xdt * dte_f)
    lane_l = lax.broadcasted_iota(jnp.int32, (L, LANES), 1)
    lower_half = lane_l < HEAD_DIM
    gw = SSD_WIDTH // SSD_GROUPS
    heads_per_group = SSD_HEADS // SSD_GROUPS
    for g in range(SSD_GROUPS):
        b_g = bcv[:, g * SSD_STATE:(g + 1) * SSD_STATE]
        c_g = _bf(bcv[:, (SSD_GROUPS + g) * SSD_STATE:(SSD_GROUPS + g + 1) * SSD_STATE])
        cb = _dot_nt(c_g, _bf(b_g))
        h_g = h_ref[g]
        y_off = _dot(c_g, _bf(h_g)) * ea_f[:, g * gw:(g + 1) * gw]
        new_state = _dot(_bf(b_g.T), xdte_b[:, g * gw:(g + 1) * gw])
        for pr in range(heads_per_group // 2):
            ms = []
            for e in range(2):
                h = g * heads_per_group + pr * 2 + e
                seg = a_cum[:, h:h + 1] - a_cum_t[h:h + 1, :]
                dec = jnp.where(causal, jnp.exp(jnp.minimum(seg, 0.0)), 0.0)
                ms.append(_bf(cb * dec))
            lhs = jnp.concatenate(ms, axis=1)
            lo = g * gw + pr * LANES
            xp = xdt_b[:, lo:lo + LANES]
            zero = jnp.zeros_like(xp)
            rhs = jnp.concatenate([jnp.where(lower_half, xp, zero),
                                   jnp.where(lower_half, zero, xp)], axis=0)
            y_ref[:, lo:lo + LANES] = _dot(lhs, rhs) + y_off[:, pr * LANES:(pr + 1) * LANES]
        h_ref[g] = h_g * ea_f[L - 1:L, g * gw:(g + 1) * gw] + new_state

    y = y_ref[...] + xs * dskip_ref[...]
    zz = z_ref[...]
    u = y * (zz * jax.nn.sigmoid(zz))
    for g in range(SSD_GROUPS):
        ug = u[:, g * gw:(g + 1) * gw]
        ms_ = jnp.mean(ug * ug, axis=-1, keepdims=True)
        o_ref[:, g * gw:(g + 1) * gw] = (
            ug * lax.rsqrt(ms_ + RMS_EPS) * nw_ref[:, g * gw:(g + 1) * gw]).astype(o_ref.dtype)


def _ssd_mixer(proj, bsz, t, cwx, cbx, cwb, cbb, dtb, alog, dskip, nw, expand):
    nc = t // CHUNK
    rows = lambda b, c: b * nc + c
    const = lambda shape: pl.BlockSpec(shape, lambda b, c: (0,) * len(shape))
    return pl.pallas_call(
        _ssd_body,
        grid=(bsz, nc),
        in_specs=[
            pl.BlockSpec((CHUNK, COLBLK), lambda b, c: (rows(b, c), BLK_Z)),
            pl.BlockSpec((CHUNK, COLBLK), lambda b, c: (rows(b, c), BLK_XS)),
            pl.BlockSpec((CHUNK, SMALLBLK), lambda b, c: (rows(b, c), BLK_BC)),
            pl.BlockSpec((CHUNK, SMALLBLK), lambda b, c: (rows(b, c), BLK_MISC)),
            const(cwx.shape), const(cbx.shape), const(cwb.shape), const(cbb.shape),
            const(dtb.shape), const(alog.shape), const(dskip.shape), const(nw.shape),
            const(expand.shape),
        ],
        out_specs=pl.BlockSpec((CHUNK, SSD_WIDTH), lambda b, c: (rows(b, c), 0)),
        out_shape=jax.ShapeDtypeStruct((bsz * t, SSD_WIDTH), BF16),
        scratch_shapes=[
            pltpu.VMEM((HALO + CHUNK, SSD_WIDTH), F32),
            pltpu.VMEM((HALO + CHUNK, SMALLBLK), F32),
            pltpu.VMEM((SSD_GROUPS, SSD_STATE, SSD_WIDTH // SSD_GROUPS), F32),
            pltpu.VMEM((CHUNK, SSD_WIDTH), F32),
        ],
        compiler_params=_params(("parallel", "arbitrary")),
        name="ssd_mixer",
    )(proj, proj, proj, proj, cwx, cbx, cwb, cbb, dtb, alog, dskip, nw, expand)


PAIRS = RWKV_HEADS // 2
PAIRS_PER_ITER = 2
ST_AMID, ST_RMID, ST_BMID, ST_KMID, ST_AST, ST_RST, ST_BEND, ST_KEND, ST_V, ST_N = range(10)


def _rwkv_body(has_vres, *refs):
    if has_vres:
        (r_ref, k_ref, v_ref, misc_ref, vfirst_ref, mix_ref, mixm_ref, w0_ref, wup_ref, a0_ref,
         aup_ref, gup_ref, v0_ref, vup_ref, kk_ref, ka_ref, rk_ref, lnw_ref, lnb_ref, hs_ref,
         he_ref, o_ref, r_buf, k_buf, v_buf, m_buf, st_ref, etot_ref, h_ref, y_ref) = refs
    else:
        (r_ref, k_ref, v_ref, misc_ref, mix_ref, mixm_ref, w0_ref, wup_ref, a0_ref,
         aup_ref, gup_ref, kk_ref, ka_ref, rk_ref, lnw_ref, lnb_ref, hs_ref,
         he_ref, o_ref, vfirst_out_ref, r_buf, k_buf, v_buf, m_buf, st_ref, etot_ref, h_ref,
         y_ref) = refs
    L = CHUNK
    c = pl.program_id(1)

    @pl.when(c == 0)
    def _init():
        for buf in (r_buf, k_buf, v_buf, m_buf):
            buf[0:HALO, :] = jnp.zeros((HALO, buf.shape[1]), F32)
        h_ref[...] = jnp.zeros_like(h_ref)

    def shift_lerp(ref, buf, mix):
        cur = ref[...]
        buf[HALO:HALO + L, :] = cur
        prev = buf[pl.ds(HALO - 1, L), :]
        buf[0:HALO, :] = buf[L:L + HALO, :]
        return cur + (prev - cur) * mix

    w_ = RWKV_WIDTH
    r = shift_lerp(r_ref, r_buf, mix_ref[:, 0:w_])
    k = shift_lerp(k_ref, k_buf, mix_ref[:, w_:2 * w_])
    v = shift_lerp(v_ref, v_buf, mix_ref[:, 2 * w_:3 * w_])
    m = shift_lerp(misc_ref, m_buf, mixm_ref[...])
    wa = m[:, MISC_WA:MISC_WA + LANES]
    gv = m[:, MISC_GV:MISC_GV + 2 * LANES]

    w_lin = w0_ref[...] + _dot(jnp.tanh(wa), wup_ref[...], HIGHEST)
    log_w = -jnp.exp(-_softplus(-w_lin) - 0.5)
    a = jax.nn.sigmoid(a0_ref[...] + _dot(_bf(wa), aup_ref[...]))
    g = _dot(_bf(jax.nn.sigmoid(gv)), gup_ref[...])
    if has_vres:
        v = v + (vfirst_ref[...] - v) * jax.nn.sigmoid(v0_ref[...] + _dot(_bf(gv), vup_ref[...]))
    else:
        vfirst_out_ref[...] = v

    def head_sum(x):
        return _dot(_dot(x, hs_ref[...], HIGHEST), he_ref[...], HIGHEST)

    kx = k * kk_ref[...]
    kk = kx / jnp.maximum(jnp.sqrt(head_sum(kx * kx)), L2_EPS)
    k2 = k * (1.0 + (a - 1.0) * ka_ref[...])
    av = -kk
    bv = kk * a

    row = lax.broadcasted_iota(jnp.int32, (L, L), 0)
    col = lax.broadcasted_iota(jnp.int32, (L, L), 1)
    incl = row >= col
    strict = row > col
    cum = _dot(incl.astype(F32), log_w, HIGHEST)
    cum_prev = cum - log_w
    tot = cum[L - 1:L, :]
    mid = cum[L // 2 - 1:L // 2, :]
    st_ref[ST_AMID] = av * jnp.exp(cum_prev - mid)
    st_ref[ST_RMID] = r * jnp.exp(cum - mid)
    inv_mid = jnp.exp(mid - cum)
    st_ref[ST_BMID] = bv * inv_mid
    st_ref[ST_KMID] = k2 * inv_mid
    st_ref[ST_AST] = av * jnp.exp(cum_prev)
    st_ref[ST_RST] = r * jnp.exp(cum)
    to_end = jnp.exp(tot - cum)
    st_ref[ST_BEND] = bv * to_end
    st_ref[ST_KEND] = k2 * to_end
    st_ref[ST_V] = v
    etot_ref[...] = jnp.broadcast_to(jnp.exp(tot), etot_ref.shape)

    lane_l = lax.broadcasted_iota(jnp.int32, (L, LANES), 1)
    r2 = lax.broadcasted_iota(jnp.int32, (LANES, LANES), 0)
    c2 = lax.broadcasted_iota(jnp.int32, (LANES, LANES), 1)
    same_head = (r2 < HEAD_DIM) == (c2 < HEAD_DIM)
    diag = r2 == c2

    def pair_step(p):
        sl = pl.ds(pl.multiple_of(p * LANES, LANES), LANES)
        a_mid = st_ref[ST_AMID, :, sl]
        r_mid = st_ref[ST_RMID, :, sl]
        bk = _bf(jnp.concatenate([st_ref[ST_BMID, :, sl], st_ref[ST_KMID, :, sl]], axis=0))
        a_st = st_ref[ST_AST, :, sl]
        r_st = st_ref[ST_RST, :, sl]
        v_b = _bf(st_ref[ST_V, :, sl])
        a_bar = jnp.zeros((L, LANES), F32)
        r_bar = r_st
        uv = None
        yv = None
        for h in range(2):
            mh = (lane_l < HEAD_DIM) if h == 0 else (lane_l >= HEAD_DIM)
            ar = _bf(jnp.concatenate([jnp.where(mh, a_mid, 0.0), jnp.where(mh, r_mid, 0.0)], axis=0))
            aa = _dot_nt(ar, bk)
            a_ab = jnp.where(strict, aa[0:L, 0:L], 0.0)
            a_ak = jnp.where(strict, aa[0:L, L:2 * L], 0.0)
            a_rb = jnp.where(incl, aa[L:2 * L, 0:L], 0.0)
            a_rk = jnp.where(incl, aa[L:2 * L, L:2 * L], 0.0)
            x = jnp.concatenate([jnp.where(mh, a_st, 0.0), _dot(_bf(a_ak), v_b)], axis=1)
            mm = a_ab
            n_rounds = L.bit_length() - 1
            for i in range(n_rounds):
                mm_b = _bf(mm)
                x = x + _dot(mm_b, _bf(x))
                if i + 1 < n_rounds:
                    mm = _dot(mm_b, mm_b)
            zz = _dot(_bf(a_rb), _bf(x))
            a_bar = a_bar + x[:, 0:LANES]
            r_bar = r_bar + zz[:, 0:LANES]
            yv_h = zz[:, LANES:2 * LANES] + _dot(_bf(a_rk), v_b)
            uv = x[:, LANES:2 * LANES] if h == 0 else jnp.where(mh, x[:, LANES:2 * LANES], uv)
            yv = yv_h if h == 0 else jnp.where(mh, yv_h, yv)
        b_end_t = _bf(st_ref[ST_BEND, :, sl].T)
        k_end_t = _bf(st_ref[ST_KEND, :, sl].T)
        e_tot = etot_ref[:, sl]
        g_mat = jnp.where(same_head, _dot(b_end_t, _bf(a_bar)), 0.0) + jnp.where(diag, e_tot, 0.0)
        h_add = jnp.where(same_head, _dot(b_end_t, _bf(uv)) + _dot(k_end_t, v_b), 0.0)
        h_old = _bf(h_ref[p])
        y_ref[:, sl] = _dot(_bf(r_bar), h_old) + yv
        h_ref[p] = _dot(_bf(g_mat), h_old) + h_add

    def loop_body(i, carry):
        for q in range(PAIRS_PER_ITER):
            pair_step(i * PAIRS_PER_ITER + q)
        return carry

    lax.fori_loop(0, PAIRS // PAIRS_PER_ITER, loop_body, 0)

    y = y_ref[...]
    inv_n = 1.0 / HEAD_DIM
    mu = head_sum(y) * inv_n
    d = y - mu
    var = head_sum(d * d) * inv_n
    y = d * lax.rsqrt(var + RWKV_GN_EPS) * lnw_ref[...] + lnb_ref[...]
    y = y + head_sum(r * k2 * rk_ref[...]) * v
    o_ref[...] = (y * g).astype(o_ref.dtype)


def _rwkv_mixer(proj, bsz, t, vfirst, mix, mixm, w0, wup, a0, aup, gup, v0, vup, k_k, k_a, r_k,
                ln_w, ln_b, hs, he):
    nc = t // CHUNK
    rows = lambda b, c: b * nc + c
    const = lambda x: pl.BlockSpec(x.shape, lambda b, c: (0,) * x.ndim)
    has_vres = vfirst is not None
    act_specs = [
        pl.BlockSpec((CHUNK, COLBLK), lambda b, c: (rows(b, c), BLK_R)),
        pl.BlockSpec((CHUNK, COLBLK), lambda b, c: (rows(b, c), BLK_K)),
        pl.BlockSpec((CHUNK, COLBLK), lambda b, c: (rows(b, c), BLK_V)),
        pl.BlockSpec((CHUNK, SMALLBLK), lambda b, c: (rows(b, c), BLK_MISC)),
    ]
    wide_spec = pl.BlockSpec((CHUNK, RWKV_WIDTH), lambda b, c: (rows(b, c), 0))
    if has_vres:
        args = [proj, proj, proj, proj, vfirst, mix, mixm, w0, wup, a0, aup, gup, v0, vup,
                k_k, k_a, r_k, ln_w, ln_b, hs, he]
        in_specs = act_specs + [wide_spec] + [const(x) for x in args[5:]]
        out_specs = wide_spec
        out_shape = jax.ShapeDtypeStruct((bsz * t, RWKV_WIDTH), BF16)
    else:
        args = [proj, proj, proj, proj, mix, mixm, w0, wup, a0, aup, gup,
                k_k, k_a, r_k, ln_w, ln_b, hs, he]
        in_specs = act_specs + [const(x) for x in args[4:]]
        out_specs = (wide_spec, wide_spec)
        out_shape = (jax.ShapeDtypeStruct((bsz * t, RWKV_WIDTH), BF16),
                     jax.ShapeDtypeStruct((bsz * t, RWKV_WIDTH), F32))
    return pl.pallas_call(
        functools.partial(_rwkv_body, has_vres),
        grid=(bsz, nc),
        in_specs=in_specs,
        out_specs=out_specs,
        out_shape=out_shape,
        scratch_shapes=[
            pltpu.VMEM((HALO + CHUNK, RWKV_WIDTH), F32),
            pltpu.VMEM((HALO + CHUNK, RWKV_WIDTH), F32),
            pltpu.VMEM((HALO + CHUNK, RWKV_WIDTH), F32),
            pltpu.VMEM((HALO + CHUNK, SMALLBLK), F32),
            pltpu.VMEM((ST_N, CHUNK, RWKV_WIDTH), F32),
            pltpu.VMEM((LANES, RWKV_WIDTH), F32),
            pltpu.VMEM((PAIRS, LANES, LANES), F32),
            pltpu.VMEM((CHUNK, RWKV_WIDTH), F32),
        ],
        compiler_params=_params(("parallel", "arbitrary")),
        name="rwkv_mixer",
    )(*args)


def _outproj_ln_body(ys_ref, yr_ref, wt_ref, wb_ref, x_ref, g_ref, b_ref, of_ref, ob_ref):
    mixed = _dot(ys_ref[...], wt_ref[...]) + _dot(yr_ref[...], wb_ref[...])
    out = _layer_norm(DEEPNORM_ALPHA * x_ref[...] + mixed, g_ref[...], b_ref[...])
    of_ref[...] = out
    ob_ref[...] = out.astype(BF16)


def _outproj_ln(ys, yr, w_out, x, g, b, bm):
    n = x.shape[0]
    half = w_out.shape[0] // 2
    row = lambda width: pl.BlockSpec((bm, width), lambda i: (i, 0))
    vec = pl.BlockSpec((1, D_MODEL), lambda i: (0, 0))
    return pl.pallas_call(
        _outproj_ln_body,
        grid=(n // bm,),
        in_specs=[row(half), row(half),
                  pl.BlockSpec((half, D_MODEL), lambda i: (0, 0)),
                  pl.BlockSpec((half, D_MODEL), lambda i: (1, 0)),
                  row(D_MODEL), vec, vec],
        out_specs=(row(D_MODEL), row(D_MODEL)),
        out_shape=(jax.ShapeDtypeStruct((n, D_MODEL), F32),
                   jax.ShapeDtypeStruct((n, D_MODEL), BF16)),
        compiler_params=_params(("parallel",)),
        name="out_proj_ln",
    )(ys, yr, w_out, w_out, x, g, b)


def _down_ln_body(h_ref, w_ref, x_ref, g_ref, b_ref, of_ref, ob_ref, acc_ref):
    kk = pl.program_id(1)

    @pl.when(kk == 0)
    def _zero():
        acc_ref[...] = jnp.zeros_like(acc_ref)

    acc_ref[...] += _dot(h_ref[...], w_ref[...])

    @pl.when(kk == pl.num_programs(1) - 1)
    def _finish():
        out = _layer_norm(DEEPNORM_ALPHA * x_ref[...] + acc_ref[...], g_ref[...], b_ref[...])
        of_ref[...] = out
        ob_ref[...] = out.astype(BF16)


def _down_ln(h, w, x, g, b, bm, bk):
    n, kdim = h.shape
    row = pl.BlockSpec((bm, D_MODEL), lambda i, k: (i, 0))
    vec = pl.BlockSpec((1, D_MODEL), lambda i, k: (0, 0))
    return pl.pallas_call(
        _down_ln_body,
        grid=(n // bm, kdim // bk),
        in_specs=[pl.BlockSpec((bm, bk), lambda i, k: (i, k)),
                  pl.BlockSpec((bk, D_MODEL), lambda i, k: (k, 0)),
                  row, vec, vec],
        out_specs=(row, row),
        out_shape=(jax.ShapeDtypeStruct((n, D_MODEL), F32),
                   jax.ShapeDtypeStruct((n, D_MODEL), BF16)),
        scratch_shapes=[pltpu.VMEM((bm, D_MODEL), F32)],
        compiler_params=_params(("parallel", "arbitrary")),
        name="down_proj_ln",
    )(h, w, x, g, b)


def _swiglu_up_body(x_ref, w1_ref, w3_ref, o_ref):
    xb = x_ref[...]
    h1 = _dot(xb, w1_ref[...])
    h3 = _dot(xb, w3_ref[...])
    o_ref[...] = (h1 * jax.nn.sigmoid(h1) * h3).astype(o_ref.dtype)


def _swiglu_up(xb, w1, w3, bm, bn):
    n = xb.shape[0]
    f = w1.shape[1]
    return pl.pallas_call(
        _swiglu_up_body,
        grid=(n // bm, f // bn),
        in_specs=[pl.BlockSpec((bm, D_MODEL), lambda i, j: (i, 0)),
                  pl.BlockSpec((D_MODEL, bn), lambda i, j: (0, j)),
                  pl.BlockSpec((D_MODEL, bn), lambda i, j: (0, j))],
        out_specs=pl.BlockSpec((bm, bn), lambda i, j: (i, j)),
        out_shape=jax.ShapeDtypeStruct((n, f), BF16),
        compiler_params=_params(("parallel", "parallel")),
        name="swiglu_up",
    )(xb, w1, w3)


def _router_body(x_ref, wr_ref, o_ref):
    logits = _dot(x_ref[...], wr_ref[...], HIGHEST)
    lane = lax.broadcasted_iota(jnp.int32, logits.shape, 1)
    neg = jnp.float32(-jnp.inf)
    logits = jnp.where(lane < N_EXPERTS, logits, neg)
    top1 = jnp.max(logits, axis=-1, keepdims=True)
    idx1 = jnp.min(jnp.where(logits == top1, lane, LANES), axis=-1, keepdims=True)
    rest = jnp.where(lane == idx1, neg, logits)
    top2 = jnp.max(rest, axis=-1, keepdims=True)
    idx2 = jnp.min(jnp.where(rest == top2, lane, LANES), axis=-1, keepdims=True)
    e2 = jnp.exp(top2 - top1)
    gate1 = 1.0 / (1.0 + e2)
    gate2 = e2 / (1.0 + e2)
    o_ref[...] = jnp.where(lane == idx1, gate1, 0.0) + jnp.where(lane == idx2, gate2, 0.0)


def _router(x, wr, bm):
    n = x.shape[0]
    return pl.pallas_call(
        _router_body,
        grid=(n // bm,),
        in_specs=[pl.BlockSpec((bm, D_MODEL), lambda i: (i, 0)),
                  pl.BlockSpec((D_MODEL, LANES), lambda i: (0, 0))],
        out_specs=pl.BlockSpec((bm, LANES), lambda i: (i, 0)),
        out_shape=jax.ShapeDtypeStruct((n, LANES), F32),
        compiler_params=_params(("parallel",)),
        name="moe_router",
    )(x, wr)


def _moe_up_body(x_ref, comb_ref, w1_ref, w3_ref, o_ref):
    e = pl.program_id(1)
    xb = x_ref[...]
    h1 = _dot(xb, w1_ref[...])
    h3 = _dot(xb, w3_ref[...])
    comb = comb_ref[...]
    lane = lax.broadcasted_iota(jnp.int32, comb.shape, 1)
    gate = jnp.sum(jnp.where(lane == e, comb, 0.0), axis=-1, keepdims=True)
    o_ref[...] = (h1 * jax.nn.sigmoid(h1) * h3 * gate).astype(o_ref.dtype)


def _moe_up(xb, comb, w1, w3, bm, bn):
    n = xb.shape[0]
    ne, _, fe = w1.shape
    nj = fe // bn
    wspec = pl.BlockSpec((None, D_MODEL, bn), lambda i, e, j: (e, 0, j))
    return pl.pallas_call(
        _moe_up_body,
        grid=(n // bm, ne, nj),
        in_specs=[pl.BlockSpec((bm, D_MODEL), lambda i, e, j: (i, 0)),
                  pl.BlockSpec((bm, LANES), lambda i, e, j: (i, 0)),
                  wspec, wspec],
        out_specs=pl.BlockSpec((bm, bn), lambda i, e, j: (i, e * nj + j)),
        out_shape=jax.ShapeDtypeStruct((n, ne * fe), BF16),
        compiler_params=_params(("parallel", "parallel", "parallel")),
        name="moe_up",
    )(xb, comb, w1, w3)


def _pad_lanes(v, width):
    v = v.reshape(1, -1)
    return jnp.pad(v, ((0, 0), (0, width - v.shape[1])))


def _pick(n, pref):
    return pref if n % pref == 0 else n


def kernel(x, w_in, w_in_vres, ssd_conv_w, ssd_conv_b, ssd_dt_bias, ssd_a_log, ssd_d, ssd_norm_w, rw_mix, rw_vres_mix, rw_w0, rw_w_up, rw_a0, rw_a_up, rw_v0, rw_v_up, rw_g_up, rw_k_k, rw_k_a, rw_r_k, rw_ln_w, rw_ln_b, w_out, ln1_g, ln1_b, ln2_g, ln2_b, ffn_w1, ffn_w3, ffn_w2, moe_router, moe_w1, moe_w3, moe_w2):
    bsz, t, d = x.shape
    n = bsz * t
    depth = w_in.shape[0]
    assert d == D_MODEL and t % CHUNK == 0
    xf = x.reshape(n, d)
    xb = xf.astype(BF16)

    head_of_lane = jnp.arange(RWKV_WIDTH) // HEAD_DIM
    expand = (jnp.arange(LANES)[:, None] == head_of_lane[None, :]).astype(F32)
    reduce_ = expand.T

    bm_big = _pick(n, 1024)
    bm_mid = _pick(n, 512)
    bm_small = _pick(n, 256)
    row = lambda v: v.reshape(1, -1)
    zcols = lambda k: jnp.zeros((d, k), F32)

    vfirst = None
    for l in range(depth):
        w = w_in[l]
        o = 0
        segs = {}
        for name, width in (("z", SSD_WIDTH), ("xs", SSD_WIDTH), ("bc", 2 * SSD_GROUPS * SSD_STATE),
                            ("dt", SSD_HEADS), ("r", RWKV_WIDTH), ("k", RWKV_WIDTH), ("v", RWKV_WIDTH),
                            ("w_lo", W_LORA), ("a_lo", A_LORA), ("g_lo", G_LORA)):
            segs[name] = w[:, o:o + width]
            o += width
        v_lo_cols = w_in_vres[l - 1] if l > 0 else zcols(V_LORA)
        misc_cols = jnp.concatenate(
            [segs["dt"], zcols(MISC_WA - SSD_HEADS), segs["w_lo"], segs["a_lo"], segs["g_lo"], v_lo_cols,
             zcols(SMALLBLK - MISC_GV - G_LORA - V_LORA)], axis=1)
        w_comb = jnp.concatenate([segs["z"], segs["xs"], segs["r"], segs["k"], segs["v"], segs["bc"],
                                  misc_cols], axis=1).astype(BF16)

        proj = _matmul(xb, w_comb, bm_big, COLBLK, F32)

        cw, cb = ssd_conv_w[l], ssd_conv_b[l]
        y_ssd = _ssd_mixer(
            proj, bsz, t, cw[:, :SSD_WIDTH], row(cb[:SSD_WIDTH]), cw[:, SSD_WIDTH:], row(cb[SSD_WIDTH:]),
            _pad_lanes(ssd_dt_bias[l], LANES), _pad_lanes(ssd_a_log[l], LANES),
            row(jnp.repeat(ssd_d[l], HEAD_DIM)), row(ssd_norm_w[l]), expand)

        mix = rw_mix[l]
        v_mix = rw_vres_mix[l - 1] if l > 0 else jnp.zeros((V_LORA,), F32)
        mix_misc = jnp.concatenate([jnp.zeros((MISC_WA,), F32), mix[3 * RWKV_WIDTH:], v_mix,
                                    jnp.zeros((SMALLBLK - MISC_GV - G_LORA - V_LORA,), F32)])
        wup = jnp.pad(rw_w_up[l], ((0, LANES - W_LORA), (0, 0)))
        aup = jnp.pad(rw_a_up[l], ((W_LORA, 0), (0, 0))).astype(BF16)
        gup = jnp.pad(rw_g_up[l], ((0, 2 * LANES - G_LORA), (0, 0))).astype(BF16)
        if l > 0:
            v0 = row(rw_v0[l - 1])
            vup = jnp.pad(rw_v_up[l - 1], ((G_LORA, 2 * LANES - G_LORA - V_LORA), (0, 0))).astype(BF16)
        else:
            v0 = vup = None
        res = _rwkv_mixer(
            proj, bsz, t, vfirst, row(mix[:3 * RWKV_WIDTH]), row(mix_misc), row(rw_w0[l]), wup,
            row(rw_a0[l]), aup, gup, v0, vup, row(rw_k_k[l]), row(rw_k_a[l]), row(rw_r_k[l]),
            row(rw_ln_w[l]), row(rw_ln_b[l]), reduce_, expand)
        if l == 0:
            y_rw, vfirst = res
        else:
            y_rw = res

        xf, xb = _outproj_ln(y_ssd, y_rw, w_out[l].astype(BF16), xf, row(ln1_g[l]), row(ln1_b[l]), bm_small)

        if l % 2 == 0:
            i = l // 2
            h = _swiglu_up(xb, ffn_w1[i].astype(BF16), ffn_w3[i].astype(BF16), bm_big, 512)
            w2 = ffn_w2[i].astype(BF16)
        else:
            i = l // 2
            comb = _router(xf, jnp.pad(moe_router[i], ((0, 0), (0, LANES - N_EXPERTS))), bm_mid)
            h = _moe_up(xb, comb, moe_w1[i].astype(BF16), moe_w3[i].astype(BF16), bm_mid, 1408)
            w2 = moe_w2[i].reshape(-1, d).astype(BF16)
        xf, xb = _down_ln(h, w2, xf, row(ln2_g[l]), row(ln2_b[l]), bm_mid, 1408)

    return xf.reshape(bsz, t, d)
```

```python
import functools

import jax
import jax.numpy as jnp
from jax import lax
from jax.experimental import pallas as pl
from jax.experimental.pallas import tpu as pltpu

F32 = jnp.float32
BF16 = jnp.bfloat16
HIGHEST = lax.Precision.HIGHEST

D_MODEL = 2048
HEAD_DIM = 64
SSD_WIDTH = 1024
RWKV_WIDTH = 1024
SSD_HEADS = SSD_WIDTH // HEAD_DIM
SSD_GROUPS = 2
SSD_STATE = 128
SSD_CONV_WIDTH = 4
RWKV_HEADS = RWKV_WIDTH // HEAD_DIM
W_LORA = 64
A_LORA = 64
V_LORA = 32
G_LORA = 160
N_EXPERTS = 8
DEPTH = 4
DEEPNORM_ALPHA = (2 * DEPTH) ** 0.25
LN_EPS = 1e-5
RMS_EPS = 1e-5
RWKV_GN_EPS = 64e-5
L2_EPS = 1e-12

LANES = 128
SUBLANES = 8
VMEM_LIMIT = 56 * 1024 * 1024

CHUNK = 128
HALO = SUBLANES

COLBLK = 1024
BLK_Z, BLK_XS, BLK_R, BLK_K, BLK_V = 0, 1, 2, 3, 4
SMALLBLK = 512
BLK_BC = 10
BLK_MISC = 11
PROJ_WIDTH = 6 * COLBLK
MISC_WA = 128
MISC_GV = 256


def _params(sem, vmem=VMEM_LIMIT):
    return pltpu.CompilerParams(dimension_semantics=sem, vmem_limit_bytes=vmem)


def _dot(a, b, precision=None):
    return jnp.dot(a, b, preferred_element_type=F32, precision=precision)


def _dot_nt(a, b):
    return lax.dot_general(a, b, (((1,), (1,)), ((), ())), preferred_element_type=F32)


def _bf(x):
    return x.astype(BF16)


def _softplus(x):
    return jnp.maximum(x, 0.0) + jnp.log1p(jnp.exp(-jnp.abs(x)))


def _layer_norm(y, g, b):
    mu = jnp.mean(y, axis=-1, keepdims=True)
    d = y - mu
    var = jnp.mean(d * d, axis=-1, keepdims=True)
    return d * lax.rsqrt(var + LN_EPS) * g + b


def _mm_body(x_ref, w_ref, o_ref):
    o_ref[...] = _dot(x_ref[...], w_ref[...]).astype(o_ref.dtype)


def _matmul(x, w, bm, bn, out_dtype):
    m, k = x.shape
    n = w.shape[1]
    return pl.pallas_call(
        _mm_body,
        grid=(m // bm, n // bn),
        in_specs=[pl.BlockSpec((bm, k), lambda i, j: (i, 0)),
                  pl.BlockSpec((k, bn), lambda i, j: (0, j))],
        out_specs=pl.BlockSpec((bm, bn), lambda i, j: (i, j)),
        out_shape=jax.ShapeDtypeStruct((m, n), out_dtype),
        compiler_params=_params(("parallel", "parallel")),
        name="in_proj",
    )(x, w)


def _ssd_body(z_ref, xs_ref, bc_ref, misc_ref, cwx_ref, cbx_ref, cwb_ref, cbb_ref, dtb_ref,
              alog_ref, dskip_ref, nw_ref, e_ref, o_ref, xs_buf, bc_buf, h_ref, y_ref):
    L = CHUNK
    c = pl.program_id(1)

    @pl.when(c == 0)
    def _init():
        xs_buf[0:HALO, :] = jnp.zeros((HALO, xs_buf.shape[1]), F32)
        bc_buf[0:HALO, :] = jnp.zeros((HALO, bc_buf.shape[1]), F32)
        h_ref[...] = jnp.zeros_like(h_ref)

    xs_buf[HALO:HALO + L, :] = xs_ref[...]
    bc_buf[HALO:HALO + L, :] = bc_ref[...]

    def conv_silu(buf, w_ref, b_ref):
        acc = b_ref[...]
        for i in range(SSD_CONV_WIDTH):
            acc = acc + buf[pl.ds(HALO - (SSD_CONV_WIDTH - 1) + i, L), :] * w_ref[i:i + 1, :]
        return acc * jax.nn.sigmoid(acc)

    xs = conv_silu(xs_buf, cwx_ref, cbx_ref)
    bcv = conv_silu(bc_buf, cwb_ref, cbb_ref)
    xs_buf[0:HALO, :] = xs_buf[L:L + HALO, :]
    bc_buf[0:HALO, :] = bc_buf[L:L + HALO, :]

    lane = lax.broadcasted_iota(jnp.int32, (1, LANES), 1)
    dt = _softplus(misc_ref[:, 0:LANES] + dtb_ref[...])
    a_neg = jnp.where(lane < SSD_HEADS, -jnp.exp(alog_ref[...]), 0.0)
    d_a = dt * a_neg
    row = lax.broadcasted_iota(jnp.int32, (L, L), 0)
    col = lax.broadcasted_iota(jnp.int32, (L, L), 1)
    causal = row >= col
    a_cum = _dot(causal.astype(F32), d_a, HIGHEST)
    a_cum_t = a_cum.T
    a_last = a_cum[L - 1:L, :]
    per_head = jnp.concatenate([dt, jnp.exp(a_last - a_cum), jnp.exp(a_cum)], axis=0)
    full = _dot(per_head, e_ref[...], HIGHEST)
    dt_f, dte_f, ea_f = full[0:L], full[L:2 * L], full[2 * L:3 * L]
    xdt = xs * dt_f
    xdt_b = _bf(xdt)
    xdte_b = _bf(xdt * dte_f)
    lane_l = lax.broadcasted_iota(jnp.int32, (L, LANES), 1)
    lower_half = lane_l < HEAD_DIM
    gw = SSD_WIDTH // SSD_GROUPS
    heads_per_group = SSD_HEADS // SSD_GROUPS
    for g in range(SSD_GROUPS):
        b_g = bcv[:, g * SSD_STATE:(g + 1) * SSD_STATE]
        c_g = _bf(bcv[:, (SSD_GROUPS + g) * SSD_STATE:(SSD_GROUPS + g + 1) * SSD_STATE])
        cb = _dot_nt(c_g, _bf(b_g))
        h_g = h_ref[g]
        y_off = _dot(c_g, _bf(h_g)) * ea_f[:, g * gw:(g + 1) * gw]
        new_state = _dot(_bf(b_g.T), xdte_b[:, g * gw:(g + 1) * gw])
        for pr in range(heads_per_group // 2):
            ms = []
            for e in range(2):
                h = g * heads_per_group + pr * 2 + e
                seg = a_cum[:, h:h + 1] - a_cum_t[h:h + 1, :]
                dec = jnp.where(causal, jnp.exp(jnp.minimum(seg, 0.0)), 0.0)
                ms.append(_bf(cb * dec))
            lhs = jnp.concatenate(ms, axis=1)
            lo = g * gw + pr * LANES
            xp = xdt_b[:, lo:lo + LANES]
            zero = jnp.zeros_like(xp)
            rhs = jnp.concatenate([jnp.where(lower_half, xp, zero),
                                   jnp.where(lower_half, zero, xp)], axis=0)
            y_ref[:, lo:lo + LANES] = _dot(lhs, rhs) + y_off[:, pr * LANES:(pr + 1) * LANES]
        h_ref[g] = h_g * ea_f[L - 1:L, g * gw:(g + 1) * gw] + new_state

    y = y_ref[...] + xs * dskip_ref[...]
    zz = z_ref[...]
    u = y * (zz * jax.nn.sigmoid(zz))
    for g in range(SSD_GROUPS):
        ug = u[:, g * gw:(g + 1) * gw]
        ms_ = jnp.mean(ug * ug, axis=-1, keepdims=True)
        o_ref[:, g * gw:(g + 1) * gw] = (
            ug * lax.rsqrt(ms_ + RMS_EPS) * nw_ref[:, g * gw:(g + 1) * gw]).astype(o_ref.dtype)


def _ssd_mixer(proj, bsz, t, cwx, cbx, cwb, cbb, dtb, alog, dskip, nw, expand):
    nc = t // CHUNK
    rows = lambda b, c: b * nc + c
    const = lambda shape: pl.BlockSpec(shape, lambda b, c: (0,) * len(shape))
    return pl.pallas_call(
        _ssd_body,
        grid=(bsz, nc),
        in_specs=[
            pl.BlockSpec((CHUNK, COLBLK), lambda b, c: (rows(b, c), BLK_Z)),
            pl.BlockSpec((CHUNK, COLBLK), lambda b, c: (rows(b, c), BLK_XS)),
            pl.BlockSpec((CHUNK, SMALLBLK), lambda b, c: (rows(b, c), BLK_BC)),
            pl.BlockSpec((CHUNK, SMALLBLK), lambda b, c: (rows(b, c), BLK_MISC)),
            const(cwx.shape), const(cbx.shape), const(cwb.shape), const(cbb.shape),
            const(dtb.shape), const(alog.shape), const(dskip.shape), const(nw.shape),
            const(expand.shape),
        ],
        out_specs=pl.BlockSpec((CHUNK, SSD_WIDTH), lambda b, c: (rows(b, c), 0)),
        out_shape=jax.ShapeDtypeStruct((bsz * t, SSD_WIDTH), BF16),
        scratch_shapes=[
            pltpu.VMEM((HALO + CHUNK, SSD_WIDTH), F32),
            pltpu.VMEM((HALO + CHUNK, SMALLBLK), F32),
            pltpu.VMEM((SSD_GROUPS, SSD_STATE, SSD_WIDTH // SSD_GROUPS), F32),
            pltpu.VMEM((CHUNK, SSD_WIDTH), F32),
        ],
        compiler_params=_params(("parallel", "arbitrary")),
        name="ssd_mixer",
    )(proj, proj, proj, proj, cwx, cbx, cwb, cbb, dtb, alog, dskip, nw, expand)


PAIRS = RWKV_HEADS // 2
MXU_WIDTH = 256
ST_AMID, ST_RMID, ST_BMID, ST_KMID, ST_AST, ST_RST, ST_BEND, ST_KEND, ST_V, ST_N = range(10)


def _split2(x):
    hi = x.astype(BF16)
    return hi, (x - hi.astype(F32)).astype(BF16)


def _split3(x):
    hi = x.astype(BF16)
    r1 = x - hi.astype(F32)
    mid = r1.astype(BF16)
    return hi, mid, (r1 - mid.astype(F32)).astype(BF16)


def _rwkv_body(has_vres, *refs):
    if has_vres:
        (r_ref, k_ref, v_ref, misc_ref, vfirst_ref, mix_ref, mixm_ref, w0_ref, wuph_ref, wupl_ref,
         a0_ref, aup_ref, gup_ref, v0_ref, vup_ref, kk_ref, ka_ref, rk_ref, lnw_ref, lnb_ref, bd_ref,
         o_ref, r_buf, k_buf, v_buf, m_buf, st_ref, h_ref, y_ref) = refs
    else:
        (r_ref, k_ref, v_ref, misc_ref, mix_ref, mixm_ref, w0_ref, wuph_ref, wupl_ref,
         a0_ref, aup_ref, gup_ref, kk_ref, ka_ref, rk_ref, lnw_ref, lnb_ref, bd_ref,
         o_ref, vfirst_out_ref, r_buf, k_buf, v_buf, m_buf, st_ref, h_ref, y_ref) = refs
    L = CHUNK
    c = pl.program_id(1)

    @pl.when(c == 0)
    def _init():
        for buf in (r_buf, k_buf, v_buf, m_buf):
            buf[0:HALO, :] = jnp.zeros((HALO, buf.shape[1]), F32)
        h_ref[...] = jnp.zeros_like(h_ref)

    def shift_lerp(ref, buf, mix):
        cur = ref[...]
        buf[HALO:HALO + L, :] = cur
        prev = buf[pl.ds(HALO - 1, L), :]
        buf[0:HALO, :] = buf[L:L + HALO, :]
        return cur + (prev - cur) * mix

    w_ = RWKV_WIDTH
    r = shift_lerp(r_ref, r_buf, mix_ref[:, 0:w_])
    k = shift_lerp(k_ref, k_buf, mix_ref[:, w_:2 * w_])
    v = shift_lerp(v_ref, v_buf, mix_ref[:, 2 * w_:3 * w_])
    m = shift_lerp(misc_ref, m_buf, mixm_ref[...])
    wa = m[:, MISC_WA:MISC_WA + LANES]
    gv = m[:, MISC_GV:MISC_GV + 2 * LANES]

    th_hi, th_lo = _split2(jnp.tanh(wa))
    wup_hi, wup_lo = wuph_ref[...], wupl_ref[...]
    w_lin = w0_ref[...] + (_dot(th_hi, wup_hi) + _dot(th_hi, wup_lo) + _dot(th_lo, wup_hi))
    log_w = -jnp.exp(-_softplus(-w_lin) - 0.5)
    a = jax.nn.sigmoid(a0_ref[...] + _dot(_bf(wa), aup_ref[...]))
    g = _dot(_bf(jax.nn.sigmoid(gv)), gup_ref[...])
    if has_vres:
        v = v + (vfirst_ref[...] - v) * jax.nn.sigmoid(v0_ref[...] + _dot(_bf(gv), vup_ref[...]))
    else:
        vfirst_out_ref[...] = v

    n_blk = RWKV_WIDTH // MXU_WIDTH

    def head_sum(x):
        parts = [t[:, j * MXU_WIDTH:(j + 1) * MXU_WIDTH] for t in _split2(x) for j in range(n_blk)]
        s = _dot(jnp.concatenate(parts, axis=0), bd_ref[...])
        return jnp.concatenate(
            [s[j * L:(j + 1) * L] + s[(n_blk + j) * L:(n_blk + j + 1) * L] for j in range(n_blk)], axis=1)

    kx = k * kk_ref[...]
    kk = kx / jnp.maximum(jnp.sqrt(head_sum(kx * kx)), L2_EPS)
    k2 = k * (1.0 + (a - 1.0) * ka_ref[...])
    av = -kk
    bv = kk * a

    row = lax.broadcasted_iota(jnp.int32, (L, L), 0)
    col = lax.broadcasted_iota(jnp.int32, (L, L), 1)
    incl = row >= col
    strict = row > col
    tri = incl.astype(BF16)
    lw_hi, lw_mid, lw_lo = _split3(log_w)
    cum = _dot(tri, lw_hi) + _dot(tri, lw_mid) + _dot(tri, lw_lo)
    cum_prev = cum - log_w
    tot = cum[L - 1:L, :]
    mid = cum[L // 2 - 1:L // 2, :]
    st_ref[ST_AMID] = av * jnp.exp(cum_prev - mid)
    st_ref[ST_RMID] = r * jnp.exp(cum - mid)
    inv_mid = jnp.exp(mid - cum)
    st_ref[ST_BMID] = bv * inv_mid
    st_ref[ST_KMID] = k2 * inv_mid
    st_ref[ST_AST] = av * jnp.exp(cum_prev)
    st_ref[ST_RST] = r * jnp.exp(cum)
    to_end = jnp.exp(tot - cum)
    st_ref[ST_BEND] = bv * to_end
    st_ref[ST_KEND] = k2 * to_end
    st_ref[ST_V] = v
    e_tot_all = jnp.exp(tot)

    lane_l = lax.broadcasted_iota(jnp.int32, (L, LANES), 1)
    lower = lane_l < HEAD_DIM
    r2 = lax.broadcasted_iota(jnp.int32, (LANES, LANES), 0)
    c2 = lax.broadcasted_iota(jnp.int32, (LANES, LANES), 1)
    same_head = (r2 < HEAD_DIM) == (c2 < HEAD_DIM)
    diag = r2 == c2
    n_rounds = L.bit_length() - 1

    lanes_of = [slice(p * LANES, (p + 1) * LANES) for p in range(PAIRS)]
    masks = (lower, jnp.logical_not(lower))
    heads = [(p, h) for p in range(PAIRS) for h in range(2)]
    v_b, v_sw, bk_t = [], [], []
    for sl in lanes_of:
        v_p = st_ref[ST_V, :, sl]
        v_b.append(_bf(v_p))
        v_sw.append(_bf(pltpu.roll(v_p, HEAD_DIM, 1)))
        bk_t.append(_bf(jnp.concatenate([st_ref[ST_BMID, :, sl].T, st_ref[ST_KMID, :, sl].T], axis=1)))
    zero_b = jnp.zeros_like(v_sw[0])

    mm, a_ak, a_rbk = [], [], []
    for p, h in heads:
        sl, mh = lanes_of[p], masks[h]
        ar = _bf(jnp.concatenate([jnp.where(mh, st_ref[ST_AMID, :, sl], 0.0),
                                  jnp.where(mh, st_ref[ST_RMID, :, sl], 0.0)], axis=0))
        aa = _dot(ar, bk_t[p])
        mm.append(_bf(jnp.where(strict, aa[0:L, 0:L], 0.0)))
        a_ak.append(_bf(jnp.where(strict, aa[0:L, L:2 * L], 0.0)))
        a_rbk.append(_bf(jnp.concatenate([jnp.where(incl, aa[L:2 * L, 0:L], 0.0),
                                          jnp.where(incl, aa[L:2 * L, L:2 * L], 0.0)], axis=1)))
    xs_ = [jnp.where(masks[h], st_ref[ST_AST, :, lanes_of[p]], _dot(a_ak[i], v_sw[p]))
           for i, (p, h) in enumerate(heads)]
    for rnd in range(n_rounds):
        xs_ = [x + _dot(m_, _bf(x)) for x, m_ in zip(xs_, mm)]
        if rnd + 1 < n_rounds:
            mm = [_bf(_dot(m_, m_)) for m_ in mm]
    yz = [_dot(a_rbk[i], jnp.concatenate([_bf(xs_[i]), jnp.where(masks[h], zero_b, v_sw[p])], axis=0))
          for i, (p, h) in enumerate(heads)]

    for p, sl in enumerate(lanes_of):
        x0, x1, yz0, yz1 = xs_[2 * p], xs_[2 * p + 1], yz[2 * p], yz[2 * p + 1]
        a_bar = jnp.where(lower, x0, x1)
        uv = pltpu.roll(jnp.where(lower, x1, x0), HEAD_DIM, 1)
        r_bar = st_ref[ST_RST, :, sl] + jnp.where(lower, yz0, yz1)
        yv = pltpu.roll(jnp.where(lower, yz1, yz0), HEAD_DIM, 1)
        b_end_t = _bf(st_ref[ST_BEND, :, sl].T)
        k_end_t = _bf(st_ref[ST_KEND, :, sl].T)
        g_mat = (jnp.where(same_head, _dot(b_end_t, _bf(a_bar)), 0.0)
                 + jnp.where(diag, e_tot_all[:, sl], 0.0))
        h_add = jnp.where(same_head, _dot(b_end_t, _bf(uv)) + _dot(k_end_t, v_b[p]), 0.0)
        h_old = _bf(h_ref[p])
        y_ref[:, sl] = _dot(_bf(r_bar), h_old) + yv
        h_ref[p] = _dot(_bf(g_mat), h_old) + h_add

    y = y_ref[...]
    inv_n = 1.0 / HEAD_DIM
    mu = head_sum(y) * inv_n
    d = y - mu
    var = head_sum(d * d) * inv_n
    y = d * lax.rsqrt(var + RWKV_GN_EPS) * lnw_ref[...] + lnb_ref[...]
    y = y + head_sum(r * k2 * rk_ref[...]) * v
    o_ref[...] = (y * g).astype(o_ref.dtype)


def _rwkv_mixer(proj, bsz, t, vfirst, mix, mixm, w0, wup_hi, wup_lo, a0, aup, gup, v0, vup, k_k, k_a,
                r_k, ln_w, ln_b, bd):
    nc = t // CHUNK
    rows = lambda b, c: b * nc + c
    const = lambda x: pl.BlockSpec(x.shape, lambda b, c: (0,) * x.ndim)
    has_vres = vfirst is not None
    act_specs = [
        pl.BlockSpec((CHUNK, COLBLK), lambda b, c: (rows(b, c), BLK_R)),
        pl.BlockSpec((CHUNK, COLBLK), lambda b, c: (rows(b, c), BLK_K)),
        pl.BlockSpec((CHUNK, COLBLK), lambda b, c: (rows(b, c), BLK_V)),
        pl.BlockSpec((CHUNK, SMALLBLK), lambda b, c: (rows(b, c), BLK_MISC)),
    ]
    wide_spec = pl.BlockSpec((CHUNK, RWKV_WIDTH), lambda b, c: (rows(b, c), 0))
    if has_vres:
        args = [proj, proj, proj, proj, vfirst, mix, mixm, w0, wup_hi, wup_lo, a0, aup, gup, v0, vup,
                k_k, k_a, r_k, ln_w, ln_b, bd]
        in_specs = act_specs + [wide_spec] + [const(x) for x in args[5:]]
        out_specs = wide_spec
        out_shape = jax.ShapeDtypeStruct((bsz * t, RWKV_WIDTH), BF16)
    else:
        args = [proj, proj, proj, proj, mix, mixm, w0, wup_hi, wup_lo, a0, aup, gup,
                k_k, k_a, r_k, ln_w, ln_b, bd]
        in_specs = act_specs + [const(x) for x in args[4:]]
        out_specs = (wide_spec, wide_spec)
        out_shape = (jax.ShapeDtypeStruct((bsz * t, RWKV_WIDTH), BF16),
                     jax.ShapeDtypeStruct((bsz * t, RWKV_WIDTH), F32))
    return pl.pallas_call(
        functools.partial(_rwkv_body, has_vres),
        grid=(bsz, nc),
        in_specs=in_specs,
        out_specs=out_specs,
        out_shape=out_shape,
        scratch_shapes=[
            pltpu.VMEM((HALO + CHUNK, RWKV_WIDTH), F32),
            pltpu.VMEM((HALO + CHUNK, RWKV_WIDTH), F32),
            pltpu.VMEM((HALO + CHUNK, RWKV_WIDTH), F32),
            pltpu.VMEM((HALO + CHUNK, SMALLBLK), F32),
            pltpu.VMEM((ST_N, CHUNK, RWKV_WIDTH), F32),
            pltpu.VMEM((PAIRS, LANES, LANES), F32),
            pltpu.VMEM((CHUNK, RWKV_WIDTH), F32),
        ],
        compiler_params=_params(("parallel", "arbitrary")),
        name="rwkv_mixer",
    )(*args)


def _outproj_ln_body(ys_ref, yr_ref, wt_ref, wb_ref, x_ref, g_ref, b_ref, of_ref, ob_ref):
    mixed = _dot(ys_ref[...], wt_ref[...]) + _dot(yr_ref[...], wb_ref[...])
    out = _layer_norm(DEEPNORM_ALPHA * x_ref[...] + mixed, g_ref[...], b_ref[...])
    of_ref[...] = out
    ob_ref[...] = out.astype(BF16)


def _outproj_ln(ys, yr, w_out, x, g, b, bm):
    n = x.shape[0]
    half = w_out.shape[0] // 2
    row = lambda width: pl.BlockSpec((bm, width), lambda i: (i, 0))
    vec = pl.BlockSpec((1, D_MODEL), lambda i: (0, 0))
    return pl.pallas_call(
        _outproj_ln_body,
        grid=(n // bm,),
        in_specs=[row(half), row(half),
                  pl.BlockSpec((half, D_MODEL), lambda i: (0, 0)),
                  pl.BlockSpec((half, D_MODEL), lambda i: (1, 0)),
                  row(D_MODEL), vec, vec],
        out_specs=(row(D_MODEL), row(D_MODEL)),
        out_shape=(jax.ShapeDtypeStruct((n, D_MODEL), F32),
                   jax.ShapeDtypeStruct((n, D_MODEL), BF16)),
        compiler_params=_params(("parallel",)),
        name="out_proj_ln",
    )(ys, yr, w_out, w_out, x, g, b)


def _down_ln_body(h_ref, w_ref, x_ref, g_ref, b_ref, of_ref, ob_ref, acc_ref):
    kk = pl.program_id(1)

    @pl.when(kk == 0)
    def _zero():
        acc_ref[...] = jnp.zeros_like(acc_ref)

    acc_ref[...] += _dot(h_ref[...], w_ref[...])

    @pl.when(kk == pl.num_programs(1) - 1)
    def _finish():
        out = _layer_norm(DEEPNORM_ALPHA * x_ref[...] + acc_ref[...], g_ref[...], b_ref[...])
        of_ref[...] = out
        ob_ref[...] = out.astype(BF16)


def _down_ln(h, w, x, g, b, bm, bk):
    n, kdim = h.shape
    row = pl.BlockSpec((bm, D_MODEL), lambda i, k: (i, 0))
    vec = pl.BlockSpec((1, D_MODEL), lambda i, k: (0, 0))
    return pl.pallas_call(
        _down_ln_body,
        grid=(n // bm, kdim // bk),
        in_specs=[pl.BlockSpec((bm, bk), lambda i, k: (i, k)),
                  pl.BlockSpec((bk, D_MODEL), lambda i, k: (k, 0)),
                  row, vec, vec],
        out_specs=(row, row),
        out_shape=(jax.ShapeDtypeStruct((n, D_MODEL), F32),
                   jax.ShapeDtypeStruct((n, D_MODEL), BF16)),
        scratch_shapes=[pltpu.VMEM((bm, D_MODEL), F32)],
        compiler_params=_params(("parallel", "arbitrary")),
        name="down_proj_ln",
    )(h, w, x, g, b)


def _swiglu_up_body(x_ref, w1_ref, w3_ref, o_ref):
    xb = x_ref[...]
    h1 = _dot(xb, w1_ref[...])
    h3 = _dot(xb, w3_ref[...])
    o_ref[...] = (h1 * jax.nn.sigmoid(h1) * h3).astype(o_ref.dtype)


def _swiglu_up(xb, w1, w3, bm, bn):
    n = xb.shape[0]
    f = w1.shape[1]
    return pl.pallas_call(
        _swiglu_up_body,
        grid=(n // bm, f // bn),
        in_specs=[pl.BlockSpec((bm, D_MODEL), lambda i, j: (i, 0)),
                  pl.BlockSpec((D_MODEL, bn), lambda i, j: (0, j)),
                  pl.BlockSpec((D_MODEL, bn), lambda i, j: (0, j))],
        out_specs=pl.BlockSpec((bm, bn), lambda i, j: (i, j)),
        out_shape=jax.ShapeDtypeStruct((n, f), BF16),
        compiler_params=_params(("parallel", "parallel")),
        name="swiglu_up",
    )(xb, w1, w3)


def _router_body(x_ref, wr_ref, o_ref):
    logits = _dot(x_ref[...], wr_ref[...], HIGHEST)
    lane = lax.broadcasted_iota(jnp.int32, logits.shape, 1)
    neg = jnp.float32(-jnp.inf)
    logits = jnp.where(lane < N_EXPERTS, logits, neg)
    top1 = jnp.max(logits, axis=-1, keepdims=True)
    idx1 = jnp.min(jnp.where(logits == top1, lane, LANES), axis=-1, keepdims=True)
    rest = jnp.where(lane == idx1, neg, logits)
    top2 = jnp.max(rest, axis=-1, keepdims=True)
    idx2 = jnp.min(jnp.where(rest == top2, lane, LANES), axis=-1, keepdims=True)
    e2 = jnp.exp(top2 - top1)
    gate1 = 1.0 / (1.0 + e2)
    gate2 = e2 / (1.0 + e2)
    o_ref[...] = jnp.where(lane == idx1, gate1, 0.0) + jnp.where(lane == idx2, gate2, 0.0)


def _router(x, wr, bm):
    n = x.shape[0]
    return pl.pallas_call(
        _router_body,
        grid=(n // bm,),
        in_specs=[pl.BlockSpec((bm, D_MODEL), lambda i: (i, 0)),
                  pl.BlockSpec((D_MODEL, LANES), lambda i: (0, 0))],
        out_specs=pl.BlockSpec((bm, LANES), lambda i: (i, 0)),
        out_shape=jax.ShapeDtypeStruct((n, LANES), F32),
        compiler_params=_params(("parallel",)),
        name="moe_router",
    )(x, wr)


def _moe_up_body(x_ref, comb_ref, w1_ref, w3_ref, o_ref):
    e = pl.program_id(1)
    xb = x_ref[...]
    h1 = _dot(xb, w1_ref[...])
    h3 = _dot(xb, w3_ref[...])
    comb = comb_ref[...]
    lane = lax.broadcasted_iota(jnp.int32, comb.shape, 1)
    gate = jnp.sum(jnp.where(lane == e, comb, 0.0), axis=-1, keepdims=True)
    o_ref[...] = (h1 * jax.nn.sigmoid(h1) * h3 * gate).astype(o_ref.dtype)


def _moe_up(xb, comb, w1, w3, bm, bn):
    n = xb.shape[0]
    ne, _, fe = w1.shape
    nj = fe // bn
    wspec = pl.BlockSpec((None, D_MODEL, bn), lambda i, e, j: (e, 0, j))
    return pl.pallas_call(
        _moe_up_body,
        grid=(n // bm, ne, nj),
        in_specs=[pl.BlockSpec((bm, D_MODEL), lambda i, e, j: (i, 0)),
                  pl.BlockSpec((bm, LANES), lambda i, e, j: (i, 0)),
                  wspec, wspec],
        out_specs=pl.BlockSpec((bm, bn), lambda i, e, j: (i, e * nj + j)),
        out_shape=jax.ShapeDtypeStruct((n, ne * fe), BF16),
        compiler_params=_params(("parallel", "parallel", "parallel")),
        name="moe_up",
    )(xb, comb, w1, w3)


def _pad_lanes(v, width):
    v = v.reshape(1, -1)
    return jnp.pad(v, ((0, 0), (0, width - v.shape[1])))


def _pick(n, pref):
    return pref if n % pref == 0 else n


def kernel(x, w_in, w_in_vres, ssd_conv_w, ssd_conv_b, ssd_dt_bias, ssd_a_log, ssd_d, ssd_norm_w, rw_mix, rw_vres_mix, rw_w0, rw_w_up, rw_a0, rw_a_up, rw_v0, rw_v_up, rw_g_up, rw_k_k, rw_k_a, rw_r_k, rw_ln_w, rw_ln_b, w_out, ln1_g, ln1_b, ln2_g, ln2_b, ffn_w1, ffn_w3, ffn_w2, moe_router, moe_w1, moe_w3, moe_w2):
    bsz, t, d = x.shape
    n = bsz * t
    depth = w_in.shape[0]
    assert d == D_MODEL and t % CHUNK == 0
    xf = x.reshape(n, d)
    xb = xf.astype(BF16)

    head_of_lane = jnp.arange(RWKV_WIDTH) // HEAD_DIM
    expand = (jnp.arange(LANES)[:, None] == head_of_lane[None, :]).astype(F32)
    blk_head = jnp.arange(MXU_WIDTH) // HEAD_DIM
    head_ones = (blk_head[:, None] == blk_head[None, :]).astype(BF16)

    bm_big = _pick(n, 1024)
    bm_mid = _pick(n, 512)
    bm_small = _pick(n, 256)
    row = lambda v: v.reshape(1, -1)
    zcols = lambda k: jnp.zeros((d, k), F32)

    vfirst = None
    for l in range(depth):
        w = w_in[l]
        o = 0
        segs = {}
        for name, width in (("z", SSD_WIDTH), ("xs", SSD_WIDTH), ("bc", 2 * SSD_GROUPS * SSD_STATE),
                            ("dt", SSD_HEADS), ("r", RWKV_WIDTH), ("k", RWKV_WIDTH), ("v", RWKV_WIDTH),
                            ("w_lo", W_LORA), ("a_lo", A_LORA), ("g_lo", G_LORA)):
            segs[name] = w[:, o:o + width]
            o += width
        v_lo_cols = w_in_vres[l - 1] if l > 0 else zcols(V_LORA)
        misc_cols = jnp.concatenate(
            [segs["dt"], zcols(MISC_WA - SSD_HEADS), segs["w_lo"], segs["a_lo"], segs["g_lo"], v_lo_cols,
             zcols(SMALLBLK - MISC_GV - G_LORA - V_LORA)], axis=1)
        w_comb = jnp.concatenate([segs["z"], segs["xs"], segs["r"], segs["k"], segs["v"], segs["bc"],
                                  misc_cols], axis=1).astype(BF16)

        proj = _matmul(xb, w_comb, bm_big, COLBLK, F32)

        cw, cb = ssd_conv_w[l], ssd_conv_b[l]
        y_ssd = _ssd_mixer(
            proj, bsz, t, cw[:, :SSD_WIDTH], row(cb[:SSD_WIDTH]), cw[:, SSD_WIDTH:], row(cb[SSD_WIDTH:]),
            _pad_lanes(ssd_dt_bias[l], LANES), _pad_lanes(ssd_a_log[l], LANES),
            row(jnp.repeat(ssd_d[l], HEAD_DIM)), row(ssd_norm_w[l]), expand)

        mix = rw_mix[l]
        v_mix = rw_vres_mix[l - 1] if l > 0 else jnp.zeros((V_LORA,), F32)
        mix_misc = jnp.concatenate([jnp.zeros((MISC_WA,), F32), mix[3 * RWKV_WIDTH:], v_mix,
                                    jnp.zeros((SMALLBLK - MISC_GV - G_LORA - V_LORA,), F32)])
        wup = jnp.pad(rw_w_up[l], ((0, LANES - W_LORA), (0, 0)))
        wup_hi = wup.astype(BF16)
        wup_lo = (wup - wup_hi.astype(F32)).astype(BF16)
        aup = jnp.pad(rw_a_up[l], ((W_LORA, 0), (0, 0))).astype(BF16)
        gup = jnp.pad(rw_g_up[l], ((0, 2 * LANES - G_LORA), (0, 0))).astype(BF16)
        if l > 0:
            v0 = row(rw_v0[l - 1])
            vup = jnp.pad(rw_v_up[l - 1], ((G_LORA, 2 * LANES - G_LORA - V_LORA), (0, 0))).astype(BF16)
        else:
            v0 = vup = None
        res = _rwkv_mixer(
            proj, bsz, t, vfirst, row(mix[:3 * RWKV_WIDTH]), row(mix_misc), row(rw_w0[l]), wup_hi, wup_lo,
            row(rw_a0[l]), aup, gup, v0, vup, row(rw_k_k[l]), row(rw_k_a[l]), row(rw_r_k[l]),
            row(rw_ln_w[l]), row(rw_ln_b[l]), head_ones)
        if l == 0:
            y_rw, vfirst = res
        else:
            y_rw = res

        xf, xb = _outproj_ln(y_ssd, y_rw, w_out[l].astype(BF16), xf, row(ln1_g[l]), row(ln1_b[l]), bm_small)

        if l % 2 == 0:
            i = l // 2
            h = _swiglu_up(xb, ffn_w1[i].astype(BF16), ffn_w3[i].astype(BF16), bm_big, 512)
            w2 = ffn_w2[i].astype(BF16)
        else:
            i = l // 2
            comb = _router(xf, jnp.pad(moe_router[i], ((0, 0), (0, LANES - N_EXPERTS))), bm_mid)
            h = _moe_up(xb, comb, moe_w1[i].astype(BF16), moe_w3[i].astype(BF16), bm_mid, 1408)
            w2 = moe_w2[i].reshape(-1, d).astype(BF16)
        xf, xb = _down_ln(h, w2, xf, row(ln2_g[l]), row(ln2_b[l]), bm_mid, 1408)

    return xf.reshape(bsz, t, d)
```

```python
import functools

import jax
import jax.numpy as jnp
from jax import lax
from jax.experimental import pallas as pl
from jax.experimental.pallas import tpu as pltpu

F32 = jnp.float32
BF16 = jnp.bfloat16
HIGHEST = lax.Precision.HIGHEST

D_MODEL = 2048
HEAD_DIM = 64
SSD_WIDTH = 1024
RWKV_WIDTH = 1024
SSD_HEADS = SSD_WIDTH // HEAD_DIM
SSD_GROUPS = 2
SSD_STATE = 128
SSD_CONV_WIDTH = 4
RWKV_HEADS = RWKV_WIDTH // HEAD_DIM
W_LORA = 64
A_LORA = 64
V_LORA = 32
G_LORA = 160
N_EXPERTS = 8
DEPTH = 4
DEEPNORM_ALPHA = (2 * DEPTH) ** 0.25
LN_EPS = 1e-5
RMS_EPS = 1e-5
RWKV_GN_EPS = 64e-5
L2_EPS = 1e-12

LANES = 128
SUBLANES = 8
VMEM_LIMIT = 56 * 1024 * 1024

CHUNK = 128
HALO = SUBLANES

COLBLK = 1024
BLK_Z, BLK_XS, BLK_R, BLK_K, BLK_V = 0, 1, 2, 3, 4
SMALLBLK = 512
BLK_BC = 10
BLK_MISC = 11
PROJ_WIDTH = 6 * COLBLK
MISC_WA = 128
MISC_GV = 256


def _params(sem, vmem=VMEM_LIMIT):
    return pltpu.CompilerParams(dimension_semantics=sem, vmem_limit_bytes=vmem)


def _dot(a, b, precision=None):
    return jnp.dot(a, b, preferred_element_type=F32, precision=precision)


def _dot_nt(a, b):
    return lax.dot_general(a, b, (((1,), (1,)), ((), ())), preferred_element_type=F32)


def _bf(x):
    return x.astype(BF16)


def _softplus(x):
    return jnp.maximum(x, 0.0) + jnp.log1p(jnp.exp(-jnp.abs(x)))


def _layer_norm(y, g, b):
    mu = jnp.mean(y, axis=-1, keepdims=True)
    d = y - mu
    var = jnp.mean(d * d, axis=-1, keepdims=True)
    return d * lax.rsqrt(var + LN_EPS) * g + b


def _mm_body(x_ref, w_ref, o_ref):
    o_ref[...] = _dot(x_ref[...], w_ref[...]).astype(o_ref.dtype)


def _matmul(x, w, bm, bn, out_dtype):
    m, k = x.shape
    n = w.shape[1]
    return pl.pallas_call(
        _mm_body,
        grid=(m // bm, n // bn),
        in_specs=[pl.BlockSpec((bm, k), lambda i, j: (i, 0)),
                  pl.BlockSpec((k, bn), lambda i, j: (0, j))],
        out_specs=pl.BlockSpec((bm, bn), lambda i, j: (i, j)),
        out_shape=jax.ShapeDtypeStruct((m, n), out_dtype),
        compiler_params=_params(("parallel", "parallel")),
        name="in_proj",
    )(x, w)


def _ssd_body(z_ref, xs_ref, bc_ref, misc_ref, cwx_ref, cbx_ref, cwb_ref, cbb_ref, dtb_ref,
              alog_ref, dskip_ref, nw_ref, e_ref, o_ref, xs_buf, bc_buf, h_ref, y_ref):
    L = CHUNK
    c = pl.program_id(1)

    @pl.when(c == 0)
    def _init():
        xs_buf[0:HALO, :] = jnp.zeros((HALO, xs_buf.shape[1]), F32)
        bc_buf[0:HALO, :] = jnp.zeros((HALO, bc_buf.shape[1]), F32)
        h_ref[...] = jnp.zeros_like(h_ref)

    xs_buf[HALO:HALO + L, :] = xs_ref[...]
    bc_buf[HALO:HALO + L, :] = bc_ref[...]

    def conv_silu(buf, w_ref, b_ref):
        acc = b_ref[...]
        for i in range(SSD_CONV_WIDTH):
            acc = acc + buf[pl.ds(HALO - (SSD_CONV_WIDTH - 1) + i, L), :] * w_ref[i:i + 1, :]
        return acc * jax.nn.sigmoid(acc)

    xs = conv_silu(xs_buf, cwx_ref, cbx_ref)
    bcv = conv_silu(bc_buf, cwb_ref, cbb_ref)
    xs_buf[0:HALO, :] = xs_buf[L:L + HALO, :]
    bc_buf[0:HALO, :] = bc_buf[L:L + HALO, :]

    lane = lax.broadcasted_iota(jnp.int32, (1, LANES), 1)
    dt = _softplus(misc_ref[:, 0:LANES] + dtb_ref[...])
    a_neg = jnp.where(lane < SSD_HEADS, -jnp.exp(alog_ref[...]), 0.0)
    d_a = dt * a_neg
    row = lax.broadcasted_iota(jnp.int32, (L, L), 0)
    col = lax.broadcasted_iota(jnp.int32, (L, L), 1)
    causal = row >= col
    a_cum = _dot(causal.astype(F32), d_a, HIGHEST)
    a_cum_t = a_cum.T
    a_last = a_cum[L - 1:L, :]
    per_head = jnp.concatenate([dt, jnp.exp(a_last - a_cum), jnp.exp(a_cum)], axis=0)
    full = _dot(per_head, e_ref[...], HIGHEST)
    dt_f, dte_f, ea_f = full[0:L], full[L:2 * L], full[2 * L:3 * L]
    xdt = xs * dt_f
    xdt_b = _bf(xdt)
    xdte_b = _bf(xdt * dte_f)
    lane_l = lax.broadcasted_iota(jnp.int32, (L, LANES), 1)
    lower_half = lane_l < HEAD_DIM
    gw = SSD_WIDTH // SSD_GROUPS
    heads_per_group = SSD_HEADS // SSD_GROUPS
    for g in range(SSD_GROUPS):
        b_g = bcv[:, g * SSD_STATE:(g + 1) * SSD_STATE]
        c_g = _bf(bcv[:, (SSD_GROUPS + g) * SSD_STATE:(SSD_GROUPS + g + 1) * SSD_STATE])
        cb = _dot_nt(c_g, _bf(b_g))
        h_g = h_ref[g]
        y_off = _dot(c_g, _bf(h_g)) * ea_f[:, g * gw:(g + 1) * gw]
        new_state = _dot(_bf(b_g.T), xdte_b[:, g * gw:(g + 1) * gw])
        for pr in range(heads_per_group // 2):
            ms = []
            for e in range(2):
                h = g * heads_per_group + pr * 2 + e
                seg = a_cum[:, h:h + 1] - a_cum_t[h:h + 1, :]
                dec = jnp.where(causal, jnp.exp(jnp.minimum(seg, 0.0)), 0.0)
                ms.append(_bf(cb * dec))
            lhs = jnp.concatenate(ms, axis=1)
            lo = g * gw + pr * LANES
            xp = xdt_b[:, lo:lo + LANES]
            zero = jnp.zeros_like(xp)
            rhs = jnp.concatenate([jnp.where(lower_half, xp, zero),
                                   jnp.where(lower_half, zero, xp)], axis=0)
            y_ref[:, lo:lo + LANES] = _dot(lhs, rhs) + y_off[:, pr * LANES:(pr + 1) * LANES]
        h_ref[g] = h_g * ea_f[L - 1:L, g * gw:(g + 1) * gw] + new_state

    y = y_ref[...] + xs * dskip_ref[...]
    zz = z_ref[...]
    u = y * (zz * jax.nn.sigmoid(zz))
    for g in range(SSD_GROUPS):
        ug = u[:, g * gw:(g + 1) * gw]
        ms_ = jnp.mean(ug * ug, axis=-1, keepdims=True)
        o_ref[:, g * gw:(g + 1) * gw] = (
            ug * lax.rsqrt(ms_ + RMS_EPS) * nw_ref[:, g * gw:(g + 1) * gw]).astype(o_ref.dtype)


def _ssd_mixer(proj, bsz, t, cwx, cbx, cwb, cbb, dtb, alog, dskip, nw, expand):
    nc = t // CHUNK
    rows = lambda b, c: b * nc + c
    const = lambda shape: pl.BlockSpec(shape, lambda b, c: (0,) * len(shape))
    return pl.pallas_call(
        _ssd_body,
        grid=(bsz, nc),
        in_specs=[
            pl.BlockSpec((CHUNK, COLBLK), lambda b, c: (rows(b, c), BLK_Z)),
            pl.BlockSpec((CHUNK, COLBLK), lambda b, c: (rows(b, c), BLK_XS)),
            pl.BlockSpec((CHUNK, SMALLBLK), lambda b, c: (rows(b, c), BLK_BC)),
            pl.BlockSpec((CHUNK, SMALLBLK), lambda b, c: (rows(b, c), BLK_MISC)),
            const(cwx.shape), const(cbx.shape), const(cwb.shape), const(cbb.shape),
            const(dtb.shape), const(alog.shape), const(dskip.shape), const(nw.shape),
            const(expand.shape),
        ],
        out_specs=pl.BlockSpec((CHUNK, SSD_WIDTH), lambda b, c: (rows(b, c), 0)),
        out_shape=jax.ShapeDtypeStruct((bsz * t, SSD_WIDTH), BF16),
        scratch_shapes=[
            pltpu.VMEM((HALO + CHUNK, SSD_WIDTH), F32),
            pltpu.VMEM((HALO + CHUNK, SMALLBLK), F32),
            pltpu.VMEM((SSD_GROUPS, SSD_STATE, SSD_WIDTH // SSD_GROUPS), F32),
            pltpu.VMEM((CHUNK, SSD_WIDTH), F32),
        ],
        compiler_params=_params(("parallel", "arbitrary")),
        name="ssd_mixer",
    )(proj, proj, proj, proj, cwx, cbx, cwb, cbb, dtb, alog, dskip, nw, expand)


PAIRS = RWKV_HEADS // 2
MXU_WIDTH = 256
ST_AMID, ST_RMID, ST_BMID, ST_KMID, ST_AST, ST_RST, ST_BEND, ST_KEND, ST_V, ST_N = range(10)


def _split2(x):
    hi = x.astype(BF16)
    return hi, (x - hi.astype(F32)).astype(BF16)


def _split3(x):
    hi = x.astype(BF16)
    r1 = x - hi.astype(F32)
    mid = r1.astype(BF16)
    return hi, mid, (r1 - mid.astype(F32)).astype(BF16)


def _rwkv_body(has_vres, *refs):
    if has_vres:
        (r_ref, k_ref, v_ref, misc_ref, vfirst_ref, mix_ref, mixm_ref, w0_ref, wuph_ref, wupl_ref,
         a0_ref, aup_ref, gup_ref, v0_ref, vup_ref, kk_ref, ka_ref, rk_ref, lnw_ref, lnb_ref, bd_ref,
         o_ref, r_buf, k_buf, v_buf, m_buf, st_ref, h_ref, y_ref) = refs
    else:
        (r_ref, k_ref, v_ref, misc_ref, mix_ref, mixm_ref, w0_ref, wuph_ref, wupl_ref,
         a0_ref, aup_ref, gup_ref, kk_ref, ka_ref, rk_ref, lnw_ref, lnb_ref, bd_ref,
         o_ref, vfirst_out_ref, r_buf, k_buf, v_buf, m_buf, st_ref, h_ref, y_ref) = refs
    L = CHUNK
    c = pl.program_id(1)

    @pl.when(c == 0)
    def _init():
        for buf in (r_buf, k_buf, v_buf, m_buf):
            buf[0:HALO, :] = jnp.zeros((HALO, buf.shape[1]), F32)
        h_ref[...] = jnp.zeros_like(h_ref)

    def shift_lerp(ref, buf, mix):
        cur = ref[...]
        buf[HALO:HALO + L, :] = cur
        prev = buf[pl.ds(HALO - 1, L), :]
        buf[0:HALO, :] = buf[L:L + HALO, :]
        return cur + (prev - cur) * mix

    w_ = RWKV_WIDTH
    r = shift_lerp(r_ref, r_buf, mix_ref[:, 0:w_])
    k = shift_lerp(k_ref, k_buf, mix_ref[:, w_:2 * w_])
    v = shift_lerp(v_ref, v_buf, mix_ref[:, 2 * w_:3 * w_])
    m = shift_lerp(misc_ref, m_buf, mixm_ref[...])
    wa = m[:, MISC_WA:MISC_WA + LANES]
    gv = m[:, MISC_GV:MISC_GV + 2 * LANES]

    th_hi, th_lo = _split2(jnp.tanh(wa))
    wup_hi, wup_lo = wuph_ref[...], wupl_ref[...]
    w_lin = w0_ref[...] + (_dot(th_hi, wup_hi) + _dot(th_hi, wup_lo) + _dot(th_lo, wup_hi))
    log_w = -jnp.exp(-_softplus(-w_lin) - 0.5)
    a = jax.nn.sigmoid(a0_ref[...] + _dot(_bf(wa), aup_ref[...]))
    g = _dot(_bf(jax.nn.sigmoid(gv)), gup_ref[...])
    if has_vres:
        v = v + (vfirst_ref[...] - v) * jax.nn.sigmoid(v0_ref[...] + _dot(_bf(gv), vup_ref[...]))
    else:
        vfirst_out_ref[...] = v

    n_blk = RWKV_WIDTH // MXU_WIDTH

    def head_sum(x):
        parts = [t[:, j * MXU_WIDTH:(j + 1) * MXU_WIDTH] for t in _split2(x) for j in range(n_blk)]
        s = _dot(jnp.concatenate(parts, axis=0), bd_ref[...])
        return jnp.concatenate(
            [s[j * L:(j + 1) * L] + s[(n_blk + j) * L:(n_blk + j + 1) * L] for j in range(n_blk)], axis=1)

    kx = k * kk_ref[...]
    kk = kx / jnp.maximum(jnp.sqrt(head_sum(kx * kx)), L2_EPS)
    k2 = k * (1.0 + (a - 1.0) * ka_ref[...])
    av = -kk
    bv = kk * a

    row = lax.broadcasted_iota(jnp.int32, (L, L), 0)
    col = lax.broadcasted_iota(jnp.int32, (L, L), 1)
    incl = row >= col
    strict = row > col
    tri = incl.astype(BF16)
    lw_hi, lw_mid, lw_lo = _split3(log_w)
    cum = _dot(tri, lw_hi) + _dot(tri, lw_mid) + _dot(tri, lw_lo)
    cum_prev = cum - log_w
    tot = cum[L - 1:L, :]
    mid = cum[L // 2 - 1:L // 2, :]
    st_ref[ST_AMID] = av * jnp.exp(cum_prev - mid)
    st_ref[ST_RMID] = r * jnp.exp(cum - mid)
    inv_mid = jnp.exp(mid - cum)
    st_ref[ST_BMID] = bv * inv_mid
    st_ref[ST_KMID] = k2 * inv_mid
    st_ref[ST_AST] = av * jnp.exp(cum_prev)
    st_ref[ST_RST] = r * jnp.exp(cum)
    to_end = jnp.exp(tot - cum)
    st_ref[ST_BEND] = bv * to_end
    st_ref[ST_KEND] = k2 * to_end
    st_ref[ST_V] = v
    e_tot_all = jnp.exp(tot)

    lane_l = lax.broadcasted_iota(jnp.int32, (L, LANES), 1)
    lower = lane_l < HEAD_DIM
    r2 = lax.broadcasted_iota(jnp.int32, (LANES, LANES), 0)
    c2 = lax.broadcasted_iota(jnp.int32, (LANES, LANES), 1)
    same_head = (r2 < HEAD_DIM) == (c2 < HEAD_DIM)
    diag = r2 == c2
    n_rounds = L.bit_length() - 1

    lanes_of = [slice(p * LANES, (p + 1) * LANES) for p in range(PAIRS)]
    masks = (lower, jnp.logical_not(lower))
    heads = [(p, h) for p in range(PAIRS) for h in range(2)]
    v_b, v_sw, bk_t = [], [], []
    for sl in lanes_of:
        v_p = st_ref[ST_V, :, sl]
        v_b.append(_bf(v_p))
        v_sw.append(_bf(pltpu.roll(v_p, HEAD_DIM, 1)))
        bk_t.append(_bf(jnp.concatenate([st_ref[ST_BMID, :, sl].T, st_ref[ST_KMID, :, sl].T], axis=1)))
    zero_b = jnp.zeros_like(v_sw[0])

    mm, a_ak, a_rbk = [], [], []
    for p, h in heads:
        sl, mh = lanes_of[p], masks[h]
        ar = _bf(jnp.concatenate([jnp.where(mh, st_ref[ST_AMID, :, sl], 0.0),
                                  jnp.where(mh, st_ref[ST_RMID, :, sl], 0.0)], axis=0))
        aa = _dot(ar, bk_t[p])
        mm.append(_bf(jnp.where(strict, aa[0:L, 0:L], 0.0)))
        a_ak.append(_bf(jnp.where(strict, aa[0:L, L:2 * L], 0.0)))
        a_rbk.append(_bf(jnp.concatenate([jnp.where(incl, aa[L:2 * L, 0:L], 0.0),
                                          jnp.where(incl, aa[L:2 * L, L:2 * L], 0.0)], axis=1)))
    xs_ = [jnp.where(masks[h], st_ref[ST_AST, :, lanes_of[p]], _dot(a_ak[i], v_sw[p]))
           for i, (p, h) in enumerate(heads)]
    for rnd in range(n_rounds):
        xs_ = [x + _dot(m_, _bf(x)) for x, m_ in zip(xs_, mm)]
        if rnd + 1 < n_rounds:
            mm = [_bf(_dot(m_, m_)) for m_ in mm]
    yz = [_dot(a_rbk[i], jnp.concatenate([_bf(xs_[i]), jnp.where(masks[h], zero_b, v_sw[p])], axis=0))
          for i, (p, h) in enumerate(heads)]

    for p, sl in enumerate(lanes_of):
        x0, x1, yz0, yz1 = xs_[2 * p], xs_[2 * p + 1], yz[2 * p], yz[2 * p + 1]
        a_bar = jnp.where(lower, x0, x1)
        uv = pltpu.roll(jnp.where(lower, x1, x0), HEAD_DIM, 1)
        r_bar = st_ref[ST_RST, :, sl] + jnp.where(lower, yz0, yz1)
        yv = pltpu.roll(jnp.where(lower, yz1, yz0), HEAD_DIM, 1)
        b_end_t = _bf(st_ref[ST_BEND, :, sl].T)
        k_end_t = _bf(st_ref[ST_KEND, :, sl].T)
        g_mat = (jnp.where(same_head, _dot(b_end_t, _bf(a_bar)), 0.0)
                 + jnp.where(diag, e_tot_all[:, sl], 0.0))
        h_add = jnp.where(same_head, _dot(b_end_t, _bf(uv)) + _dot(k_end_t, v_b[p]), 0.0)
        h_old = _bf(h_ref[p])
        y_ref[:, sl] = _dot(_bf(r_bar), h_old) + yv
        h_ref[p] = _dot(_bf(g_mat), h_old) + h_add

    y = y_ref[...]
    inv_n = 1.0 / HEAD_DIM
    mu = head_sum(y) * inv_n
    d = y - mu
    var = head_sum(d * d) * inv_n
    y = d * lax.rsqrt(var + RWKV_GN_EPS) * lnw_ref[...] + lnb_ref[...]
    y = y + head_sum(r * k2 * rk_ref[...]) * v
    o_ref[...] = (y * g).astype(o_ref.dtype)


def _rwkv_mixer(proj, bsz, t, vfirst, mix, mixm, w0, wup_hi, wup_lo, a0, aup, gup, v0, vup, k_k, k_a,
                r_k, ln_w, ln_b, bd):
    nc = t // CHUNK
    rows = lambda b, c: b * nc + c
    const = lambda x: pl.BlockSpec(x.shape, lambda b, c: (0,) * x.ndim)
    has_vres = vfirst is not None
    act_specs = [
        pl.BlockSpec((CHUNK, COLBLK), lambda b, c: (rows(b, c), BLK_R)),
        pl.BlockSpec((CHUNK, COLBLK), lambda b, c: (rows(b, c), BLK_K)),
        pl.BlockSpec((CHUNK, COLBLK), lambda b, c: (rows(b, c), BLK_V)),
        pl.BlockSpec((CHUNK, SMALLBLK), lambda b, c: (rows(b, c), BLK_MISC)),
    ]
    wide_spec = pl.BlockSpec((CHUNK, RWKV_WIDTH), lambda b, c: (rows(b, c), 0))
    if has_vres:
        args = [proj, proj, proj, proj, vfirst, mix, mixm, w0, wup_hi, wup_lo, a0, aup, gup, v0, vup,
                k_k, k_a, r_k, ln_w, ln_b, bd]
        in_specs = act_specs + [wide_spec] + [const(x) for x in args[5:]]
        out_specs = wide_spec
        out_shape = jax.ShapeDtypeStruct((bsz * t, RWKV_WIDTH), BF16)
    else:
        args = [proj, proj, proj, proj, mix, mixm, w0, wup_hi, wup_lo, a0, aup, gup,
                k_k, k_a, r_k, ln_w, ln_b, bd]
        in_specs = act_specs + [const(x) for x in args[4:]]
        out_specs = (wide_spec, wide_spec)
        out_shape = (jax.ShapeDtypeStruct((bsz * t, RWKV_WIDTH), BF16),
                     jax.ShapeDtypeStruct((bsz * t, RWKV_WIDTH), F32))
    return pl.pallas_call(
        functools.partial(_rwkv_body, has_vres),
        grid=(bsz, nc),
        in_specs=in_specs,
        out_specs=out_specs,
        out_shape=out_shape,
        scratch_shapes=[
            pltpu.VMEM((HALO + CHUNK, RWKV_WIDTH), F32),
            pltpu.VMEM((HALO + CHUNK, RWKV_WIDTH), F32),
            pltpu.VMEM((HALO + CHUNK, RWKV_WIDTH), F32),
            pltpu.VMEM((HALO + CHUNK, SMALLBLK), F32),
            pltpu.VMEM((ST_N, CHUNK, RWKV_WIDTH), F32),
            pltpu.VMEM((PAIRS, LANES, LANES), F32),
            pltpu.VMEM((CHUNK, RWKV_WIDTH), F32),
        ],
        compiler_params=_params(("parallel", "arbitrary")),
        name="rwkv_mixer",
    )(*args)


TOK_ROWS = D_MODEL // LANES


def _store_token_major(ref, val):
    bm = val.shape[0]
    for s in range(TOK_ROWS):
        ref[pl.ds(s, bm, stride=TOK_ROWS), :] = val[:, s * LANES:(s + 1) * LANES]


def _load_token_major(ref, first_row, bm):
    return [ref[pl.ds(first_row + s, bm, stride=TOK_ROWS), :] for s in range(TOK_ROWS)]


def _outproj_ln_body(emit_rows, ys_ref, yr_ref, wt_ref, wb_ref, x_ref, g_ref, b_ref, of_ref, ob_ref,
                     *rows_ref):
    mixed = _dot(ys_ref[...], wt_ref[...]) + _dot(yr_ref[...], wb_ref[...])
    out = _layer_norm(DEEPNORM_ALPHA * x_ref[...] + mixed, g_ref[...], b_ref[...])
    of_ref[...] = out
    ob_ref[...] = out.astype(BF16)
    if emit_rows:
        _store_token_major(rows_ref[0], out)


def _outproj_ln(ys, yr, w_out, x, g, b, bm, emit_rows):
    n = x.shape[0]
    half = w_out.shape[0] // 2
    row = lambda width: pl.BlockSpec((bm, width), lambda i: (i, 0))
    vec = pl.BlockSpec((1, D_MODEL), lambda i: (0, 0))
    out_specs = [row(D_MODEL), row(D_MODEL)]
    out_shape = [jax.ShapeDtypeStruct((n, D_MODEL), F32), jax.ShapeDtypeStruct((n, D_MODEL), BF16)]
    if emit_rows:
        out_specs.append(pl.BlockSpec((bm * TOK_ROWS, LANES), lambda i: (i, 0)))
        out_shape.append(jax.ShapeDtypeStruct((n * TOK_ROWS, LANES), F32))
    return pl.pallas_call(
        functools.partial(_outproj_ln_body, emit_rows),
        grid=(n // bm,),
        in_specs=[row(half), row(half),
                  pl.BlockSpec((half, D_MODEL), lambda i: (0, 0)),
                  pl.BlockSpec((half, D_MODEL), lambda i: (1, 0)),
                  row(D_MODEL), vec, vec],
        out_specs=tuple(out_specs),
        out_shape=tuple(out_shape),
        compiler_params=_params(("parallel",)),
        name="out_proj_ln",
    )(ys, yr, w_out, w_out, x, g, b)


def _down_ln_body(h_ref, w_ref, x_ref, g_ref, b_ref, of_ref, ob_ref, acc_ref):
    kk = pl.program_id(1)

    @pl.when(kk == 0)
    def _zero():
        acc_ref[...] = jnp.zeros_like(acc_ref)

    acc_ref[...] += _dot(h_ref[...], w_ref[...])

    @pl.when(kk == pl.num_programs(1) - 1)
    def _finish():
        out = _layer_norm(DEEPNORM_ALPHA * x_ref[...] + acc_ref[...], g_ref[...], b_ref[...])
        of_ref[...] = out
        ob_ref[...] = out.astype(BF16)


def _down_ln(h, w, x, g, b, bm, bk):
    n, kdim = h.shape
    row = pl.BlockSpec((bm, D_MODEL), lambda i, k: (i, 0))
    vec = pl.BlockSpec((1, D_MODEL), lambda i, k: (0, 0))
    return pl.pallas_call(
        _down_ln_body,
        grid=(n // bm, kdim // bk),
        in_specs=[pl.BlockSpec((bm, bk), lambda i, k: (i, k)),
                  pl.BlockSpec((bk, D_MODEL), lambda i, k: (k, 0)),
                  row, vec, vec],
        out_specs=(row, row),
        out_shape=(jax.ShapeDtypeStruct((n, D_MODEL), F32),
                   jax.ShapeDtypeStruct((n, D_MODEL), BF16)),
        scratch_shapes=[pltpu.VMEM((bm, D_MODEL), F32)],
        compiler_params=_params(("parallel", "arbitrary")),
        name="down_proj_ln",
    )(h, w, x, g, b)


def _swiglu_up_body(x_ref, w1_ref, w3_ref, o_ref):
    xb = x_ref[...]
    h1 = _dot(xb, w1_ref[...])
    h3 = _dot(xb, w3_ref[...])
    o_ref[...] = (h1 * jax.nn.sigmoid(h1) * h3).astype(o_ref.dtype)


def _swiglu_up(xb, w1, w3, bm, bn):
    n = xb.shape[0]
    f = w1.shape[1]
    return pl.pallas_call(
        _swiglu_up_body,
        grid=(n // bm, f // bn),
        in_specs=[pl.BlockSpec((bm, D_MODEL), lambda i, j: (i, 0)),
                  pl.BlockSpec((D_MODEL, bn), lambda i, j: (0, j)),
                  pl.BlockSpec((D_MODEL, bn), lambda i, j: (0, j))],
        out_specs=pl.BlockSpec((bm, bn), lambda i, j: (i, j)),
        out_shape=jax.ShapeDtypeStruct((n, f), BF16),
        compiler_params=_params(("parallel", "parallel")),
        name="swiglu_up",
    )(xb, w1, w3)


def _router_body(x_ref, wr_ref, o_ref):
    logits = _dot(x_ref[...], wr_ref[...], HIGHEST)
    lane = lax.broadcasted_iota(jnp.int32, logits.shape, 1)
    neg = jnp.float32(-jnp.inf)
    logits = jnp.where(lane < N_EXPERTS, logits, neg)
    top1 = jnp.max(logits, axis=-1, keepdims=True)
    idx1 = jnp.min(jnp.where(logits == top1, lane, LANES), axis=-1, keepdims=True)
    rest = jnp.where(lane == idx1, neg, logits)
    top2 = jnp.max(rest, axis=-1, keepdims=True)
    idx2 = jnp.min(jnp.where(rest == top2, lane, LANES), axis=-1, keepdims=True)
    e2 = jnp.exp(top2 - top1)
    gate1 = 1.0 / (1.0 + e2)
    gate2 = e2 / (1.0 + e2)
    o_ref[...] = (jnp.where(lane == 0, idx1.astype(F32), 0.0) + jnp.where(lane == 1, idx2.astype(F32), 0.0)
                  + jnp.where(lane == 2, gate1, 0.0) + jnp.where(lane == 3, gate2, 0.0))


def _router(x, wr, bm):
    n = x.shape[0]
    return pl.pallas_call(
        _router_body,
        grid=(n // bm,),
        in_specs=[pl.BlockSpec((bm, D_MODEL), lambda i: (i, 0)),
                  pl.BlockSpec((D_MODEL, LANES), lambda i: (0, 0))],
        out_specs=pl.BlockSpec((bm, LANES), lambda i: (i, 0)),
        out_shape=jax.ShapeDtypeStruct((n, LANES), F32),
        compiler_params=_params(("parallel",)),
        name="moe_router",
    )(x, wr)


MOE_BM = 512


def _row_copy(src_hbm, src_tok, dst_buf, dst_tok, sem):
    return pltpu.make_async_copy(
        src_hbm.at[pl.ds(pl.multiple_of(src_tok * TOK_ROWS, TOK_ROWS), TOK_ROWS), :],
        dst_buf.at[pl.ds(pl.multiple_of(dst_tok * TOK_ROWS, TOK_ROWS), TOK_ROWS), :], sem)


def _moe_gather_body(src_ref, valid_ref, rows_hbm, o_ref, buf, sem):
    bm = o_ref.shape[0]
    i = pl.program_id(0)

    @pl.when(valid_ref[i] != 0)
    def _gather():
        base = i * bm

        def issue(j, carry):
            _row_copy(rows_hbm, src_ref[base + j], buf, j, sem).start()
            return carry

        def wait(j, carry):
            _row_copy(rows_hbm, src_ref[base + j], buf, j, sem).wait()
            return carry

        lax.fori_loop(0, bm, issue, 0)
        lax.fori_loop(0, bm, wait, 0)
        for s, piece in enumerate(_load_token_major(buf, 0, bm)):
            o_ref[:, s * LANES:(s + 1) * LANES] = piece.astype(o_ref.dtype)

    @pl.when(valid_ref[i] == 0)
    def _empty():
        o_ref[...] = jnp.zeros_like(o_ref)


def _moe_gather(src, valid, rows, n_rows):
    return pl.pallas_call(
        _moe_gather_body,
        grid_spec=pltpu.PrefetchScalarGridSpec(
            num_scalar_prefetch=2,
            grid=(n_rows // MOE_BM,),
            in_specs=[pl.BlockSpec(memory_space=pl.ANY)],
            out_specs=pl.BlockSpec((MOE_BM, D_MODEL), lambda i, src, valid: (i, 0)),
            scratch_shapes=[pltpu.VMEM((MOE_BM * TOK_ROWS, LANES), F32), pltpu.SemaphoreType.DMA(())],
        ),
        out_shape=jax.ShapeDtypeStruct((n_rows, D_MODEL), BF16),
        compiler_params=_params(("arbitrary",)),
        name="moe_gather",
    )(src, valid, rows)


def _moe_up_body(te_ref, valid_ref, x_ref, gate_ref, w1_ref, w3_ref, o_ref):
    i = pl.program_id(1)

    @pl.when(valid_ref[i] != 0)
    def _compute():
        xb = x_ref[...]
        h1 = _dot(xb, w1_ref[...])
        h3 = _dot(xb, w3_ref[...])
        o_ref[...] = (h1 * jax.nn.sigmoid(h1) * h3 * gate_ref[...]).astype(o_ref.dtype)

    @pl.when(valid_ref[i] == 0)
    def _empty():
        o_ref[...] = jnp.zeros_like(o_ref)


def _moe_up(te, valid, xs, gate, w1, w3, bn):
    n_rows = xs.shape[0]
    fe = w1.shape[2]
    wspec = pl.BlockSpec((None, D_MODEL, bn), lambda j, i, te, valid: (te[i], 0, j))
    return pl.pallas_call(
        _moe_up_body,
        grid_spec=pltpu.PrefetchScalarGridSpec(
            num_scalar_prefetch=2,
            grid=(fe // bn, n_rows // MOE_BM),
            in_specs=[pl.BlockSpec((MOE_BM, D_MODEL), lambda j, i, te, valid: (i, 0)),
                      pl.BlockSpec((MOE_BM, 1), lambda j, i, te, valid: (i, 0)),
                      wspec, wspec],
            out_specs=pl.BlockSpec((MOE_BM, bn), lambda j, i, te, valid: (i, j)),
        ),
        out_shape=jax.ShapeDtypeStruct((n_rows, fe), BF16),
        compiler_params=_params(("arbitrary", "arbitrary")),
        name="moe_up",
    )(te, valid, xs, gate, w1, w3)


def _moe_down_body(te_ref, valid_ref, h_ref, w_ref, o_ref):
    i = pl.program_id(0)

    @pl.when(valid_ref[i] != 0)
    def _compute():
        _store_token_major(o_ref, _dot(h_ref[...], w_ref[...]))

    @pl.when(valid_ref[i] == 0)
    def _empty():
        o_ref[...] = jnp.zeros_like(o_ref)


def _moe_down(te, valid, h, w2):
    n_rows, fe = h.shape
    return pl.pallas_call(
        _moe_down_body,
        grid_spec=pltpu.PrefetchScalarGridSpec(
            num_scalar_prefetch=2,
            grid=(n_rows // MOE_BM,),
            in_specs=[pl.BlockSpec((MOE_BM, fe), lambda i, te, valid: (i, 0)),
                      pl.BlockSpec((None, fe, D_MODEL), lambda i, te, valid: (te[i], 0, 0))],
            out_specs=pl.BlockSpec((MOE_BM * TOK_ROWS, LANES), lambda i, te, valid: (i, 0)),
        ),
        out_shape=jax.ShapeDtypeStruct((n_rows * TOK_ROWS, LANES), F32),
        compiler_params=_params(("arbitrary",)),
        name="moe_down",
    )(te, valid, h, w2)


def _moe_combine_ln_body(n_tok, dest_ref, ys_hbm, x_ref, g_ref, b_ref, of_ref, ob_ref, buf, sem):
    bm = x_ref.shape[0]
    base = pl.program_id(0) * bm

    def copies(j):
        return [_row_copy(ys_hbm, dest_ref[slot * n_tok + base + j], buf, slot * bm + j, sem)
                for slot in range(2)]

    def issue(j, carry):
        for cp in copies(j):
            cp.start()
        return carry

    def wait(j, carry):
        for cp in copies(j):
            cp.wait()
        return carry

    lax.fori_loop(0, bm, issue, 0)
    lax.fori_loop(0, bm, wait, 0)
    first = _load_token_major(buf, 0, bm)
    second = _load_token_major(buf, bm * TOK_ROWS, bm)
    f = jnp.concatenate([p + q for p, q in zip(first, second)], axis=1)
    out = _layer_norm(DEEPNORM_ALPHA * x_ref[...] + f, g_ref[...], b_ref[...])
    of_ref[...] = out
    ob_ref[...] = out.astype(BF16)


def _moe_combine_ln(dest, ys, x, g, b, bm):
    n = x.shape[0]
    row = pl.BlockSpec((bm, D_MODEL), lambda i, dest: (i, 0))
    vec = pl.BlockSpec((1, D_MODEL), lambda i, dest: (0, 0))
    return pl.pallas_call(
        functools.partial(_moe_combine_ln_body, n),
        grid_spec=pltpu.PrefetchScalarGridSpec(
            num_scalar_prefetch=1,
            grid=(n // bm,),
            in_specs=[pl.BlockSpec(memory_space=pl.ANY), row, vec, vec],
            out_specs=(row, row),
            scratch_shapes=[pltpu.VMEM((2 * bm * TOK_ROWS, LANES), F32), pltpu.SemaphoreType.DMA(())],
        ),
        out_shape=(jax.ShapeDtypeStruct((n, D_MODEL), F32), jax.ShapeDtypeStruct((n, D_MODEL), BF16)),
        compiler_params=_params(("arbitrary",)),
        name="moe_combine_ln",
    )(dest, ys, x, g, b)


def _moe_dispatch_plan(route, n):
    n_rows = 2 * n + N_EXPERTS * MOE_BM
    expert = route[:, 0:2].astype(jnp.int32).T.reshape(-1)
    gates = route[:, 2:4].T.reshape(-1)
    onehot = (expert[:, None] == jnp.arange(N_EXPERTS)[None, :]).astype(jnp.int32)
    csum = jnp.cumsum(onehot, axis=0)
    rank = jnp.sum(csum * onehot, axis=1) - 1
    count = csum[-1]
    padded = (count + MOE_BM - 1) // MOE_BM * MOE_BM
    ends = jnp.cumsum(padded)
    dest = (ends - padded)[expert] + rank
    tok = jnp.tile(jnp.arange(n, dtype=jnp.int32), 2)
    src = jnp.zeros((n_rows,), jnp.int32).at[dest].set(tok)
    gate = jnp.zeros((n_rows,), F32).at[dest].set(gates).reshape(n_rows, 1)
    tile_start = jnp.arange(n_rows // MOE_BM, dtype=jnp.int32) * MOE_BM
    te = jnp.minimum(jnp.sum((tile_start[:, None] >= ends[None, :]).astype(jnp.int32), axis=1),
                     N_EXPERTS - 1)
    valid = (tile_start < ends[-1]).astype(jnp.int32)
    return dest.astype(jnp.int32), src, gate, te.astype(jnp.int32), valid, n_rows


def _pad_lanes(v, width):
    v = v.reshape(1, -1)
    return jnp.pad(v, ((0, 0), (0, width - v.shape[1])))


def _pick(n, pref):
    return pref if n % pref == 0 else n


def kernel(x, w_in, w_in_vres, ssd_conv_w, ssd_conv_b, ssd_dt_bias, ssd_a_log, ssd_d, ssd_norm_w, rw_mix, rw_vres_mix, rw_w0, rw_w_up, rw_a0, rw_a_up, rw_v0, rw_v_up, rw_g_up, rw_k_k, rw_k_a, rw_r_k, rw_ln_w, rw_ln_b, w_out, ln1_g, ln1_b, ln2_g, ln2_b, ffn_w1, ffn_w3, ffn_w2, moe_router, moe_w1, moe_w3, moe_w2):
    bsz, t, d = x.shape
    n = bsz * t
    depth = w_in.shape[0]
    assert d == D_MODEL and t % CHUNK == 0
    xf = x.reshape(n, d)
    xb = xf.astype(BF16)

    head_of_lane = jnp.arange(RWKV_WIDTH) // HEAD_DIM
    expand = (jnp.arange(LANES)[:, None] == head_of_lane[None, :]).astype(F32)
    blk_head = jnp.arange(MXU_WIDTH) // HEAD_DIM
    head_ones = (blk_head[:, None] == blk_head[None, :]).astype(BF16)

    bm_big = _pick(n, 1024)
    bm_mid = _pick(n, 512)
    bm_small = _pick(n, 256)
    row = lambda v: v.reshape(1, -1)
    zcols = lambda k: jnp.zeros((d, k), F32)

    vfirst = None
    for l in range(depth):
        w = w_in[l]
        o = 0
        segs = {}
        for name, width in (("z", SSD_WIDTH), ("xs", SSD_WIDTH), ("bc", 2 * SSD_GROUPS * SSD_STATE),
                            ("dt", SSD_HEADS), ("r", RWKV_WIDTH), ("k", RWKV_WIDTH), ("v", RWKV_WIDTH),
                            ("w_lo", W_LORA), ("a_lo", A_LORA), ("g_lo", G_LORA)):
            segs[name] = w[:, o:o + width]
            o += width
        v_lo_cols = w_in_vres[l - 1] if l > 0 else zcols(V_LORA)
        misc_cols = jnp.concatenate(
            [segs["dt"], zcols(MISC_WA - SSD_HEADS), segs["w_lo"], segs["a_lo"], segs["g_lo"], v_lo_cols,
             zcols(SMALLBLK - MISC_GV - G_LORA - V_LORA)], axis=1)
        w_comb = jnp.concatenate([segs["z"], segs["xs"], segs["r"], segs["k"], segs["v"], segs["bc"],
                                  misc_cols], axis=1).astype(BF16)

        proj = _matmul(xb, w_comb, bm_big, COLBLK, F32)

        cw, cb = ssd_conv_w[l], ssd_conv_b[l]
        y_ssd = _ssd_mixer(
            proj, bsz, t, cw[:, :SSD_WIDTH], row(cb[:SSD_WIDTH]), cw[:, SSD_WIDTH:], row(cb[SSD_WIDTH:]),
            _pad_lanes(ssd_dt_bias[l], LANES), _pad_lanes(ssd_a_log[l], LANES),
            row(jnp.repeat(ssd_d[l], HEAD_DIM)), row(ssd_norm_w[l]), expand)

        mix = rw_mix[l]
        v_mix = rw_vres_mix[l - 1] if l > 0 else jnp.zeros((V_LORA,), F32)
        mix_misc = jnp.concatenate([jnp.zeros((MISC_WA,), F32), mix[3 * RWKV_WIDTH:], v_mix,
                                    jnp.zeros((SMALLBLK - MISC_GV - G_LORA - V_LORA,), F32)])
        wup = jnp.pad(rw_w_up[l], ((0, LANES - W_LORA), (0, 0)))
        wup_hi = wup.astype(BF16)
        wup_lo = (wup - wup_hi.astype(F32)).astype(BF16)
        aup = jnp.pad(rw_a_up[l], ((W_LORA, 0), (0, 0))).astype(BF16)
        gup = jnp.pad(rw_g_up[l], ((0, 2 * LANES - G_LORA), (0, 0))).astype(BF16)
        if l > 0:
            v0 = row(rw_v0[l - 1])
            vup = jnp.pad(rw_v_up[l - 1], ((G_LORA, 2 * LANES - G_LORA - V_LORA), (0, 0))).astype(BF16)
        else:
            v0 = vup = None
        res = _rwkv_mixer(
            proj, bsz, t, vfirst, row(mix[:3 * RWKV_WIDTH]), row(mix_misc), row(rw_w0[l]), wup_hi, wup_lo,
            row(rw_a0[l]), aup, gup, v0, vup, row(rw_k_k[l]), row(rw_k_a[l]), row(rw_r_k[l]),
            row(rw_ln_w[l]), row(rw_ln_b[l]), head_ones)
        if l == 0:
            y_rw, vfirst = res
        else:
            y_rw = res

        is_moe = l % 2 == 1
        res = _outproj_ln(y_ssd, y_rw, w_out[l].astype(BF16), xf, row(ln1_g[l]), row(ln1_b[l]), bm_small,
                          emit_rows=is_moe)
        i = l // 2
        if not is_moe:
            xf, xb = res
            h = _swiglu_up(xb, ffn_w1[i].astype(BF16), ffn_w3[i].astype(BF16), bm_big, 512)
            xf, xb = _down_ln(h, ffn_w2[i].astype(BF16), xf, row(ln2_g[l]), row(ln2_b[l]), bm_mid, 1408)
        else:
            xf, xb, x_rows = res
            route = _router(xf, jnp.pad(moe_router[i], ((0, 0), (0, LANES - N_EXPERTS))), bm_mid)
            dest, src, gate, te, valid, n_rows = _moe_dispatch_plan(route, n)
            xs = _moe_gather(src, valid, x_rows, n_rows)
            h = _moe_up(te, valid, xs, gate, moe_w1[i].astype(BF16), moe_w3[i].astype(BF16), 1408)
            ys = _moe_down(te, valid, h, moe_w2[i].astype(BF16))
            xf, xb = _moe_combine_ln(dest, ys, xf, row(ln2_g[l]), row(ln2_b[l]), bm_small)

    return xf.reshape(bsz, t, d)
```

```python
import functools

import jax
import jax.numpy as jnp
from jax import lax
from jax.experimental import pallas as pl
from jax.experimental.pallas import tpu as pltpu

F32 = jnp.float32
BF16 = jnp.bfloat16
HIGHEST = lax.Precision.HIGHEST

D_MODEL = 2048
HEAD_DIM = 64
SSD_WIDTH = 1024
RWKV_WIDTH = 1024
SSD_HEADS = SSD_WIDTH // HEAD_DIM
SSD_GROUPS = 2
SSD_STATE = 128
SSD_CONV_WIDTH = 4
RWKV_HEADS = RWKV_WIDTH // HEAD_DIM
W_LORA = 64
A_LORA = 64
V_LORA = 32
G_LORA = 160
N_EXPERTS = 8
DEPTH = 4
DEEPNORM_ALPHA = (2 * DEPTH) ** 0.25
LN_EPS = 1e-5
RMS_EPS = 1e-5
RWKV_GN_EPS = 64e-5
L2_EPS = 1e-12

LANES = 128
SUBLANES = 8
VMEM_LIMIT = 56 * 1024 * 1024

CHUNK = 128
HALO = SUBLANES

COLBLK = 1024
BLK_Z, BLK_XS, BLK_R, BLK_K, BLK_V = 0, 1, 2, 3, 4
SMALLBLK = 512
BLK_BC = 10
BLK_MISC = 11
PROJ_WIDTH = 6 * COLBLK
MISC_WA = 128
MISC_GV = 256


def _params(sem, vmem=VMEM_LIMIT):
    return pltpu.CompilerParams(dimension_semantics=sem, vmem_limit_bytes=vmem)


def _dot(a, b, precision=None):
    return jnp.dot(a, b, preferred_element_type=F32, precision=precision)


def _dot_nt(a, b):
    return lax.dot_general(a, b, (((1,), (1,)), ((), ())), preferred_element_type=F32)


def _bf(x):
    return x.astype(BF16)


def _softplus(x):
    return jnp.maximum(x, 0.0) + jnp.log1p(jnp.exp(-jnp.abs(x)))


def _layer_norm(y, g, b):
    mu = jnp.mean(y, axis=-1, keepdims=True)
    d = y - mu
    var = jnp.mean(d * d, axis=-1, keepdims=True)
    return d * lax.rsqrt(var + LN_EPS) * g + b


def _mm_body(x_ref, w_ref, o_ref):
    o_ref[...] = _dot(x_ref[...], w_ref[...]).astype(o_ref.dtype)


def _matmul(x, w, bm, bn, out_dtype):
    m, k = x.shape
    n = w.shape[1]
    return pl.pallas_call(
        _mm_body,
        grid=(m // bm, n // bn),
        in_specs=[pl.BlockSpec((bm, k), lambda i, j: (i, 0)),
                  pl.BlockSpec((k, bn), lambda i, j: (0, j))],
        out_specs=pl.BlockSpec((bm, bn), lambda i, j: (i, j)),
        out_shape=jax.ShapeDtypeStruct((m, n), out_dtype),
        compiler_params=_params(("parallel", "parallel")),
        name="in_proj",
    )(x, w)


def _ssd_body(z_ref, xs_ref, bc_ref, misc_ref, cwx_ref, cbx_ref, cwb_ref, cbb_ref, dtb_ref,
              alog_ref, dskip_ref, nw_ref, e_ref, o_ref, xs_buf, bc_buf, h_ref, y_ref):
    L = CHUNK
    c = pl.program_id(1)

    @pl.when(c == 0)
    def _init():
        xs_buf[0:HALO, :] = jnp.zeros((HALO, xs_buf.shape[1]), F32)
        bc_buf[0:HALO, :] = jnp.zeros((HALO, bc_buf.shape[1]), F32)
        h_ref[...] = jnp.zeros_like(h_ref)

    xs_buf[HALO:HALO + L, :] = xs_ref[...]
    bc_buf[HALO:HALO + L, :] = bc_ref[...]

    def conv_silu(buf, w_ref, b_ref):
        acc = b_ref[...]
        for i in range(SSD_CONV_WIDTH):
            acc = acc + buf[pl.ds(HALO - (SSD_CONV_WIDTH - 1) + i, L), :] * w_ref[i:i + 1, :]
        return acc * jax.nn.sigmoid(acc)

    xs = conv_silu(xs_buf, cwx_ref, cbx_ref)
    bcv = conv_silu(bc_buf, cwb_ref, cbb_ref)
    xs_buf[0:HALO, :] = xs_buf[L:L + HALO, :]
    bc_buf[0:HALO, :] = bc_buf[L:L + HALO, :]

    lane = lax.broadcasted_iota(jnp.int32, (1, LANES), 1)
    dt = _softplus(misc_ref[:, 0:LANES] + dtb_ref[...])
    a_neg = jnp.where(lane < SSD_HEADS, -jnp.exp(alog_ref[...]), 0.0)
    d_a = dt * a_neg
    row = lax.broadcasted_iota(jnp.int32, (L, L), 0)
    col = lax.broadcasted_iota(jnp.int32, (L, L), 1)
    causal = row >= col
    a_cum = _dot(causal.astype(F32), d_a, HIGHEST)
    a_cum_t = a_cum.T
    a_last = a_cum[L - 1:L, :]
    per_head = jnp.concatenate([dt, jnp.exp(a_last - a_cum), jnp.exp(a_cum)], axis=0)
    full = _dot(per_head, e_ref[...], HIGHEST)
    dt_f, dte_f, ea_f = full[0:L], full[L:2 * L], full[2 * L:3 * L]
    xdt = xs * dt_f
    xdt_b = _bf(xdt)
    xdte_b = _bf(xdt * dte_f)
    lane_l = lax.broadcasted_iota(jnp.int32, (L, LANES), 1)
    lower_half = lane_l < HEAD_DIM
    gw = SSD_WIDTH // SSD_GROUPS
    heads_per_group = SSD_HEADS // SSD_GROUPS
    for g in range(SSD_GROUPS):
        b_g = bcv[:, g * SSD_STATE:(g + 1) * SSD_STATE]
        c_g = _bf(bcv[:, (SSD_GROUPS + g) * SSD_STATE:(SSD_GROUPS + g + 1) * SSD_STATE])
        cb = _dot_nt(c_g, _bf(b_g))
        h_g = h_ref[g]
        y_off = _dot(c_g, _bf(h_g)) * ea_f[:, g * gw:(g + 1) * gw]
        new_state = _dot(_bf(b_g.T), xdte_b[:, g * gw:(g + 1) * gw])
        for pr in range(heads_per_group // 2):
            ms = []
            for e in range(2):
                h = g * heads_per_group + pr * 2 + e
                seg = a_cum[:, h:h + 1] - a_cum_t[h:h + 1, :]
                dec = jnp.where(causal, jnp.exp(jnp.minimum(seg, 0.0)), 0.0)
                ms.append(_bf(cb * dec))
            lhs = jnp.concatenate(ms, axis=1)
            lo = g * gw + pr * LANES
            xp = xdt_b[:, lo:lo + LANES]
            zero = jnp.zeros_like(xp)
            rhs = jnp.concatenate([jnp.where(lower_half, xp, zero),
                                   jnp.where(lower_half, zero, xp)], axis=0)
            y_ref[:, lo:lo + LANES] = _dot(lhs, rhs) + y_off[:, pr * LANES:(pr + 1) * LANES]
        h_ref[g] = h_g * ea_f[L - 1:L, g * gw:(g + 1) * gw] + new_state

    y = y_ref[...] + xs * dskip_ref[...]
    zz = z_ref[...]
    u = y * (zz * jax.nn.sigmoid(zz))
    for g in range(SSD_GROUPS):
        ug = u[:, g * gw:(g + 1) * gw]
        ms_ = jnp.mean(ug * ug, axis=-1, keepdims=True)
        o_ref[:, g * gw:(g + 1) * gw] = (
            ug * lax.rsqrt(ms_ + RMS_EPS) * nw_ref[:, g * gw:(g + 1) * gw]).astype(o_ref.dtype)


def _ssd_mixer(proj, bsz, t, cwx, cbx, cwb, cbb, dtb, alog, dskip, nw, expand):
    nc = t // CHUNK
    rows = lambda b, c: b * nc + c
    const = lambda shape: pl.BlockSpec(shape, lambda b, c: (0,) * len(shape))
    return pl.pallas_call(
        _ssd_body,
        grid=(bsz, nc),
        in_specs=[
            pl.BlockSpec((CHUNK, COLBLK), lambda b, c: (rows(b, c), BLK_Z)),
            pl.BlockSpec((CHUNK, COLBLK), lambda b, c: (rows(b, c), BLK_XS)),
            pl.BlockSpec((CHUNK, SMALLBLK), lambda b, c: (rows(b, c), BLK_BC)),
            pl.BlockSpec((CHUNK, SMALLBLK), lambda b, c: (rows(b, c), BLK_MISC)),
            const(cwx.shape), const(cbx.shape), const(cwb.shape), const(cbb.shape),
            const(dtb.shape), const(alog.shape), const(dskip.shape), const(nw.shape),
            const(expand.shape),
        ],
        out_specs=pl.BlockSpec((CHUNK, SSD_WIDTH), lambda b, c: (rows(b, c), 0)),
        out_shape=jax.ShapeDtypeStruct((bsz * t, SSD_WIDTH), BF16),
        scratch_shapes=[
            pltpu.VMEM((HALO + CHUNK, SSD_WIDTH), F32),
            pltpu.VMEM((HALO + CHUNK, SMALLBLK), F32),
            pltpu.VMEM((SSD_GROUPS, SSD_STATE, SSD_WIDTH // SSD_GROUPS), F32),
            pltpu.VMEM((CHUNK, SSD_WIDTH), F32),
        ],
        compiler_params=_params(("parallel", "arbitrary")),
        name="ssd_mixer",
    )(proj, proj, proj, proj, cwx, cbx, cwb, cbb, dtb, alog, dskip, nw, expand)


PAIRS = RWKV_HEADS // 2
MXU_WIDTH = 256
ST_AMID, ST_RMID, ST_BMID, ST_KMID, ST_AST, ST_RST, ST_BEND, ST_KEND, ST_V, ST_N = range(10)


def _split2(x):
    hi = x.astype(BF16)
    return hi, (x - hi.astype(F32)).astype(BF16)


def _split3(x):
    hi = x.astype(BF16)
    r1 = x - hi.astype(F32)
    mid = r1.astype(BF16)
    return hi, mid, (r1 - mid.astype(F32)).astype(BF16)


def _rwkv_body(has_vres, *refs):
    if has_vres:
        (r_ref, k_ref, v_ref, misc_ref, vfirst_ref, mix_ref, mixm_ref, w0_ref, wuph_ref, wupl_ref,
         a0_ref, aup_ref, gup_ref, v0_ref, vup_ref, kk_ref, ka_ref, rk_ref, lnw_ref, lnb_ref, bd_ref,
         o_ref, r_buf, k_buf, v_buf, m_buf, st_ref, h_ref, y_ref) = refs
    else:
        (r_ref, k_ref, v_ref, misc_ref, mix_ref, mixm_ref, w0_ref, wuph_ref, wupl_ref,
         a0_ref, aup_ref, gup_ref, kk_ref, ka_ref, rk_ref, lnw_ref, lnb_ref, bd_ref,
         o_ref, vfirst_out_ref, r_buf, k_buf, v_buf, m_buf, st_ref, h_ref, y_ref) = refs
    L = CHUNK
    c = pl.program_id(1)

    @pl.when(c == 0)
    def _init():
        for buf in (r_buf, k_buf, v_buf, m_buf):
            buf[0:HALO, :] = jnp.zeros((HALO, buf.shape[1]), F32)
        h_ref[...] = jnp.zeros_like(h_ref)

    def shift_lerp(ref, buf, mix):
        cur = ref[...]
        buf[HALO:HALO + L, :] = cur
        prev = buf[pl.ds(HALO - 1, L), :]
        buf[0:HALO, :] = buf[L:L + HALO, :]
        return cur + (prev - cur) * mix

    w_ = RWKV_WIDTH
    r = shift_lerp(r_ref, r_buf, mix_ref[:, 0:w_])
    k = shift_lerp(k_ref, k_buf, mix_ref[:, w_:2 * w_])
    v = shift_lerp(v_ref, v_buf, mix_ref[:, 2 * w_:3 * w_])
    m = shift_lerp(misc_ref, m_buf, mixm_ref[...])
    wa = m[:, MISC_WA:MISC_WA + LANES]
    gv = m[:, MISC_GV:MISC_GV + 2 * LANES]

    th_hi, th_lo = _split2(jnp.tanh(wa))
    wup_hi, wup_lo = wuph_ref[...], wupl_ref[...]
    w_lin = w0_ref[...] + (_dot(th_hi, wup_hi) + _dot(th_hi, wup_lo) + _dot(th_lo, wup_hi))
    log_w = -jnp.exp(-_softplus(-w_lin) - 0.5)
    a = jax.nn.sigmoid(a0_ref[...] + _dot(_bf(wa), aup_ref[...]))
    g = _dot(_bf(jax.nn.sigmoid(gv)), gup_ref[...])
    if has_vres:
        v = v + (vfirst_ref[...] - v) * jax.nn.sigmoid(v0_ref[...] + _dot(_bf(gv), vup_ref[...]))
    else:
        vfirst_out_ref[...] = v

    n_blk = RWKV_WIDTH // MXU_WIDTH

    def head_sum(x):
        parts = [t[:, j * MXU_WIDTH:(j + 1) * MXU_WIDTH] for t in _split2(x) for j in range(n_blk)]
        s = _dot(jnp.concatenate(parts, axis=0), bd_ref[...])
        return jnp.concatenate(
            [s[j * L:(j + 1) * L] + s[(n_blk + j) * L:(n_blk + j + 1) * L] for j in range(n_blk)], axis=1)

    kx = k * kk_ref[...]
    kk = kx / jnp.maximum(jnp.sqrt(head_sum(kx * kx)), L2_EPS)
    k2 = k * (1.0 + (a - 1.0) * ka_ref[...])
    av = -kk
    bv = kk * a

    row = lax.broadcasted_iota(jnp.int32, (L, L), 0)
    col = lax.broadcasted_iota(jnp.int32, (L, L), 1)
    incl = row >= col
    strict = row > col
    tri = incl.astype(BF16)
    lw_hi, lw_mid, lw_lo = _split3(log_w)
    cum = _dot(tri, lw_hi) + _dot(tri, lw_mid) + _dot(tri, lw_lo)
    cum_prev = cum - log_w
    tot = cum[L - 1:L, :]
    mid = cum[L // 2 - 1:L // 2, :]
    st_ref[ST_AMID] = av * jnp.exp(cum_prev - mid)
    st_ref[ST_RMID] = r * jnp.exp(cum - mid)
    inv_mid = jnp.exp(mid - cum)
    st_ref[ST_BMID] = bv * inv_mid
    st_ref[ST_KMID] = k2 * inv_mid
    st_ref[ST_AST] = av * jnp.exp(cum_prev)
    st_ref[ST_RST] = r * jnp.exp(cum)
    to_end = jnp.exp(tot - cum)
    st_ref[ST_BEND] = bv * to_end
    st_ref[ST_KEND] = k2 * to_end
    st_ref[ST_V] = v
    e_tot_all = jnp.exp(tot)

    lane_l = lax.broadcasted_iota(jnp.int32, (L, LANES), 1)
    lower = lane_l < HEAD_DIM
    r2 = lax.broadcasted_iota(jnp.int32, (LANES, LANES), 0)
    c2 = lax.broadcasted_iota(jnp.int32, (LANES, LANES), 1)
    same_head = (r2 < HEAD_DIM) == (c2 < HEAD_DIM)
    diag = r2 == c2
    n_rounds = L.bit_length() - 1

    lanes_of = [slice(p * LANES, (p + 1) * LANES) for p in range(PAIRS)]
    masks = (lower, jnp.logical_not(lower))
    heads = [(p, h) for p in range(PAIRS) for h in range(2)]
    v_b, v_sw, bk_t = [], [], []
    for sl in lanes_of:
        v_p = st_ref[ST_V, :, sl]
        v_b.append(_bf(v_p))
        v_sw.append(_bf(pltpu.roll(v_p, HEAD_DIM, 1)))
        bk_t.append(_bf(jnp.concatenate([st_ref[ST_BMID, :, sl].T, st_ref[ST_KMID, :, sl].T], axis=1)))
    zero_b = jnp.zeros_like(v_sw[0])

    mm, a_ak, a_rbk = [], [], []
    for p, h in heads:
        sl, mh = lanes_of[p], masks[h]
        ar = _bf(jnp.concatenate([jnp.where(mh, st_ref[ST_AMID, :, sl], 0.0),
                                  jnp.where(mh, st_ref[ST_RMID, :, sl], 0.0)], axis=0))
        aa = _dot(ar, bk_t[p])
        mm.append(_bf(jnp.where(strict, aa[0:L, 0:L], 0.0)))
        a_ak.append(_bf(jnp.where(strict, aa[0:L, L:2 * L], 0.0)))
        a_rbk.append(_bf(jnp.concatenate([jnp.where(incl, aa[L:2 * L, 0:L], 0.0),
                                          jnp.where(incl, aa[L:2 * L, L:2 * L], 0.0)], axis=1)))
    xs_ = [jnp.where(masks[h], st_ref[ST_AST, :, lanes_of[p]], _dot(a_ak[i], v_sw[p]))
           for i, (p, h) in enumerate(heads)]
    for rnd in range(n_rounds):
        xs_ = [x + _dot(m_, _bf(x)) for x, m_ in zip(xs_, mm)]
        if rnd + 1 < n_rounds:
            mm = [_bf(_dot(m_, m_)) for m_ in mm]
    yz = [_dot(a_rbk[i], jnp.concatenate([_bf(xs_[i]), jnp.where(masks[h], zero_b, v_sw[p])], axis=0))
          for i, (p, h) in enumerate(heads)]

    for p, sl in enumerate(lanes_of):
        x0, x1, yz0, yz1 = xs_[2 * p], xs_[2 * p + 1], yz[2 * p], yz[2 * p + 1]
        a_bar = jnp.where(lower, x0, x1)
        uv = pltpu.roll(jnp.where(lower, x1, x0), HEAD_DIM, 1)
        r_bar = st_ref[ST_RST, :, sl] + jnp.where(lower, yz0, yz1)
        yv = pltpu.roll(jnp.where(lower, yz1, yz0), HEAD_DIM, 1)
        b_end_t = _bf(st_ref[ST_BEND, :, sl].T)
        k_end_t = _bf(st_ref[ST_KEND, :, sl].T)
        g_mat = (jnp.where(same_head, _dot(b_end_t, _bf(a_bar)), 0.0)
                 + jnp.where(diag, e_tot_all[:, sl], 0.0))
        h_add = jnp.where(same_head, _dot(b_end_t, _bf(uv)) + _dot(k_end_t, v_b[p]), 0.0)
        h_old = _bf(h_ref[p])
        y_ref[:, sl] = _dot(_bf(r_bar), h_old) + yv
        h_ref[p] = _dot(_bf(g_mat), h_old) + h_add

    y = y_ref[...]
    inv_n = 1.0 / HEAD_DIM
    mu = head_sum(y) * inv_n
    d = y - mu
    var = head_sum(d * d) * inv_n
    y = d * lax.rsqrt(var + RWKV_GN_EPS) * lnw_ref[...] + lnb_ref[...]
    y = y + head_sum(r * k2 * rk_ref[...]) * v
    o_ref[...] = (y * g).astype(o_ref.dtype)


def _rwkv_mixer(proj, bsz, t, vfirst, mix, mixm, w0, wup_hi, wup_lo, a0, aup, gup, v0, vup, k_k, k_a,
                r_k, ln_w, ln_b, bd):
    nc = t // CHUNK
    rows = lambda b, c: b * nc + c
    const = lambda x: pl.BlockSpec(x.shape, lambda b, c: (0,) * x.ndim)
    has_vres = vfirst is not None
    act_specs = [
        pl.BlockSpec((CHUNK, COLBLK), lambda b, c: (rows(b, c), BLK_R)),
        pl.BlockSpec((CHUNK, COLBLK), lambda b, c: (rows(b, c), BLK_K)),
        pl.BlockSpec((CHUNK, COLBLK), lambda b, c: (rows(b, c), BLK_V)),
        pl.BlockSpec((CHUNK, SMALLBLK), lambda b, c: (rows(b, c), BLK_MISC)),
    ]
    wide_spec = pl.BlockSpec((CHUNK, RWKV_WIDTH), lambda b, c: (rows(b, c), 0))
    if has_vres:
        args = [proj, proj, proj, proj, vfirst, mix, mixm, w0, wup_hi, wup_lo, a0, aup, gup, v0, vup,
                k_k, k_a, r_k, ln_w, ln_b, bd]
        in_specs = act_specs + [wide_spec] + [const(x) for x in args[5:]]
        out_specs = wide_spec
        out_shape = jax.ShapeDtypeStruct((bsz * t, RWKV_WIDTH), BF16)
    else:
        args = [proj, proj, proj, proj, mix, mixm, w0, wup_hi, wup_lo, a0, aup, gup,
                k_k, k_a, r_k, ln_w, ln_b, bd]
        in_specs = act_specs + [const(x) for x in args[4:]]
        out_specs = (wide_spec, wide_spec)
        out_shape = (jax.ShapeDtypeStruct((bsz * t, RWKV_WIDTH), BF16),
                     jax.ShapeDtypeStruct((bsz * t, RWKV_WIDTH), F32))
    return pl.pallas_call(
        functools.partial(_rwkv_body, has_vres),
        grid=(bsz, nc),
        in_specs=in_specs,
        out_specs=out_specs,
        out_shape=out_shape,
        scratch_shapes=[
            pltpu.VMEM((HALO + CHUNK, RWKV_WIDTH), F32),
            pltpu.VMEM((HALO + CHUNK, RWKV_WIDTH), F32),
            pltpu.VMEM((HALO + CHUNK, RWKV_WIDTH), F32),
            pltpu.VMEM((HALO + CHUNK, SMALLBLK), F32),
            pltpu.VMEM((ST_N, CHUNK, RWKV_WIDTH), F32),
            pltpu.VMEM((PAIRS, LANES, LANES), F32),
            pltpu.VMEM((CHUNK, RWKV_WIDTH), F32),
        ],
        compiler_params=_params(("parallel", "arbitrary")),
        name="rwkv_mixer",
    )(*args)


TOK_ROWS = D_MODEL // LANES


def _store_token_major(ref, val):
    bm = val.shape[0]
    for s in range(TOK_ROWS):
        ref[pl.ds(s, bm, stride=TOK_ROWS), :] = val[:, s * LANES:(s + 1) * LANES]


def _load_token_major(ref, first_row, bm):
    return [ref[pl.ds(first_row + s, bm, stride=TOK_ROWS), :] for s in range(TOK_ROWS)]


def _outproj_ln_body(emit_rows, ys_ref, yr_ref, wt_ref, wb_ref, x_ref, g_ref, b_ref, of_ref, ob_ref,
                     *rows_ref):
    mixed = _dot(ys_ref[...], wt_ref[...]) + _dot(yr_ref[...], wb_ref[...])
    out = _layer_norm(DEEPNORM_ALPHA * x_ref[...] + mixed, g_ref[...], b_ref[...])
    of_ref[...] = out
    ob_ref[...] = out.astype(BF16)
    if emit_rows:
        _store_token_major(rows_ref[0], out)


def _outproj_ln(ys, yr, w_out, x, g, b, bm, emit_rows):
    n = x.shape[0]
    half = w_out.shape[0] // 2
    row = lambda width: pl.BlockSpec((bm, width), lambda i: (i, 0))
    vec = pl.BlockSpec((1, D_MODEL), lambda i: (0, 0))
    out_specs = [row(D_MODEL), row(D_MODEL)]
    out_shape = [jax.ShapeDtypeStruct((n, D_MODEL), F32), jax.ShapeDtypeStruct((n, D_MODEL), BF16)]
    if emit_rows:
        out_specs.append(pl.BlockSpec((bm * TOK_ROWS, LANES), lambda i: (i, 0)))
        out_shape.append(jax.ShapeDtypeStruct((n * TOK_ROWS, LANES), F32))
    return pl.pallas_call(
        functools.partial(_outproj_ln_body, emit_rows),
        grid=(n // bm,),
        in_specs=[row(half), row(half),
                  pl.BlockSpec((half, D_MODEL), lambda i: (0, 0)),
                  pl.BlockSpec((half, D_MODEL), lambda i: (1, 0)),
                  row(D_MODEL), vec, vec],
        out_specs=tuple(out_specs),
        out_shape=tuple(out_shape),
        compiler_params=_params(("parallel",)),
        name="out_proj_ln",
    )(ys, yr, w_out, w_out, x, g, b)


def _down_ln_body(h_ref, w_ref, x_ref, g_ref, b_ref, of_ref, ob_ref, acc_ref):
    kk = pl.program_id(1)

    @pl.when(kk == 0)
    def _zero():
        acc_ref[...] = jnp.zeros_like(acc_ref)

    acc_ref[...] += _dot(h_ref[...], w_ref[...])

    @pl.when(kk == pl.num_programs(1) - 1)
    def _finish():
        out = _layer_norm(DEEPNORM_ALPHA * x_ref[...] + acc_ref[...], g_ref[...], b_ref[...])
        of_ref[...] = out
        ob_ref[...] = out.astype(BF16)


def _down_ln(h, w, x, g, b, bm, bk):
    n, kdim = h.shape
    row = pl.BlockSpec((bm, D_MODEL), lambda i, k: (i, 0))
    vec = pl.BlockSpec((1, D_MODEL), lambda i, k: (0, 0))
    return pl.pallas_call(
        _down_ln_body,
        grid=(n // bm, kdim // bk),
        in_specs=[pl.BlockSpec((bm, bk), lambda i, k: (i, k)),
                  pl.BlockSpec((bk, D_MODEL), lambda i, k: (k, 0)),
                  row, vec, vec],
        out_specs=(row, row),
        out_shape=(jax.ShapeDtypeStruct((n, D_MODEL), F32),
                   jax.ShapeDtypeStruct((n, D_MODEL), BF16)),
        scratch_shapes=[pltpu.VMEM((bm, D_MODEL), F32)],
        compiler_params=_params(("parallel", "arbitrary")),
        name="down_proj_ln",
    )(h, w, x, g, b)


def _swiglu_up_body(x_ref, w1_ref, w3_ref, o_ref):
    xb = x_ref[...]
    h1 = _dot(xb, w1_ref[...])
    h3 = _dot(xb, w3_ref[...])
    o_ref[...] = (h1 * jax.nn.sigmoid(h1) * h3).astype(o_ref.dtype)


def _swiglu_up(xb, w1, w3, bm, bn):
    n = xb.shape[0]
    f = w1.shape[1]
    return pl.pallas_call(
        _swiglu_up_body,
        grid=(n // bm, f // bn),
        in_specs=[pl.BlockSpec((bm, D_MODEL), lambda i, j: (i, 0)),
                  pl.BlockSpec((D_MODEL, bn), lambda i, j: (0, j)),
                  pl.BlockSpec((D_MODEL, bn), lambda i, j: (0, j))],
        out_specs=pl.BlockSpec((bm, bn), lambda i, j: (i, j)),
        out_shape=jax.ShapeDtypeStruct((n, f), BF16),
        compiler_params=_params(("parallel", "parallel")),
        name="swiglu_up",
    )(xb, w1, w3)


def _router_body(x_ref, wr_ref, o_ref):
    logits = _dot(x_ref[...], wr_ref[...], HIGHEST)
    lane = lax.broadcasted_iota(jnp.int32, logits.shape, 1)
    neg = jnp.float32(-jnp.inf)
    logits = jnp.where(lane < N_EXPERTS, logits, neg)
    top1 = jnp.max(logits, axis=-1, keepdims=True)
    idx1 = jnp.min(jnp.where(logits == top1, lane, LANES), axis=-1, keepdims=True)
    rest = jnp.where(lane == idx1, neg, logits)
    top2 = jnp.max(rest, axis=-1, keepdims=True)
    idx2 = jnp.min(jnp.where(rest == top2, lane, LANES), axis=-1, keepdims=True)
    e2 = jnp.exp(top2 - top1)
    gate1 = 1.0 / (1.0 + e2)
    gate2 = e2 / (1.0 + e2)
    o_ref[...] = (jnp.where(lane == 0, idx1.astype(F32), 0.0) + jnp.where(lane == 1, idx2.astype(F32), 0.0)
                  + jnp.where(lane == 2, gate1, 0.0) + jnp.where(lane == 3, gate2, 0.0))


def _router(x, wr, bm):
    n = x.shape[0]
    return pl.pallas_call(
        _router_body,
        grid=(n // bm,),
        in_specs=[pl.BlockSpec((bm, D_MODEL), lambda i: (i, 0)),
                  pl.BlockSpec((D_MODEL, LANES), lambda i: (0, 0))],
        out_specs=pl.BlockSpec((bm, LANES), lambda i: (i, 0)),
        out_shape=jax.ShapeDtypeStruct((n, LANES), F32),
        compiler_params=_params(("parallel",)),
        name="moe_router",
    )(x, wr)


MOE_BM = 512


def _row_copy(src_hbm, src_tok, dst_buf, dst_tok, sem):
    return pltpu.make_async_copy(
        src_hbm.at[pl.ds(pl.multiple_of(src_tok * TOK_ROWS, TOK_ROWS), TOK_ROWS), :],
        dst_buf.at[pl.ds(pl.multiple_of(dst_tok * TOK_ROWS, TOK_ROWS), TOK_ROWS), :], sem)


DISPATCH_BM = 512
DISPATCH_UNROLL = 4


def _moe_dispatch_body(n_tok, bm, dest_ref, pad_lo_ref, pad_hi_ref, rows_hbm, o_hbm, zero_buf, sem):
    i = pl.program_id(0)
    base = i * bm

    def copies(j):
        return [_row_copy(rows_hbm, base + j, o_hbm, dest_ref[slot * n_tok + base + j], sem)
                for slot in range(2)]

    def issue(j, carry):
        for cp in copies(j):
            cp.start()
        return carry

    def wait(j, carry):
        for cp in copies(j):
            cp.wait()
        return carry

    lax.fori_loop(0, bm, issue, 0, unroll=DISPATCH_UNROLL)
    lax.fori_loop(0, bm, wait, 0, unroll=DISPATCH_UNROLL)

    @pl.when(i == 0)
    def _zero_padding_rows():
        zero_buf[...] = jnp.zeros_like(zero_buf)

        def pad_copy(r):
            return pltpu.make_async_copy(
                zero_buf, o_hbm.at[pl.ds(pl.multiple_of(r * TOK_ROWS, TOK_ROWS), TOK_ROWS), :], sem)

        def z_issue(r, carry):
            pad_copy(r).start()
            return carry

        def z_wait(r, carry):
            pad_copy(r).wait()
            return carry

        for e in range(N_EXPERTS + 1):
            lax.fori_loop(pad_lo_ref[e], pad_hi_ref[e], z_issue, 0)
            lax.fori_loop(pad_lo_ref[e], pad_hi_ref[e], z_wait, 0)


def _moe_dispatch(dest, pad_lo, pad_hi, rows, n_tok, n_rows):
    bm = DISPATCH_BM if n_tok % DISPATCH_BM == 0 else n_tok
    return pl.pallas_call(
        functools.partial(_moe_dispatch_body, n_tok, bm),
        grid_spec=pltpu.PrefetchScalarGridSpec(
            num_scalar_prefetch=3,
            grid=(n_tok // bm,),
            in_specs=[pl.BlockSpec(memory_space=pl.ANY)],
            out_specs=pl.BlockSpec(memory_space=pl.ANY),
            scratch_shapes=[pltpu.VMEM((TOK_ROWS, LANES), F32), pltpu.SemaphoreType.DMA(())],
        ),
        out_shape=jax.ShapeDtypeStruct((n_rows * TOK_ROWS, LANES), F32),
        compiler_params=_params(("arbitrary",)),
        name="moe_dispatch",
    )(dest, pad_lo, pad_hi, rows)


def _expert_changed(te_ref, i):
    return jnp.logical_or(i == 0, te_ref[i] != te_ref[jnp.maximum(i - 1, 0)])


def _moe_up_body(te_ref, valid_ref, x_ref, w1_ref, w3_ref, o_ref, w1_b, w3_b):
    i = pl.program_id(1)

    @pl.when(_expert_changed(te_ref, i))
    def _cast_weights():
        w1_b[...] = w1_ref[...].astype(BF16)
        w3_b[...] = w3_ref[...].astype(BF16)

    @pl.when(valid_ref[i] != 0)
    def _compute():
        xb = jnp.concatenate([_bf(p) for p in _load_token_major(x_ref, 0, MOE_BM)], axis=1)
        h1 = _dot(xb, w1_b[...])
        h3 = _dot(xb, w3_b[...])
        o_ref[...] = (h1 * jax.nn.sigmoid(h1) * h3).astype(o_ref.dtype)

    @pl.when(valid_ref[i] == 0)
    def _empty():
        o_ref[...] = jnp.zeros_like(o_ref)


def _moe_up(te, valid, xs_rows, w1, w3, bn):
    n_rows = xs_rows.shape[0] // TOK_ROWS
    fe = w1.shape[2]
    wspec = pl.BlockSpec((None, D_MODEL, bn), lambda j, i, te, valid: (te[i], 0, j),
                         pipeline_mode=pl.Buffered(1))
    return pl.pallas_call(
        _moe_up_body,
        grid_spec=pltpu.PrefetchScalarGridSpec(
            num_scalar_prefetch=2,
            grid=(fe // bn, n_rows // MOE_BM),
            in_specs=[pl.BlockSpec((MOE_BM * TOK_ROWS, LANES), lambda j, i, te, valid: (i, 0)),
                      wspec, wspec],
            out_specs=pl.BlockSpec((MOE_BM, bn), lambda j, i, te, valid: (i, j)),
            scratch_shapes=[pltpu.VMEM((D_MODEL, bn), BF16), pltpu.VMEM((D_MODEL, bn), BF16)],
        ),
        out_shape=jax.ShapeDtypeStruct((n_rows, fe), BF16),
        compiler_params=_params(("arbitrary", "arbitrary")),
        name="moe_up",
    )(te, valid, xs_rows, w1, w3)


DOWN_SPLIT = 2


def _moe_down_body(te_ref, valid_ref, h_ref, w_ref, o_ref, w_b):
    i = pl.program_id(1)

    @pl.when(_expert_changed(te_ref, i))
    def _cast_weights():
        w_b[...] = w_ref[...].astype(BF16)

    @pl.when(valid_ref[i] != 0)
    def _compute():
        y = _dot(h_ref[...], w_b[...])
        for s in range(TOK_ROWS // DOWN_SPLIT):
            o_ref[:, s, :] = y[:, s * LANES:(s + 1) * LANES]

    @pl.when(valid_ref[i] == 0)
    def _empty():
        o_ref[...] = jnp.zeros_like(o_ref)


def _moe_down(te, valid, h, w2):
    n_rows, fe = h.shape
    bn = D_MODEL // DOWN_SPLIT
    sub = TOK_ROWS // DOWN_SPLIT
    return pl.pallas_call(
        _moe_down_body,
        grid_spec=pltpu.PrefetchScalarGridSpec(
            num_scalar_prefetch=2,
            grid=(DOWN_SPLIT, n_rows // MOE_BM),
            in_specs=[pl.BlockSpec((MOE_BM, fe), lambda c, i, te, valid: (i, 0)),
                      pl.BlockSpec((None, fe, bn), lambda c, i, te, valid: (te[i], 0, c))],
            out_specs=pl.BlockSpec((MOE_BM, None, sub, LANES), lambda c, i, te, valid: (i, c, 0, 0)),
            scratch_shapes=[pltpu.VMEM((fe, bn), BF16)],
        ),
        out_shape=jax.ShapeDtypeStruct((n_rows, DOWN_SPLIT, sub, LANES), F32),
        compiler_params=_params(("arbitrary", "arbitrary")),
        name="moe_down",
    )(te, valid, h, w2)


def _moe_combine_ln_body(n_tok, dest_ref, ys_hbm, route_ref, x_ref, g_ref, b_ref, of_ref, ob_ref, buf,
                         sem):
    bm = x_ref.shape[0]
    base = pl.program_id(0) * bm

    def copies(j):
        return [_row_copy(ys_hbm, dest_ref[slot * n_tok + base + j], buf, slot * bm + j, sem)
                for slot in range(2)]

    def issue(j, carry):
        for cp in copies(j):
            cp.start()
        return carry

    def wait(j, carry):
        for cp in copies(j):
            cp.wait()
        return carry

    lax.fori_loop(0, bm, issue, 0, unroll=DISPATCH_UNROLL)
    lax.fori_loop(0, bm, wait, 0, unroll=DISPATCH_UNROLL)
    route = route_ref[...]
    gate1, gate2 = route[:, 2:3], route[:, 3:4]
    first = _load_token_major(buf, 0, bm)
    second = _load_token_major(buf, bm * TOK_ROWS, bm)
    f = jnp.concatenate([gate1 * p + gate2 * q for p, q in zip(first, second)], axis=1)
    out = _layer_norm(DEEPNORM_ALPHA * x_ref[...] + f, g_ref[...], b_ref[...])
    of_ref[...] = out
    ob_ref[...] = out.astype(BF16)


def _moe_combine_ln(dest, ys, route, x, g, b, bm):
    n = x.shape[0]
    row = pl.BlockSpec((bm, D_MODEL), lambda i, dest: (i, 0))
    vec = pl.BlockSpec((1, D_MODEL), lambda i, dest: (0, 0))
    route_spec = pl.BlockSpec((bm, LANES), lambda i, dest: (i, 0))
    return pl.pallas_call(
        functools.partial(_moe_combine_ln_body, n),
        grid_spec=pltpu.PrefetchScalarGridSpec(
            num_scalar_prefetch=1,
            grid=(n // bm,),
            in_specs=[pl.BlockSpec(memory_space=pl.ANY), route_spec, row, vec, vec],
            out_specs=(row, row),
            scratch_shapes=[pltpu.VMEM((2 * bm * TOK_ROWS, LANES), F32), pltpu.SemaphoreType.DMA(())],
        ),
        out_shape=(jax.ShapeDtypeStruct((n, D_MODEL), F32), jax.ShapeDtypeStruct((n, D_MODEL), BF16)),
        compiler_params=_params(("arbitrary",)),
        name="moe_combine_ln",
    )(dest, ys, route, x, g, b)


def _moe_dispatch_plan(route, n):
    n_rows = 2 * n + N_EXPERTS * MOE_BM
    expert = route[:, 0:2].astype(jnp.int32).T.reshape(-1)
    onehot = (expert[:, None] == jnp.arange(N_EXPERTS)[None, :]).astype(jnp.int32)
    csum = jnp.cumsum(onehot, axis=0)
    rank = jnp.sum(csum * onehot, axis=1) - 1
    count = csum[-1]
    padded = (count + MOE_BM - 1) // MOE_BM * MOE_BM
    ends = jnp.cumsum(padded)
    starts = ends - padded
    dest = starts[expert] + rank
    tile_start = jnp.arange(n_rows // MOE_BM, dtype=jnp.int32) * MOE_BM
    te = jnp.minimum(jnp.sum((tile_start[:, None] >= ends[None, :]).astype(jnp.int32), axis=1),
                     N_EXPERTS - 1)
    valid = (tile_start < ends[-1]).astype(jnp.int32)
    i32 = lambda v: v.astype(jnp.int32)
    pad_lo = jnp.concatenate([starts + count, ends[-1:]])
    pad_hi = jnp.concatenate([ends, jnp.full((1,), n_rows, ends.dtype)])
    return i32(dest), i32(te), valid, i32(pad_lo), i32(pad_hi), n_rows


def _pad_lanes(v, width):
    v = v.reshape(1, -1)
    return jnp.pad(v, ((0, 0), (0, width - v.shape[1])))


def _pick(n, pref):
    return pref if n % pref == 0 else n


def kernel(x, w_in, w_in_vres, ssd_conv_w, ssd_conv_b, ssd_dt_bias, ssd_a_log, ssd_d, ssd_norm_w, rw_mix, rw_vres_mix, rw_w0, rw_w_up, rw_a0, rw_a_up, rw_v0, rw_v_up, rw_g_up, rw_k_k, rw_k_a, rw_r_k, rw_ln_w, rw_ln_b, w_out, ln1_g, ln1_b, ln2_g, ln2_b, ffn_w1, ffn_w3, ffn_w2, moe_router, moe_w1, moe_w3, moe_w2):
    bsz, t, d = x.shape
    n = bsz * t
    depth = w_in.shape[0]
    assert d == D_MODEL and t % CHUNK == 0
    xf = x.reshape(n, d)
    xb = xf.astype(BF16)

    head_of_lane = jnp.arange(RWKV_WIDTH) // HEAD_DIM
    expand = (jnp.arange(LANES)[:, None] == head_of_lane[None, :]).astype(F32)
    blk_head = jnp.arange(MXU_WIDTH) // HEAD_DIM
    head_ones = (blk_head[:, None] == blk_head[None, :]).astype(BF16)

    bm_big = _pick(n, 1024)
    bm_mid = _pick(n, 512)
    bm_small = _pick(n, 256)
    row = lambda v: v.reshape(1, -1)
    zcols = lambda k: jnp.zeros((d, k), F32)

    vfirst = None
    for l in range(depth):
        w = w_in[l]
        o = 0
        segs = {}
        for name, width in (("z", SSD_WIDTH), ("xs", SSD_WIDTH), ("bc", 2 * SSD_GROUPS * SSD_STATE),
                            ("dt", SSD_HEADS), ("r", RWKV_WIDTH), ("k", RWKV_WIDTH), ("v", RWKV_WIDTH),
                            ("w_lo", W_LORA), ("a_lo", A_LORA), ("g_lo", G_LORA)):
            segs[name] = w[:, o:o + width]
            o += width
        v_lo_cols = w_in_vres[l - 1] if l > 0 else zcols(V_LORA)
        misc_cols = jnp.concatenate(
            [segs["dt"], zcols(MISC_WA - SSD_HEADS), segs["w_lo"], segs["a_lo"], segs["g_lo"], v_lo_cols,
             zcols(SMALLBLK - MISC_GV - G_LORA - V_LORA)], axis=1)
        w_comb = jnp.concatenate([segs["z"], segs["xs"], segs["r"], segs["k"], segs["v"], segs["bc"],
                                  misc_cols], axis=1).astype(BF16)

        proj = _matmul(xb, w_comb, bm_big, COLBLK, F32)

        cw, cb = ssd_conv_w[l], ssd_conv_b[l]
        y_ssd = _ssd_mixer(
            proj, bsz, t, cw[:, :SSD_WIDTH], row(cb[:SSD_WIDTH]), cw[:, SSD_WIDTH:], row(cb[SSD_WIDTH:]),
            _pad_lanes(ssd_dt_bias[l], LANES), _pad_lanes(ssd_a_log[l], LANES),
            row(jnp.repeat(ssd_d[l], HEAD_DIM)), row(ssd_norm_w[l]), expand)

        mix = rw_mix[l]
        v_mix = rw_vres_mix[l - 1] if l > 0 else jnp.zeros((V_LORA,), F32)
        mix_misc = jnp.concatenate([jnp.zeros((MISC_WA,), F32), mix[3 * RWKV_WIDTH:], v_mix,
                                    jnp.zeros((SMALLBLK - MISC_GV - G_LORA - V_LORA,), F32)])
        wup = jnp.pad(rw_w_up[l], ((0, LANES - W_LORA), (0, 0)))
        wup_hi = wup.astype(BF16)
        wup_lo = (wup - wup_hi.astype(F32)).astype(BF16)
        aup = jnp.pad(rw_a_up[l], ((W_LORA, 0), (0, 0))).astype(BF16)
        gup = jnp.pad(rw_g_up[l], ((0, 2 * LANES - G_LORA), (0, 0))).astype(BF16)
        if l > 0:
            v0 = row(rw_v0[l - 1])
            vup = jnp.pad(rw_v_up[l - 1], ((G_LORA, 2 * LANES - G_LORA - V_LORA), (0, 0))).astype(BF16)
        else:
            v0 = vup = None
        res = _rwkv_mixer(
            proj, bsz, t, vfirst, row(mix[:3 * RWKV_WIDTH]), row(mix_misc), row(rw_w0[l]), wup_hi, wup_lo,
            row(rw_a0[l]), aup, gup, v0, vup, row(rw_k_k[l]), row(rw_k_a[l]), row(rw_r_k[l]),
            row(rw_ln_w[l]), row(rw_ln_b[l]), head_ones)
        if l == 0:
            y_rw, vfirst = res
        else:
            y_rw = res

        is_moe = l % 2 == 1
        res = _outproj_ln(y_ssd, y_rw, w_out[l].astype(BF16), xf, row(ln1_g[l]), row(ln1_b[l]), bm_small,
                          emit_rows=is_moe)
        i = l // 2
        if not is_moe:
            xf, xb = res
            h = _swiglu_up(xb, ffn_w1[i].astype(BF16), ffn_w3[i].astype(BF16), bm_big, 512)
            xf, xb = _down_ln(h, ffn_w2[i].astype(BF16), xf, row(ln2_g[l]), row(ln2_b[l]), bm_mid, 1408)
        else:
            xf, xb, x_rows = res
            route = _router(xf, jnp.pad(moe_router[i], ((0, 0), (0, LANES - N_EXPERTS))), bm_mid)
            dest, te, valid, pad_lo, pad_hi, n_rows = _moe_dispatch_plan(route, n)
            xs_rows = _moe_dispatch(dest, pad_lo, pad_hi, x_rows, n, n_rows)
            h = _moe_up(te, valid, xs_rows, moe_w1[i], moe_w3[i], 1408)
            ys = _moe_down(te, valid, h, moe_w2[i]).reshape(n_rows * TOK_ROWS, LANES)
            xf, xb = _moe_combine_ln(dest, ys, route, xf, row(ln2_g[l]), row(ln2_b[l]), bm_small)

    return xf.reshape(bsz, t, d)
```

```python
import functools

import jax
import jax.numpy as jnp
from jax import lax
from jax.experimental import pallas as pl
from jax.experimental.pallas import tpu as pltpu

F32 = jnp.float32
BF16 = jnp.bfloat16
HIGHEST = lax.Precision.HIGHEST

D_MODEL = 2048
HEAD_DIM = 64
SSD_WIDTH = 1024
RWKV_WIDTH = 1024
SSD_HEADS = SSD_WIDTH // HEAD_DIM
SSD_GROUPS = 2
SSD_STATE = 128
SSD_CONV_WIDTH = 4
RWKV_HEADS = RWKV_WIDTH // HEAD_DIM
W_LORA = 64
A_LORA = 64
V_LORA = 32
G_LORA = 160
N_EXPERTS = 8
DEPTH = 4
DEEPNORM_ALPHA = (2 * DEPTH) ** 0.25
LN_EPS = 1e-5
RMS_EPS = 1e-5
RWKV_GN_EPS = 64e-5
L2_EPS = 1e-12

LANES = 128
SUBLANES = 8
VMEM_LIMIT = 56 * 1024 * 1024

CHUNK = 128
HALO = SUBLANES

COLBLK = 1024
BLK_Z, BLK_XS, BLK_R, BLK_K, BLK_V = 0, 1, 2, 3, 4
SMALLBLK = 512
BLK_BC = 10
BLK_MISC = 11
PROJ_WIDTH = 6 * COLBLK
MISC_WA = 128
MISC_GV = 256


def _params(sem, vmem=VMEM_LIMIT):
    return pltpu.CompilerParams(dimension_semantics=sem, vmem_limit_bytes=vmem)


def _dot(a, b, precision=None):
    return jnp.dot(a, b, preferred_element_type=F32, precision=precision)


def _dot_nt(a, b):
    return lax.dot_general(a, b, (((1,), (1,)), ((), ())), preferred_element_type=F32)


def _bf(x):
    return x.astype(BF16)


def _softplus(x):
    return jnp.maximum(x, 0.0) + jnp.log1p(jnp.exp(-jnp.abs(x)))


def _layer_norm(y, g, b):
    mu = jnp.mean(y, axis=-1, keepdims=True)
    d = y - mu
    var = jnp.mean(d * d, axis=-1, keepdims=True)
    return d * lax.rsqrt(var + LN_EPS) * g + b


def _mm_body(x_ref, w_ref, o_ref):
    o_ref[...] = _dot(x_ref[...], w_ref[...]).astype(o_ref.dtype)


def _matmul(x, w, bm, bn, out_dtype):
    m, k = x.shape
    n = w.shape[1]
    return pl.pallas_call(
        _mm_body,
        grid=(m // bm, n // bn),
        in_specs=[pl.BlockSpec((bm, k), lambda i, j: (i, 0)),
                  pl.BlockSpec((k, bn), lambda i, j: (0, j))],
        out_specs=pl.BlockSpec((bm, bn), lambda i, j: (i, j)),
        out_shape=jax.ShapeDtypeStruct((m, n), out_dtype),
        compiler_params=_params(("parallel", "parallel")),
        name="in_proj",
    )(x, w)


def _ssd_body(z_ref, xs_ref, bc_ref, misc_ref, cwx_ref, cbx_ref, cwb_ref, cbb_ref, dtb_ref,
              alog_ref, dskip_ref, nw_ref, e_ref, o_ref, xs_buf, bc_buf, h_ref, y_ref):
    L = CHUNK
    c = pl.program_id(1)

    @pl.when(c == 0)
    def _init():
        xs_buf[0:HALO, :] = jnp.zeros((HALO, xs_buf.shape[1]), F32)
        bc_buf[0:HALO, :] = jnp.zeros((HALO, bc_buf.shape[1]), F32)
        h_ref[...] = jnp.zeros_like(h_ref)

    xs_buf[HALO:HALO + L, :] = xs_ref[...]
    bc_buf[HALO:HALO + L, :] = bc_ref[...]

    def conv_silu(buf, w_ref, b_ref):
        acc = b_ref[...]
        for i in range(SSD_CONV_WIDTH):
            acc = acc + buf[pl.ds(HALO - (SSD_CONV_WIDTH - 1) + i, L), :] * w_ref[i:i + 1, :]
        return acc * jax.nn.sigmoid(acc)

    xs = conv_silu(xs_buf, cwx_ref, cbx_ref)
    bcv = conv_silu(bc_buf, cwb_ref, cbb_ref)
    xs_buf[0:HALO, :] = xs_buf[L:L + HALO, :]
    bc_buf[0:HALO, :] = bc_buf[L:L + HALO, :]

    lane = lax.broadcasted_iota(jnp.int32, (1, LANES), 1)
    dt = _softplus(misc_ref[:, 0:LANES] + dtb_ref[...])
    a_neg = jnp.where(lane < SSD_HEADS, -jnp.exp(alog_ref[...]), 0.0)
    d_a = dt * a_neg
    row = lax.broadcasted_iota(jnp.int32, (L, L), 0)
    col = lax.broadcasted_iota(jnp.int32, (L, L), 1)
    causal = row >= col
    tri = causal.astype(BF16)
    a_cum = sum(_dot(tri, t) for t in _split3(d_a))
    a_cum_t = a_cum.T
    a_last = a_cum[L - 1:L, :]
    per_head = jnp.concatenate([dt, jnp.exp(a_last - a_cum), jnp.exp(a_cum)], axis=0)
    head_to_lanes = e_ref[...]
    full = sum(_dot(t, head_to_lanes) for t in _split3(per_head))
    dt_f, dte_f, ea_f = full[0:L], full[L:2 * L], full[2 * L:3 * L]
    xdt = xs * dt_f
    xdt_b = _bf(xdt)
    xdte_b = _bf(xdt * dte_f)
    lane_l = lax.broadcasted_iota(jnp.int32, (L, LANES), 1)
    lower_half = lane_l < HEAD_DIM
    gw = SSD_WIDTH // SSD_GROUPS
    heads_per_group = SSD_HEADS // SSD_GROUPS
    for g in range(SSD_GROUPS):
        b_g = bcv[:, g * SSD_STATE:(g + 1) * SSD_STATE]
        c_g = _bf(bcv[:, (SSD_GROUPS + g) * SSD_STATE:(SSD_GROUPS + g + 1) * SSD_STATE])
        cb = _dot_nt(c_g, _bf(b_g))
        h_g = h_ref[g]
        y_off = _dot(c_g, _bf(h_g)) * ea_f[:, g * gw:(g + 1) * gw]
        new_state = _dot(_bf(b_g.T), xdte_b[:, g * gw:(g + 1) * gw])
        for pr in range(heads_per_group // 2):
            ms = []
            for e in range(2):
                h = g * heads_per_group + pr * 2 + e
                seg = a_cum[:, h:h + 1] - a_cum_t[h:h + 1, :]
                dec = jnp.where(causal, jnp.exp(jnp.minimum(seg, 0.0)), 0.0)
                ms.append(_bf(cb * dec))
            lhs = jnp.concatenate(ms, axis=1)
            lo = g * gw + pr * LANES
            xp = xdt_b[:, lo:lo + LANES]
            zero = jnp.zeros_like(xp)
            rhs = jnp.concatenate([jnp.where(lower_half, xp, zero),
                                   jnp.where(lower_half, zero, xp)], axis=0)
            y_ref[:, lo:lo + LANES] = _dot(lhs, rhs) + y_off[:, pr * LANES:(pr + 1) * LANES]
        h_ref[g] = h_g * ea_f[L - 1:L, g * gw:(g + 1) * gw] + new_state

    y = y_ref[...] + xs * dskip_ref[...]
    zz = z_ref[...]
    u = y * (zz * jax.nn.sigmoid(zz))
    for g in range(SSD_GROUPS):
        ug = u[:, g * gw:(g + 1) * gw]
        ms_ = jnp.mean(ug * ug, axis=-1, keepdims=True)
        o_ref[:, g * gw:(g + 1) * gw] = (
            ug * lax.rsqrt(ms_ + RMS_EPS) * nw_ref[:, g * gw:(g + 1) * gw]).astype(o_ref.dtype)


def _ssd_mixer(proj, bsz, t, cwx, cbx, cwb, cbb, dtb, alog, dskip, nw, expand):
    nc = t // CHUNK
    rows = lambda b, c: b * nc + c
    const = lambda shape: pl.BlockSpec(shape, lambda b, c: (0,) * len(shape))
    return pl.pallas_call(
        _ssd_body,
        grid=(bsz, nc),
        in_specs=[
            pl.BlockSpec((CHUNK, COLBLK), lambda b, c: (rows(b, c), BLK_Z)),
            pl.BlockSpec((CHUNK, COLBLK), lambda b, c: (rows(b, c), BLK_XS)),
            pl.BlockSpec((CHUNK, SMALLBLK), lambda b, c: (rows(b, c), BLK_BC)),
            pl.BlockSpec((CHUNK, SMALLBLK), lambda b, c: (rows(b, c), BLK_MISC)),
            const(cwx.shape), const(cbx.shape), const(cwb.shape), const(cbb.shape),
            const(dtb.shape), const(alog.shape), const(dskip.shape), const(nw.shape),
            const(expand.shape),
        ],
        out_specs=pl.BlockSpec((CHUNK, SSD_WIDTH), lambda b, c: (rows(b, c), 0)),
        out_shape=jax.ShapeDtypeStruct((bsz * t, SSD_WIDTH), BF16),
        scratch_shapes=[
            pltpu.VMEM((HALO + CHUNK, SSD_WIDTH), F32),
            pltpu.VMEM((HALO + CHUNK, SMALLBLK), F32),
            pltpu.VMEM((SSD_GROUPS, SSD_STATE, SSD_WIDTH // SSD_GROUPS), F32),
            pltpu.VMEM((CHUNK, SSD_WIDTH), F32),
        ],
        compiler_params=_params(("parallel", "arbitrary")),
        name="ssd_mixer",
    )(proj, proj, proj, proj, cwx, cbx, cwb, cbb, dtb, alog, dskip, nw, expand)


PAIRS = RWKV_HEADS // 2
MXU_WIDTH = 256
ST_AMID, ST_RMID, ST_BMID, ST_KMID, ST_AST, ST_RST, ST_BEND, ST_KEND, ST_V, ST_N = range(10)


def _split2(x):
    hi = x.astype(BF16)
    return hi, (x - hi.astype(F32)).astype(BF16)


def _split3(x):
    hi = x.astype(BF16)
    r1 = x - hi.astype(F32)
    mid = r1.astype(BF16)
    return hi, mid, (r1 - mid.astype(F32)).astype(BF16)


def _rwkv_body(has_vres, *refs):
    if has_vres:
        (r_ref, k_ref, v_ref, misc_ref, vfirst_ref, mix_ref, mixm_ref, w0_ref, wuph_ref, wupl_ref,
         a0_ref, aup_ref, gup_ref, v0_ref, vup_ref, kk_ref, ka_ref, rk_ref, lnw_ref, lnb_ref, bd_ref,
         o_ref, r_buf, k_buf, v_buf, m_buf, st_ref, h_ref, y_ref) = refs
    else:
        (r_ref, k_ref, v_ref, misc_ref, mix_ref, mixm_ref, w0_ref, wuph_ref, wupl_ref,
         a0_ref, aup_ref, gup_ref, kk_ref, ka_ref, rk_ref, lnw_ref, lnb_ref, bd_ref,
         o_ref, vfirst_out_ref, r_buf, k_buf, v_buf, m_buf, st_ref, h_ref, y_ref) = refs
    L = CHUNK
    c = pl.program_id(1)

    @pl.when(c == 0)
    def _init():
        for buf in (r_buf, k_buf, v_buf, m_buf):
            buf[0:HALO, :] = jnp.zeros((HALO, buf.shape[1]), F32)
        h_ref[...] = jnp.zeros_like(h_ref)

    def shift_lerp(ref, buf, mix):
        cur = ref[...]
        buf[HALO:HALO + L, :] = cur
        prev = buf[pl.ds(HALO - 1, L), :]
        buf[0:HALO, :] = buf[L:L + HALO, :]
        return cur + (prev - cur) * mix

    w_ = RWKV_WIDTH
    r = shift_lerp(r_ref, r_buf, mix_ref[:, 0:w_])
    k = shift_lerp(k_ref, k_buf, mix_ref[:, w_:2 * w_])
    v = shift_lerp(v_ref, v_buf, mix_ref[:, 2 * w_:3 * w_])
    m = shift_lerp(misc_ref, m_buf, mixm_ref[...])
    wa = m[:, MISC_WA:MISC_WA + LANES]
    gv = m[:, MISC_GV:MISC_GV + 2 * LANES]

    th_hi, th_lo = _split2(jnp.tanh(wa))
    wup_hi, wup_lo = wuph_ref[...], wupl_ref[...]
    w_lin = w0_ref[...] + (_dot(th_hi, wup_hi) + _dot(th_hi, wup_lo) + _dot(th_lo, wup_hi))
    log_w = -jnp.exp(-_softplus(-w_lin) - 0.5)
    a = jax.nn.sigmoid(a0_ref[...] + _dot(_bf(wa), aup_ref[...]))
    g = _dot(_bf(jax.nn.sigmoid(gv)), gup_ref[...])
    if has_vres:
        v = v + (vfirst_ref[...] - v) * jax.nn.sigmoid(v0_ref[...] + _dot(_bf(gv), vup_ref[...]))
    else:
        vfirst_out_ref[...] = v

    n_blk = RWKV_WIDTH // MXU_WIDTH

    def head_sum(x):
        parts = [t[:, j * MXU_WIDTH:(j + 1) * MXU_WIDTH] for t in _split2(x) for j in range(n_blk)]
        s = _dot(jnp.concatenate(parts, axis=0), bd_ref[...])
        return jnp.concatenate(
            [s[j * L:(j + 1) * L] + s[(n_blk + j) * L:(n_blk + j + 1) * L] for j in range(n_blk)], axis=1)

    kx = k * kk_ref[...]
    kk = kx / jnp.maximum(jnp.sqrt(head_sum(kx * kx)), L2_EPS)
    k2 = k * (1.0 + (a - 1.0) * ka_ref[...])
    av = -kk
    bv = kk * a

    row = lax.broadcasted_iota(jnp.int32, (L, L), 0)
    col = lax.broadcasted_iota(jnp.int32, (L, L), 1)
    incl = row >= col
    strict = row > col
    tri = incl.astype(BF16)
    lw_hi, lw_mid, lw_lo = _split3(log_w)
    cum = _dot(tri, lw_hi) + _dot(tri, lw_mid) + _dot(tri, lw_lo)
    cum_prev = cum - log_w
    tot = cum[L - 1:L, :]
    mid = cum[L // 2 - 1:L // 2, :]
    st_ref[ST_AMID] = av * jnp.exp(cum_prev - mid)
    st_ref[ST_RMID] = r * jnp.exp(cum - mid)
    inv_mid = jnp.exp(mid - cum)
    st_ref[ST_BMID] = bv * inv_mid
    st_ref[ST_KMID] = k2 * inv_mid
    st_ref[ST_AST] = av * jnp.exp(cum_prev)
    st_ref[ST_RST] = r * jnp.exp(cum)
    to_end = jnp.exp(tot - cum)
    st_ref[ST_BEND] = bv * to_end
    st_ref[ST_KEND] = k2 * to_end
    st_ref[ST_V] = v
    e_tot_all = jnp.exp(tot)

    lane_l = lax.broadcasted_iota(jnp.int32, (L, LANES), 1)
    lower = lane_l < HEAD_DIM
    r2 = lax.broadcasted_iota(jnp.int32, (LANES, LANES), 0)
    c2 = lax.broadcasted_iota(jnp.int32, (LANES, LANES), 1)
    same_head = (r2 < HEAD_DIM) == (c2 < HEAD_DIM)
    diag = r2 == c2
    n_rounds = L.bit_length() - 1

    lanes_of = [slice(p * LANES, (p + 1) * LANES) for p in range(PAIRS)]
    masks = (lower, jnp.logical_not(lower))
    heads = [(p, h) for p in range(PAIRS) for h in range(2)]
    v_b, v_sw, bk_t = [], [], []
    for sl in lanes_of:
        v_p = st_ref[ST_V, :, sl]
        v_b.append(_bf(v_p))
        v_sw.append(_bf(pltpu.roll(v_p, HEAD_DIM, 1)))
        bk_t.append(_bf(jnp.concatenate([st_ref[ST_BMID, :, sl].T, st_ref[ST_KMID, :, sl].T], axis=1)))
    zero_b = jnp.zeros_like(v_sw[0])

    mm, a_ak, a_rbk = [], [], []
    for p, h in heads:
        sl, mh = lanes_of[p], masks[h]
        ar = _bf(jnp.concatenate([jnp.where(mh, st_ref[ST_AMID, :, sl], 0.0),
                                  jnp.where(mh, st_ref[ST_RMID, :, sl], 0.0)], axis=0))
        aa = _dot(ar, bk_t[p])
        mm.append(_bf(jnp.where(strict, aa[0:L, 0:L], 0.0)))
        a_ak.append(_bf(jnp.where(strict, aa[0:L, L:2 * L], 0.0)))
        a_rbk.append(_bf(jnp.concatenate([jnp.where(incl, aa[L:2 * L, 0:L], 0.0),
                                          jnp.where(incl, aa[L:2 * L, L:2 * L], 0.0)], axis=1)))
    xs_ = [jnp.where(masks[h], st_ref[ST_AST, :, lanes_of[p]], _dot(a_ak[i], v_sw[p]))
           for i, (p, h) in enumerate(heads)]
    for rnd in range(n_rounds):
        xs_ = [x + _dot(m_, _bf(x)) for x, m_ in zip(xs_, mm)]
        if rnd + 1 < n_rounds:
            mm = [_bf(_dot(m_, m_)) for m_ in mm]
    yz = [_dot(a_rbk[i], jnp.concatenate([_bf(xs_[i]), jnp.where(masks[h], zero_b, v_sw[p])], axis=0))
          for i, (p, h) in enumerate(heads)]

    for p, sl in enumerate(lanes_of):
        x0, x1, yz0, yz1 = xs_[2 * p], xs_[2 * p + 1], yz[2 * p], yz[2 * p + 1]
        a_bar = jnp.where(lower, x0, x1)
        uv = pltpu.roll(jnp.where(lower, x1, x0), HEAD_DIM, 1)
        r_bar = st_ref[ST_RST, :, sl] + jnp.where(lower, yz0, yz1)
        yv = pltpu.roll(jnp.where(lower, yz1, yz0), HEAD_DIM, 1)
        b_end_t = _bf(st_ref[ST_BEND, :, sl].T)
        k_end_t = _bf(st_ref[ST_KEND, :, sl].T)
        g_mat = (jnp.where(same_head, _dot(b_end_t, _bf(a_bar)), 0.0)
                 + jnp.where(diag, e_tot_all[:, sl], 0.0))
        h_add = jnp.where(same_head, _dot(b_end_t, _bf(uv)) + _dot(k_end_t, v_b[p]), 0.0)
        h_old = _bf(h_ref[p])
        y_ref[:, sl] = _dot(_bf(r_bar), h_old) + yv
        h_ref[p] = _dot(_bf(g_mat), h_old) + h_add

    y = y_ref[...]
    inv_n = 1.0 / HEAD_DIM
    mu = head_sum(y) * inv_n
    d = y - mu
    var = head_sum(d * d) * inv_n
    y = d * lax.rsqrt(var + RWKV_GN_EPS) * lnw_ref[...] + lnb_ref[...]
    y = y + head_sum(r * k2 * rk_ref[...]) * v
    o_ref[...] = (y * g).astype(o_ref.dtype)


def _rwkv_mixer(proj, bsz, t, vfirst, mix, mixm, w0, wup_hi, wup_lo, a0, aup, gup, v0, vup, k_k, k_a,
                r_k, ln_w, ln_b, bd):
    nc = t // CHUNK
    rows = lambda b, c: b * nc + c
    const = lambda x: pl.BlockSpec(x.shape, lambda b, c: (0,) * x.ndim)
    has_vres = vfirst is not None
    act_specs = [
        pl.BlockSpec((CHUNK, COLBLK), lambda b, c: (rows(b, c), BLK_R)),
        pl.BlockSpec((CHUNK, COLBLK), lambda b, c: (rows(b, c), BLK_K)),
        pl.BlockSpec((CHUNK, COLBLK), lambda b, c: (rows(b, c), BLK_V)),
        pl.BlockSpec((CHUNK, SMALLBLK), lambda b, c: (rows(b, c), BLK_MISC)),
    ]
    wide_spec = pl.BlockSpec((CHUNK, RWKV_WIDTH), lambda b, c: (rows(b, c), 0))
    if has_vres:
        args = [proj, proj, proj, proj, vfirst, mix, mixm, w0, wup_hi, wup_lo, a0, aup, gup, v0, vup,
                k_k, k_a, r_k, ln_w, ln_b, bd]
        in_specs = act_specs + [wide_spec] + [const(x) for x in args[5:]]
        out_specs = wide_spec
        out_shape = jax.ShapeDtypeStruct((bsz * t, RWKV_WIDTH), BF16)
    else:
        args = [proj, proj, proj, proj, mix, mixm, w0, wup_hi, wup_lo, a0, aup, gup,
                k_k, k_a, r_k, ln_w, ln_b, bd]
        in_specs = act_specs + [const(x) for x in args[4:]]
        out_specs = (wide_spec, wide_spec)
        out_shape = (jax.ShapeDtypeStruct((bsz * t, RWKV_WIDTH), BF16),
                     jax.ShapeDtypeStruct((bsz * t, RWKV_WIDTH), F32))
    return pl.pallas_call(
        functools.partial(_rwkv_body, has_vres),
        grid=(bsz, nc),
        in_specs=in_specs,
        out_specs=out_specs,
        out_shape=out_shape,
        scratch_shapes=[
            pltpu.VMEM((HALO + CHUNK, RWKV_WIDTH), F32),
            pltpu.VMEM((HALO + CHUNK, RWKV_WIDTH), F32),
            pltpu.VMEM((HALO + CHUNK, RWKV_WIDTH), F32),
            pltpu.VMEM((HALO + CHUNK, SMALLBLK), F32),
            pltpu.VMEM((ST_N, CHUNK, RWKV_WIDTH), F32),
            pltpu.VMEM((PAIRS, LANES, LANES), F32),
            pltpu.VMEM((CHUNK, RWKV_WIDTH), F32),
        ],
        compiler_params=_params(("parallel", "arbitrary")),
        name="rwkv_mixer",
    )(*args)


TOK_ROWS = D_MODEL // LANES


def _store_token_major(ref, val):
    bm = val.shape[0]
    for s in range(TOK_ROWS):
        ref[pl.ds(s, bm, stride=TOK_ROWS), :] = val[:, s * LANES:(s + 1) * LANES]


def _load_token_major(ref, first_row, bm):
    return [ref[pl.ds(first_row + s, bm, stride=TOK_ROWS), :] for s in range(TOK_ROWS)]


def _outproj_ln_body(emit_rows, ys_ref, yr_ref, wt_ref, wb_ref, x_ref, g_ref, b_ref, of_ref, ob_ref,
                     *rows_ref):
    mixed = _dot(ys_ref[...], wt_ref[...]) + _dot(yr_ref[...], wb_ref[...])
    out = _layer_norm(DEEPNORM_ALPHA * x_ref[...] + mixed, g_ref[...], b_ref[...])
    of_ref[...] = out
    ob_ref[...] = out.astype(BF16)
    if emit_rows:
        _store_token_major(rows_ref[0], out)


def _outproj_ln(ys, yr, w_out, x, g, b, bm, emit_rows):
    n = x.shape[0]
    half = w_out.shape[0] // 2
    row = lambda width: pl.BlockSpec((bm, width), lambda i: (i, 0))
    vec = pl.BlockSpec((1, D_MODEL), lambda i: (0, 0))
    out_specs = [row(D_MODEL), row(D_MODEL)]
    out_shape = [jax.ShapeDtypeStruct((n, D_MODEL), F32), jax.ShapeDtypeStruct((n, D_MODEL), BF16)]
    if emit_rows:
        out_specs.append(pl.BlockSpec((bm * TOK_ROWS, LANES), lambda i: (i, 0)))
        out_shape.append(jax.ShapeDtypeStruct((n * TOK_ROWS, LANES), F32))
    return pl.pallas_call(
        functools.partial(_outproj_ln_body, emit_rows),
        grid=(n // bm,),
        in_specs=[row(half), row(half),
                  pl.BlockSpec((half, D_MODEL), lambda i: (0, 0)),
                  pl.BlockSpec((half, D_MODEL), lambda i: (1, 0)),
                  row(D_MODEL), vec, vec],
        out_specs=tuple(out_specs),
        out_shape=tuple(out_shape),
        compiler_params=_params(("parallel",)),
        name="out_proj_ln",
    )(ys, yr, w_out, w_out, x, g, b)


def _down_ln_body(h_ref, w_ref, x_ref, g_ref, b_ref, of_ref, ob_ref, acc_ref):
    kk = pl.program_id(1)

    @pl.when(kk == 0)
    def _zero():
        acc_ref[...] = jnp.zeros_like(acc_ref)

    acc_ref[...] += _dot(h_ref[...], w_ref[...])

    @pl.when(kk == pl.num_programs(1) - 1)
    def _finish():
        out = _layer_norm(DEEPNORM_ALPHA * x_ref[...] + acc_ref[...], g_ref[...], b_ref[...])
        of_ref[...] = out
        ob_ref[...] = out.astype(BF16)


def _down_ln(h, w, x, g, b, bm, bk):
    n, kdim = h.shape
    row = pl.BlockSpec((bm, D_MODEL), lambda i, k: (i, 0))
    vec = pl.BlockSpec((1, D_MODEL), lambda i, k: (0, 0))
    return pl.pallas_call(
        _down_ln_body,
        grid=(n // bm, kdim // bk),
        in_specs=[pl.BlockSpec((bm, bk), lambda i, k: (i, k)),
                  pl.BlockSpec((bk, D_MODEL), lambda i, k: (k, 0)),
                  row, vec, vec],
        out_specs=(row, row),
        out_shape=(jax.ShapeDtypeStruct((n, D_MODEL), F32),
                   jax.ShapeDtypeStruct((n, D_MODEL), BF16)),
        scratch_shapes=[pltpu.VMEM((bm, D_MODEL), F32)],
        compiler_params=_params(("parallel", "arbitrary")),
        name="down_proj_ln",
    )(h, w, x, g, b)


def _swiglu_up_body(x_ref, w1_ref, w3_ref, o_ref):
    xb = x_ref[...]
    h1 = _dot(xb, w1_ref[...])
    h3 = _dot(xb, w3_ref[...])
    o_ref[...] = (h1 * jax.nn.sigmoid(h1) * h3).astype(o_ref.dtype)


def _swiglu_up(xb, w1, w3, bm, bn):
    n = xb.shape[0]
    f = w1.shape[1]
    return pl.pallas_call(
        _swiglu_up_body,
        grid=(n // bm, f // bn),
        in_specs=[pl.BlockSpec((bm, D_MODEL), lambda i, j: (i, 0)),
                  pl.BlockSpec((D_MODEL, bn), lambda i, j: (0, j)),
                  pl.BlockSpec((D_MODEL, bn), lambda i, j: (0, j))],
        out_specs=pl.BlockSpec((bm, bn), lambda i, j: (i, j)),
        out_shape=jax.ShapeDtypeStruct((n, f), BF16),
        compiler_params=_params(("parallel", "parallel")),
        name="swiglu_up",
    )(xb, w1, w3)


def _router_body(x_ref, wr_ref, o_ref):
    logits = _dot(x_ref[...], wr_ref[...], HIGHEST)
    lane = lax.broadcasted_iota(jnp.int32, logits.shape, 1)
    neg = jnp.float32(-jnp.inf)
    logits = jnp.where(lane < N_EXPERTS, logits, neg)
    top1 = jnp.max(logits, axis=-1, keepdims=True)
    idx1 = jnp.min(jnp.where(logits == top1, lane, LANES), axis=-1, keepdims=True)
    rest = jnp.where(lane == idx1, neg, logits)
    top2 = jnp.max(rest, axis=-1, keepdims=True)
    idx2 = jnp.min(jnp.where(rest == top2, lane, LANES), axis=-1, keepdims=True)
    e2 = jnp.exp(top2 - top1)
    gate1 = 1.0 / (1.0 + e2)
    gate2 = e2 / (1.0 + e2)
    o_ref[...] = (jnp.where(lane == 0, idx1.astype(F32), 0.0) + jnp.where(lane == 1, idx2.astype(F32), 0.0)
                  + jnp.where(lane == 2, gate1, 0.0) + jnp.where(lane == 3, gate2, 0.0))


def _router(x, wr, bm):
    n = x.shape[0]
    return pl.pallas_call(
        _router_body,
        grid=(n // bm,),
        in_specs=[pl.BlockSpec((bm, D_MODEL), lambda i: (i, 0)),
                  pl.BlockSpec((D_MODEL, LANES), lambda i: (0, 0))],
        out_specs=pl.BlockSpec((bm, LANES), lambda i: (i, 0)),
        out_shape=jax.ShapeDtypeStruct((n, LANES), F32),
        compiler_params=_params(("parallel",)),
        name="moe_router",
    )(x, wr)


MOE_BM = 512


def _row_copy(src_hbm, src_tok, dst_buf, dst_tok, sem):
    return pltpu.make_async_copy(
        src_hbm.at[pl.ds(pl.multiple_of(src_tok * TOK_ROWS, TOK_ROWS), TOK_ROWS), :],
        dst_buf.at[pl.ds(pl.multiple_of(dst_tok * TOK_ROWS, TOK_ROWS), TOK_ROWS), :], sem)


DISPATCH_BM = 512
DISPATCH_UNROLL = 4


def _moe_dispatch_body(n_tok, bm, dest_ref, pad_lo_ref, pad_hi_ref, x_ref, o_hbm, zero_buf, sem):
    i = pl.program_id(0)
    base = i * bm

    def copies(j):
        return [_row_copy(x_ref, j, o_hbm, dest_ref[slot * n_tok + base + j], sem) for slot in range(2)]

    def issue(j, carry):
        for cp in copies(j):
            cp.start()
        return carry

    def wait(j, carry):
        for cp in copies(j):
            cp.wait()
        return carry

    lax.fori_loop(0, bm, issue, 0, unroll=DISPATCH_UNROLL)
    lax.fori_loop(0, bm, wait, 0, unroll=DISPATCH_UNROLL)

    @pl.when(i == 0)
    def _zero_padding_rows():
        zero_buf[...] = jnp.zeros_like(zero_buf)

        def pad_copy(r):
            return pltpu.make_async_copy(
                zero_buf, o_hbm.at[pl.ds(pl.multiple_of(r * TOK_ROWS, TOK_ROWS), TOK_ROWS), :], sem)

        def z_issue(r, carry):
            pad_copy(r).start()
            return carry

        def z_wait(r, carry):
            pad_copy(r).wait()
            return carry

        for e in range(N_EXPERTS + 1):
            lax.fori_loop(pad_lo_ref[e], pad_hi_ref[e], z_issue, 0)
            lax.fori_loop(pad_lo_ref[e], pad_hi_ref[e], z_wait, 0)


def _moe_dispatch(dest, pad_lo, pad_hi, rows, n_tok, n_rows):
    bm = DISPATCH_BM if n_tok % DISPATCH_BM == 0 else n_tok
    return pl.pallas_call(
        functools.partial(_moe_dispatch_body, n_tok, bm),
        grid_spec=pltpu.PrefetchScalarGridSpec(
            num_scalar_prefetch=3,
            grid=(n_tok // bm,),
            in_specs=[pl.BlockSpec((bm * TOK_ROWS, LANES), lambda i, dest, lo, hi: (i, 0))],
            out_specs=pl.BlockSpec(memory_space=pl.ANY),
            scratch_shapes=[pltpu.VMEM((TOK_ROWS, LANES), F32), pltpu.SemaphoreType.DMA(())],
        ),
        out_shape=jax.ShapeDtypeStruct((n_rows * TOK_ROWS, LANES), F32),
        compiler_params=_params(("arbitrary",)),
        name="moe_dispatch",
    )(dest, pad_lo, pad_hi, rows)


def _moe_up_body(te_ref, valid_ref, x_ref, w1_ref, w3_ref, o_ref):
    i = pl.program_id(1)

    @pl.when(valid_ref[i] != 0)
    def _compute():
        xb = jnp.concatenate([_bf(p) for p in _load_token_major(x_ref, 0, MOE_BM)], axis=1)
        h1 = _dot(xb, w1_ref[...])
        h3 = _dot(xb, w3_ref[...])
        o_ref[...] = (h1 * jax.nn.sigmoid(h1) * h3).astype(o_ref.dtype)

    @pl.when(valid_ref[i] == 0)
    def _empty():
        o_ref[...] = jnp.zeros_like(o_ref)


def _moe_up(te, valid, xs_rows, w1, w3, layer, bn):
    n_rows = xs_rows.shape[0] // TOK_ROWS
    fe = w1.shape[3]
    wspec = pl.BlockSpec((None, None, D_MODEL, bn), lambda j, i, te, valid: (layer, te[i], 0, j))
    return pl.pallas_call(
        _moe_up_body,
        grid_spec=pltpu.PrefetchScalarGridSpec(
            num_scalar_prefetch=2,
            grid=(fe // bn, n_rows // MOE_BM),
            in_specs=[pl.BlockSpec((MOE_BM * TOK_ROWS, LANES), lambda j, i, te, valid: (i, 0)),
                      wspec, wspec],
            out_specs=pl.BlockSpec((MOE_BM, bn), lambda j, i, te, valid: (i, j)),
        ),
        out_shape=jax.ShapeDtypeStruct((n_rows, fe), BF16),
        compiler_params=_params(("arbitrary", "arbitrary")),
        name="moe_up",
    )(te, valid, xs_rows, w1, w3)


def _moe_down_body(te_ref, valid_ref, h_ref, w_ref, o_ref):
    i = pl.program_id(0)

    @pl.when(valid_ref[i] != 0)
    def _compute():
        _store_token_major(o_ref, _dot(h_ref[...], w_ref[...]))

    @pl.when(valid_ref[i] == 0)
    def _empty():
        o_ref[...] = jnp.zeros_like(o_ref)


def _moe_down(te, valid, h, w2, layer):
    n_rows, fe = h.shape
    return pl.pallas_call(
        _moe_down_body,
        grid_spec=pltpu.PrefetchScalarGridSpec(
            num_scalar_prefetch=2,
            grid=(n_rows // MOE_BM,),
            in_specs=[pl.BlockSpec((MOE_BM, fe), lambda i, te, valid: (i, 0)),
                      pl.BlockSpec((None, None, fe, D_MODEL), lambda i, te, valid: (layer, te[i], 0, 0))],
            out_specs=pl.BlockSpec((MOE_BM * TOK_ROWS, LANES), lambda i, te, valid: (i, 0)),
        ),
        out_shape=jax.ShapeDtypeStruct((n_rows * TOK_ROWS, LANES), F32),
        compiler_params=_params(("arbitrary",)),
        name="moe_down",
    )(te, valid, h, w2)


def _moe_combine_ln_body(n_tok, dest_ref, ys_hbm, route_ref, x_ref, g_ref, b_ref, of_ref, ob_ref, buf,
                         sem):
    bm = x_ref.shape[0]
    base = pl.program_id(0) * bm

    def copies(j):
        return [_row_copy(ys_hbm, dest_ref[slot * n_tok + base + j], buf, slot * bm + j, sem)
                for slot in range(2)]

    def issue(j, carry):
        for cp in copies(j):
            cp.start()
        return carry

    def wait(j, carry):
        for cp in copies(j):
            cp.wait()
        return carry

    lax.fori_loop(0, bm, issue, 0, unroll=DISPATCH_UNROLL)
    lax.fori_loop(0, bm, wait, 0, unroll=DISPATCH_UNROLL)
    route = route_ref[...]
    gate1, gate2 = route[:, 2:3], route[:, 3:4]
    first = _load_token_major(buf, 0, bm)
    second = _load_token_major(buf, bm * TOK_ROWS, bm)
    f = jnp.concatenate([gate1 * p + gate2 * q for p, q in zip(first, second)], axis=1)
    out = _layer_norm(DEEPNORM_ALPHA * x_ref[...] + f, g_ref[...], b_ref[...])
    of_ref[...] = out
    ob_ref[...] = out.astype(BF16)


def _moe_combine_ln(dest, ys, route, x, g, b, bm):
    n = x.shape[0]
    row = pl.BlockSpec((bm, D_MODEL), lambda i, dest: (i, 0))
    vec = pl.BlockSpec((1, D_MODEL), lambda i, dest: (0, 0))
    route_spec = pl.BlockSpec((bm, LANES), lambda i, dest: (i, 0))
    return pl.pallas_call(
        functools.partial(_moe_combine_ln_body, n),
        grid_spec=pltpu.PrefetchScalarGridSpec(
            num_scalar_prefetch=1,
            grid=(n // bm,),
            in_specs=[pl.BlockSpec(memory_space=pl.ANY), route_spec, row, vec, vec],
            out_specs=(row, row),
            scratch_shapes=[pltpu.VMEM((2 * bm * TOK_ROWS, LANES), F32), pltpu.SemaphoreType.DMA(())],
        ),
        out_shape=(jax.ShapeDtypeStruct((n, D_MODEL), F32), jax.ShapeDtypeStruct((n, D_MODEL), BF16)),
        compiler_params=_params(("arbitrary",)),
        name="moe_combine_ln",
    )(dest, ys, route, x, g, b)


def _moe_dispatch_plan(route, n):
    n_rows = 2 * n + N_EXPERTS * MOE_BM
    expert = route[:, 0:2].astype(jnp.int32).T.reshape(-1)
    onehot = (expert[:, None] == jnp.arange(N_EXPERTS)[None, :]).astype(jnp.int32)
    csum = jnp.cumsum(onehot, axis=0)
    rank = jnp.sum(csum * onehot, axis=1) - 1
    count = csum[-1]
    padded = (count + MOE_BM - 1) // MOE_BM * MOE_BM
    ends = jnp.cumsum(padded)
    starts = ends - padded
    dest = starts[expert] + rank
    tile_start = jnp.arange(n_rows // MOE_BM, dtype=jnp.int32) * MOE_BM
    te = jnp.minimum(jnp.sum((tile_start[:, None] >= ends[None, :]).astype(jnp.int32), axis=1),
                     N_EXPERTS - 1)
    valid = (tile_start < ends[-1]).astype(jnp.int32)
    i32 = lambda v: v.astype(jnp.int32)
    pad_lo = jnp.concatenate([starts + count, ends[-1:]])
    pad_hi = jnp.concatenate([ends, jnp.full((1,), n_rows, ends.dtype)])
    return i32(dest), i32(te), valid, i32(pad_lo), i32(pad_hi), n_rows


def _pad_lanes(v, width):
    v = v.reshape(1, -1)
    return jnp.pad(v, ((0, 0), (0, width - v.shape[1])))


def _pick(n, pref):
    return pref if n % pref == 0 else n


def kernel(x, w_in, w_in_vres, ssd_conv_w, ssd_conv_b, ssd_dt_bias, ssd_a_log, ssd_d, ssd_norm_w, rw_mix, rw_vres_mix, rw_w0, rw_w_up, rw_a0, rw_a_up, rw_v0, rw_v_up, rw_g_up, rw_k_k, rw_k_a, rw_r_k, rw_ln_w, rw_ln_b, w_out, ln1_g, ln1_b, ln2_g, ln2_b, ffn_w1, ffn_w3, ffn_w2, moe_router, moe_w1, moe_w3, moe_w2):
    bsz, t, d = x.shape
    n = bsz * t
    depth = w_in.shape[0]
    assert d == D_MODEL and t % CHUNK == 0
    xf = x.reshape(n, d)
    xb = xf.astype(BF16)

    head_of_lane = jnp.arange(RWKV_WIDTH) // HEAD_DIM
    expand = (jnp.arange(LANES)[:, None] == head_of_lane[None, :]).astype(F32)
    blk_head = jnp.arange(MXU_WIDTH) // HEAD_DIM
    head_ones = (blk_head[:, None] == blk_head[None, :]).astype(BF16)

    bm_big = _pick(n, 1024)
    bm_mid = _pick(n, 512)
    bm_small = _pick(n, 256)
    row = lambda v: v.reshape(1, -1)
    zcols = lambda k: jnp.zeros((d, k), F32)

    moe_w1_b, moe_w3_b, moe_w2_b = moe_w1.astype(BF16), moe_w3.astype(BF16), moe_w2.astype(BF16)

    vfirst = None
    for l in range(depth):
        w = w_in[l]
        o = 0
        segs = {}
        for name, width in (("z", SSD_WIDTH), ("xs", SSD_WIDTH), ("bc", 2 * SSD_GROUPS * SSD_STATE),
                            ("dt", SSD_HEADS), ("r", RWKV_WIDTH), ("k", RWKV_WIDTH), ("v", RWKV_WIDTH),
                            ("w_lo", W_LORA), ("a_lo", A_LORA), ("g_lo", G_LORA)):
            segs[name] = w[:, o:o + width]
            o += width
        v_lo_cols = w_in_vres[l - 1] if l > 0 else zcols(V_LORA)
        misc_cols = jnp.concatenate(
            [segs["dt"], zcols(MISC_WA - SSD_HEADS), segs["w_lo"], segs["a_lo"], segs["g_lo"], v_lo_cols,
             zcols(SMALLBLK - MISC_GV - G_LORA - V_LORA)], axis=1)
        w_comb = jnp.concatenate([segs["z"], segs["xs"], segs["r"], segs["k"], segs["v"], segs["bc"],
                                  misc_cols], axis=1).astype(BF16)

        proj = _matmul(xb, w_comb, bm_big, COLBLK, F32)

        cw, cb = ssd_conv_w[l], ssd_conv_b[l]
        y_ssd = _ssd_mixer(
            proj, bsz, t, cw[:, :SSD_WIDTH], row(cb[:SSD_WIDTH]), cw[:, SSD_WIDTH:], row(cb[SSD_WIDTH:]),
            _pad_lanes(ssd_dt_bias[l], LANES), _pad_lanes(ssd_a_log[l], LANES),
            row(jnp.repeat(ssd_d[l], HEAD_DIM)), row(ssd_norm_w[l]), expand.astype(BF16))

        mix = rw_mix[l]
        v_mix = rw_vres_mix[l - 1] if l > 0 else jnp.zeros((V_LORA,), F32)
        mix_misc = jnp.concatenate([jnp.zeros((MISC_WA,), F32), mix[3 * RWKV_WIDTH:], v_mix,
                                    jnp.zeros((SMALLBLK - MISC_GV - G_LORA - V_LORA,), F32)])
        wup = jnp.pad(rw_w_up[l], ((0, LANES - W_LORA), (0, 0)))
        wup_hi = wup.astype(BF16)
        wup_lo = (wup - wup_hi.astype(F32)).astype(BF16)
        aup = jnp.pad(rw_a_up[l], ((W_LORA, 0), (0, 0))).astype(BF16)
        gup = jnp.pad(rw_g_up[l], ((0, 2 * LANES - G_LORA), (0, 0))).astype(BF16)
        if l > 0:
            v0 = row(rw_v0[l - 1])
            vup = jnp.pad(rw_v_up[l - 1], ((G_LORA, 2 * LANES - G_LORA - V_LORA), (0, 0))).astype(BF16)
        else:
            v0 = vup = None
        res = _rwkv_mixer(
            proj, bsz, t, vfirst, row(mix[:3 * RWKV_WIDTH]), row(mix_misc), row(rw_w0[l]), wup_hi, wup_lo,
            row(rw_a0[l]), aup, gup, v0, vup, row(rw_k_k[l]), row(rw_k_a[l]), row(rw_r_k[l]),
            row(rw_ln_w[l]), row(rw_ln_b[l]), head_ones)
        if l == 0:
            y_rw, vfirst = res
        else:
            y_rw = res

        is_moe = l % 2 == 1
        res = _outproj_ln(y_ssd, y_rw, w_out[l].astype(BF16), xf, row(ln1_g[l]), row(ln1_b[l]), bm_small,
                          emit_rows=is_moe)
        i = l // 2
        if not is_moe:
            xf, xb = res
            h = _swiglu_up(xb, ffn_w1[i].astype(BF16), ffn_w3[i].astype(BF16), bm_big, 512)
            xf, xb = _down_ln(h, ffn_w2[i].astype(BF16), xf, row(ln2_g[l]), row(ln2_b[l]), bm_mid, 1408)
        else:
            xf, xb, x_rows = res
            route = _router(xf, jnp.pad(moe_router[i], ((0, 0), (0, LANES - N_EXPERTS))), bm_mid)
            dest, te, valid, pad_lo, pad_hi, n_rows = _moe_dispatch_plan(route, n)
            xs_rows = _moe_dispatch(dest, pad_lo, pad_hi, x_rows, n, n_rows)
            h = _moe_up(te, valid, xs_rows, moe_w1_b, moe_w3_b, i, 1408)
            ys = _moe_down(te, valid, h, moe_w2_b, i)
            xf, xb = _moe_combine_ln(dest, ys, route, xf, row(ln2_g[l]), row(ln2_b[l]), bm_small)

    return xf.reshape(bsz, t, d)
```

```python
import functools

import jax
import jax.numpy as jnp
from jax import lax
from jax.experimental import pallas as pl
from jax.experimental.pallas import tpu as pltpu

F32 = jnp.float32
BF16 = jnp.bfloat16
HIGHEST = lax.Precision.HIGHEST

D_MODEL = 2048
HEAD_DIM = 64
SSD_WIDTH = 1024
RWKV_WIDTH = 1024
SSD_HEADS = SSD_WIDTH // HEAD_DIM
SSD_GROUPS = 2
SSD_STATE = 128
SSD_CONV_WIDTH = 4
RWKV_HEADS = RWKV_WIDTH // HEAD_DIM
W_LORA = 64
A_LORA = 64
V_LORA = 32
G_LORA = 160
N_EXPERTS = 8
DEPTH = 4
DEEPNORM_ALPHA = (2 * DEPTH) ** 0.25
LN_EPS = 1e-5
RMS_EPS = 1e-5
RWKV_GN_EPS = 64e-5
L2_EPS = 1e-12

LANES = 128
SUBLANES = 8
VMEM_LIMIT = 56 * 1024 * 1024

CHUNK = 128
HALO = SUBLANES

COLBLK = 1024
BLK_Z, BLK_XS, BLK_R, BLK_K, BLK_V = 0, 1, 2, 3, 4
SMALLBLK = 512
BLK_BC = 10
BLK_MISC = 11
PROJ_WIDTH = 6 * COLBLK
MISC_WA = 128
MISC_GV = 256


def _params(sem, vmem=VMEM_LIMIT):
    return pltpu.CompilerParams(dimension_semantics=sem, vmem_limit_bytes=vmem)


def _dot(a, b, precision=None):
    return jnp.dot(a, b, preferred_element_type=F32, precision=precision)


def _dot_nt(a, b):
    return lax.dot_general(a, b, (((1,), (1,)), ((), ())), preferred_element_type=F32)


def _bf(x):
    return x.astype(BF16)


def _softplus(x):
    return jnp.maximum(x, 0.0) + jnp.log1p(jnp.exp(-jnp.abs(x)))


def _layer_norm(y, g, b):
    mu = jnp.mean(y, axis=-1, keepdims=True)
    d = y - mu
    var = jnp.mean(d * d, axis=-1, keepdims=True)
    return d * lax.rsqrt(var + LN_EPS) * g + b


def _mm_body(x_ref, w_ref, o_ref):
    o_ref[...] = _dot(x_ref[...], w_ref[...]).astype(o_ref.dtype)


def _matmul(x, w, bm, bn, out_dtype):
    m, k = x.shape
    n = w.shape[1]
    return pl.pallas_call(
        _mm_body,
        grid=(m // bm, n // bn),
        in_specs=[pl.BlockSpec((bm, k), lambda i, j: (i, 0)),
                  pl.BlockSpec((k, bn), lambda i, j: (0, j))],
        out_specs=pl.BlockSpec((bm, bn), lambda i, j: (i, j)),
        out_shape=jax.ShapeDtypeStruct((m, n), out_dtype),
        compiler_params=_params(("parallel", "parallel")),
        name="in_proj",
    )(x, w)


def _ssd_body(z_ref, xs_ref, bc_ref, misc_ref, cwx_ref, cbx_ref, cwb_ref, cbb_ref, dtb_ref,
              alog_ref, dskip_ref, nw_ref, e_ref, o_ref, xs_buf, bc_buf, h_ref, y_ref):
    L = CHUNK
    c = pl.program_id(1)

    @pl.when(c == 0)
    def _init():
        xs_buf[0:HALO, :] = jnp.zeros((HALO, xs_buf.shape[1]), F32)
        bc_buf[0:HALO, :] = jnp.zeros((HALO, bc_buf.shape[1]), F32)
        h_ref[...] = jnp.zeros_like(h_ref)

    xs_buf[HALO:HALO + L, :] = xs_ref[...]
    bc_buf[HALO:HALO + L, :] = bc_ref[...]

    def conv_silu(buf, w_ref, b_ref):
        acc = b_ref[...]
        for i in range(SSD_CONV_WIDTH):
            acc = acc + buf[pl.ds(HALO - (SSD_CONV_WIDTH - 1) + i, L), :] * w_ref[i:i + 1, :]
        return acc * jax.nn.sigmoid(acc)

    xs = conv_silu(xs_buf, cwx_ref, cbx_ref)
    bcv = conv_silu(bc_buf, cwb_ref, cbb_ref)
    xs_buf[0:HALO, :] = xs_buf[L:L + HALO, :]
    bc_buf[0:HALO, :] = bc_buf[L:L + HALO, :]

    lane = lax.broadcasted_iota(jnp.int32, (1, LANES), 1)
    dt = _softplus(misc_ref[:, 0:LANES] + dtb_ref[...])
    a_neg = jnp.where(lane < SSD_HEADS, -jnp.exp(alog_ref[...]), 0.0)
    d_a = dt * a_neg
    row = lax.broadcasted_iota(jnp.int32, (L, L), 0)
    col = lax.broadcasted_iota(jnp.int32, (L, L), 1)
    causal = row >= col
    tri = causal.astype(BF16)
    a_cum = sum(_dot(tri, t) for t in _split3(d_a))
    a_cum_t = a_cum.T
    a_last = a_cum[L - 1:L, :]
    per_head = jnp.concatenate([dt, jnp.exp(a_last - a_cum), jnp.exp(a_cum)], axis=0)
    head_to_lanes = e_ref[...]
    full = sum(_dot(t, head_to_lanes) for t in _split3(per_head))
    dt_f, dte_f, ea_f = full[0:L], full[L:2 * L], full[2 * L:3 * L]
    xdt = xs * dt_f
    xdt_b = _bf(xdt)
    xdte_b = _bf(xdt * dte_f)
    lane_l = lax.broadcasted_iota(jnp.int32, (L, LANES), 1)
    lower_half = lane_l < HEAD_DIM
    gw = SSD_WIDTH // SSD_GROUPS
    heads_per_group = SSD_HEADS // SSD_GROUPS
    for g in range(SSD_GROUPS):
        b_g = bcv[:, g * SSD_STATE:(g + 1) * SSD_STATE]
        c_g = _bf(bcv[:, (SSD_GROUPS + g) * SSD_STATE:(SSD_GROUPS + g + 1) * SSD_STATE])
        cb = _dot_nt(c_g, _bf(b_g))
        h_g = h_ref[g]
        y_off = _dot(c_g, _bf(h_g)) * ea_f[:, g * gw:(g + 1) * gw]
        new_state = _dot(_bf(b_g.T), xdte_b[:, g * gw:(g + 1) * gw])
        for pr in range(heads_per_group // 2):
            ms = []
            for e in range(2):
                h = g * heads_per_group + pr * 2 + e
                seg = a_cum[:, h:h + 1] - a_cum_t[h:h + 1, :]
                dec = jnp.where(causal, jnp.exp(jnp.minimum(seg, 0.0)), 0.0)
                ms.append(_bf(cb * dec))
            lhs = jnp.concatenate(ms, axis=1)
            lo = g * gw + pr * LANES
            xp = xdt_b[:, lo:lo + LANES]
            zero = jnp.zeros_like(xp)
            rhs = jnp.concatenate([jnp.where(lower_half, xp, zero),
                                   jnp.where(lower_half, zero, xp)], axis=0)
            y_ref[:, lo:lo + LANES] = _dot(lhs, rhs) + y_off[:, pr * LANES:(pr + 1) * LANES]
        h_ref[g] = h_g * ea_f[L - 1:L, g * gw:(g + 1) * gw] + new_state

    y = y_ref[...] + xs * dskip_ref[...]
    zz = z_ref[...]
    u = y * (zz * jax.nn.sigmoid(zz))
    for g in range(SSD_GROUPS):
        ug = u[:, g * gw:(g + 1) * gw]
        ms_ = jnp.mean(ug * ug, axis=-1, keepdims=True)
        o_ref[:, g * gw:(g + 1) * gw] = (
            ug * lax.rsqrt(ms_ + RMS_EPS) * nw_ref[:, g * gw:(g + 1) * gw]).astype(o_ref.dtype)


def _ssd_mixer(proj, bsz, t, cwx, cbx, cwb, cbb, dtb, alog, dskip, nw, expand):
    nc = t // CHUNK
    rows = lambda b, c: b * nc + c
    const = lambda shape: pl.BlockSpec(shape, lambda b, c: (0,) * len(shape))
    return pl.pallas_call(
        _ssd_body,
        grid=(bsz, nc),
        in_specs=[
            pl.BlockSpec((CHUNK, COLBLK), lambda b, c: (rows(b, c), BLK_Z)),
            pl.BlockSpec((CHUNK, COLBLK), lambda b, c: (rows(b, c), BLK_XS)),
            pl.BlockSpec((CHUNK, SMALLBLK), lambda b, c: (rows(b, c), BLK_BC)),
            pl.BlockSpec((CHUNK, SMALLBLK), lambda b, c: (rows(b, c), BLK_MISC)),
            const(cwx.shape), const(cbx.shape), const(cwb.shape), const(cbb.shape),
            const(dtb.shape), const(alog.shape), const(dskip.shape), const(nw.shape),
            const(expand.shape),
        ],
        out_specs=pl.BlockSpec((CHUNK, SSD_WIDTH), lambda b, c: (rows(b, c), 0)),
        out_shape=jax.ShapeDtypeStruct((bsz * t, SSD_WIDTH), BF16),
        scratch_shapes=[
            pltpu.VMEM((HALO + CHUNK, SSD_WIDTH), F32),
            pltpu.VMEM((HALO + CHUNK, SMALLBLK), F32),
            pltpu.VMEM((SSD_GROUPS, SSD_STATE, SSD_WIDTH // SSD_GROUPS), F32),
            pltpu.VMEM((CHUNK, SSD_WIDTH), F32),
        ],
        compiler_params=_params(("parallel", "arbitrary")),
        name="ssd_mixer",
    )(proj, proj, proj, proj, cwx, cbx, cwb, cbb, dtb, alog, dskip, nw, expand)


PAIRS = RWKV_HEADS // 2
MXU_WIDTH = 256
ST_AMID, ST_RMID, ST_BMID, ST_KMID, ST_AST, ST_RST, ST_BEND, ST_KEND, ST_V, ST_N = range(10)


def _split2(x):
    hi = x.astype(BF16)
    return hi, (x - hi.astype(F32)).astype(BF16)


def _split3(x):
    hi = x.astype(BF16)
    r1 = x - hi.astype(F32)
    mid = r1.astype(BF16)
    return hi, mid, (r1 - mid.astype(F32)).astype(BF16)


def _rwkv_body(has_vres, *refs):
    if has_vres:
        (r_ref, k_ref, v_ref, misc_ref, vfirst_ref, mix_ref, mixm_ref, w0_ref, wup3_ref,
         a0_ref, aup_ref, gup_ref, v0_ref, vup_ref, kk_ref, ka_ref, rk_ref, lnw_ref, lnb_ref, bd_ref,
         o_ref, r_buf, k_buf, v_buf, m_buf, st_ref, h_ref, y_ref) = refs
    else:
        (r_ref, k_ref, v_ref, misc_ref, mix_ref, mixm_ref, w0_ref, wup3_ref,
         a0_ref, aup_ref, gup_ref, kk_ref, ka_ref, rk_ref, lnw_ref, lnb_ref, bd_ref,
         o_ref, vfirst_out_ref, r_buf, k_buf, v_buf, m_buf, st_ref, h_ref, y_ref) = refs
    L = CHUNK
    c = pl.program_id(1)

    @pl.when(c == 0)
    def _init():
        for buf in (r_buf, k_buf, v_buf, m_buf):
            buf[0:HALO, :] = jnp.zeros((HALO, buf.shape[1]), F32)
        h_ref[...] = jnp.zeros_like(h_ref)

    def shift_lerp(ref, buf, mix):
        cur = ref[...]
        buf[HALO:HALO + L, :] = cur
        prev = buf[pl.ds(HALO - 1, L), :]
        buf[0:HALO, :] = buf[L:L + HALO, :]
        return cur + (prev - cur) * mix

    w_ = RWKV_WIDTH
    r = shift_lerp(r_ref, r_buf, mix_ref[:, 0:w_])
    k = shift_lerp(k_ref, k_buf, mix_ref[:, w_:2 * w_])
    v = shift_lerp(v_ref, v_buf, mix_ref[:, 2 * w_:3 * w_])
    m = shift_lerp(misc_ref, m_buf, mixm_ref[...])
    wa = m[:, MISC_WA:MISC_WA + LANES]
    gv = m[:, MISC_GV:MISC_GV + 2 * LANES]

    th_hi, th_lo = _split2(jnp.tanh(wa))
    w_lin = w0_ref[...] + _dot(jnp.concatenate([th_hi, th_hi, th_lo], axis=1), wup3_ref[...])
    log_w = -jnp.exp(-_softplus(-w_lin) - 0.5)
    a = jax.nn.sigmoid(a0_ref[...] + _dot(_bf(wa), aup_ref[...]))
    g = _dot(_bf(jax.nn.sigmoid(gv)), gup_ref[...])
    if has_vres:
        v = v + (vfirst_ref[...] - v) * jax.nn.sigmoid(v0_ref[...] + _dot(_bf(gv), vup_ref[...]))
    else:
        vfirst_out_ref[...] = v

    n_blk = RWKV_WIDTH // MXU_WIDTH

    def head_sum(x):
        hi, lo = _split2(x)
        blocks = [jnp.concatenate([t[:, j * MXU_WIDTH:(j + 1) * MXU_WIDTH] for t in (hi, lo)], axis=1)
                  for j in range(n_blk)]
        s = _dot(jnp.concatenate(blocks, axis=0), bd_ref[...])
        return jnp.concatenate([s[j * L:(j + 1) * L] for j in range(n_blk)], axis=1)

    kx = k * kk_ref[...]
    kk = kx / jnp.maximum(jnp.sqrt(head_sum(kx * kx)), L2_EPS)
    k2 = k * (1.0 + (a - 1.0) * ka_ref[...])
    av = -kk
    bv = kk * a

    row = lax.broadcasted_iota(jnp.int32, (L, L), 0)
    col = lax.broadcasted_iota(jnp.int32, (L, L), 1)
    incl = row >= col
    strict = row > col
    tri = incl.astype(BF16)
    cum = _dot(jnp.concatenate([tri, tri, tri], axis=1), jnp.concatenate(_split3(log_w), axis=0))
    cum_prev = cum - log_w
    tot = cum[L - 1:L, :]
    mid = cum[L // 2 - 1:L // 2, :]
    st_ref[ST_AMID] = av * jnp.exp(cum_prev - mid)
    st_ref[ST_RMID] = r * jnp.exp(cum - mid)
    inv_mid = jnp.exp(mid - cum)
    st_ref[ST_BMID] = bv * inv_mid
    st_ref[ST_KMID] = k2 * inv_mid
    st_ref[ST_AST] = av * jnp.exp(cum_prev)
    st_ref[ST_RST] = r * jnp.exp(cum)
    to_end = jnp.exp(tot - cum)
    st_ref[ST_BEND] = bv * to_end
    st_ref[ST_KEND] = k2 * to_end
    st_ref[ST_V] = v
    e_tot_all = jnp.exp(tot)

    lane_l = lax.broadcasted_iota(jnp.int32, (L, LANES), 1)
    lower = lane_l < HEAD_DIM
    r2 = lax.broadcasted_iota(jnp.int32, (LANES, LANES), 0)
    c2 = lax.broadcasted_iota(jnp.int32, (LANES, LANES), 1)
    same_head = (r2 < HEAD_DIM) == (c2 < HEAD_DIM)
    diag = r2 == c2
    n_rounds = L.bit_length() - 1

    lanes_of = [slice(p * LANES, (p + 1) * LANES) for p in range(PAIRS)]
    masks = (lower, jnp.logical_not(lower))
    heads = [(p, h) for p in range(PAIRS) for h in range(2)]
    v_b, v_sw, bk_t = [], [], []
    for sl in lanes_of:
        v_p = st_ref[ST_V, :, sl]
        v_b.append(_bf(v_p))
        v_sw.append(_bf(pltpu.roll(v_p, HEAD_DIM, 1)))
        bk_t.append(_bf(jnp.concatenate([st_ref[ST_BMID, :, sl].T, st_ref[ST_KMID, :, sl].T], axis=1)))
    zero_b = jnp.zeros_like(v_sw[0])

    mm, a_ak, a_rbk = [], [], []
    for p, h in heads:
        sl, mh = lanes_of[p], masks[h]
        ar = _bf(jnp.concatenate([jnp.where(mh, st_ref[ST_AMID, :, sl], 0.0),
                                  jnp.where(mh, st_ref[ST_RMID, :, sl], 0.0)], axis=0))
        aa = _dot(ar, bk_t[p])
        mm.append(_bf(jnp.where(strict, aa[0:L, 0:L], 0.0)))
        a_ak.append(_bf(jnp.where(strict, aa[0:L, L:2 * L], 0.0)))
        a_rbk.append(_bf(jnp.concatenate([jnp.where(incl, aa[L:2 * L, 0:L], 0.0),
                                          jnp.where(incl, aa[L:2 * L, L:2 * L], 0.0)], axis=1)))
    xs_ = [jnp.where(masks[h], st_ref[ST_AST, :, lanes_of[p]], _dot(a_ak[i], v_sw[p]))
           for i, (p, h) in enumerate(heads)]
    for rnd in range(n_rounds - 1):
        prods = [_dot(m_, jnp.concatenate([m_, _bf(x)], axis=1)) for x, m_ in zip(xs_, mm)]
        xs_ = [x + pr[:, L:L + LANES] for x, pr in zip(xs_, prods)]
        mm = [_bf(pr[:, 0:L]) for pr in prods]
    xs_ = [x + _dot(m_, _bf(x)) for x, m_ in zip(xs_, mm)]
    yz = [_dot(a_rbk[i], jnp.concatenate([_bf(xs_[i]), jnp.where(masks[h], zero_b, v_sw[p])], axis=0))
          for i, (p, h) in enumerate(heads)]

    for p, sl in enumerate(lanes_of):
        x0, x1, yz0, yz1 = xs_[2 * p], xs_[2 * p + 1], yz[2 * p], yz[2 * p + 1]
        a_bar = jnp.where(lower, x0, x1)
        uv = pltpu.roll(jnp.where(lower, x1, x0), HEAD_DIM, 1)
        r_bar = st_ref[ST_RST, :, sl] + jnp.where(lower, yz0, yz1)
        yv = pltpu.roll(jnp.where(lower, yz1, yz0), HEAD_DIM, 1)
        b_end_t = _bf(st_ref[ST_BEND, :, sl].T)
        k_end_t = _bf(st_ref[ST_KEND, :, sl].T)
        g_mat = (jnp.where(same_head, _dot(b_end_t, _bf(a_bar)), 0.0)
                 + jnp.where(diag, e_tot_all[:, sl], 0.0))
        h_add = jnp.where(same_head, _dot(b_end_t, _bf(uv)) + _dot(k_end_t, v_b[p]), 0.0)
        h_old = _bf(h_ref[p])
        y_ref[:, sl] = _dot(_bf(r_bar), h_old) + yv
        h_ref[p] = _dot(_bf(g_mat), h_old) + h_add

    y = y_ref[...]
    inv_n = 1.0 / HEAD_DIM
    mu = head_sum(y) * inv_n
    d = y - mu
    var = head_sum(d * d) * inv_n
    y = d * lax.rsqrt(var + RWKV_GN_EPS) * lnw_ref[...] + lnb_ref[...]
    y = y + head_sum(r * k2 * rk_ref[...]) * v
    o_ref[...] = (y * g).astype(o_ref.dtype)


def _rwkv_mixer(proj, bsz, t, vfirst, mix, mixm, w0, wup3, a0, aup, gup, v0, vup, k_k, k_a,
                r_k, ln_w, ln_b, bd):
    nc = t // CHUNK
    rows = lambda b, c: b * nc + c
    const = lambda x: pl.BlockSpec(x.shape, lambda b, c: (0,) * x.ndim)
    has_vres = vfirst is not None
    act_specs = [
        pl.BlockSpec((CHUNK, COLBLK), lambda b, c: (rows(b, c), BLK_R)),
        pl.BlockSpec((CHUNK, COLBLK), lambda b, c: (rows(b, c), BLK_K)),
        pl.BlockSpec((CHUNK, COLBLK), lambda b, c: (rows(b, c), BLK_V)),
        pl.BlockSpec((CHUNK, SMALLBLK), lambda b, c: (rows(b, c), BLK_MISC)),
    ]
    wide_spec = pl.BlockSpec((CHUNK, RWKV_WIDTH), lambda b, c: (rows(b, c), 0))
    if has_vres:
        args = [proj, proj, proj, proj, vfirst, mix, mixm, w0, wup3, a0, aup, gup, v0, vup,
                k_k, k_a, r_k, ln_w, ln_b, bd]
        in_specs = act_specs + [wide_spec] + [const(x) for x in args[5:]]
        out_specs = wide_spec
        out_shape = jax.ShapeDtypeStruct((bsz * t, RWKV_WIDTH), BF16)
    else:
        args = [proj, proj, proj, proj, mix, mixm, w0, wup3, a0, aup, gup,
                k_k, k_a, r_k, ln_w, ln_b, bd]
        in_specs = act_specs + [const(x) for x in args[4:]]
        out_specs = (wide_spec, wide_spec)
        out_shape = (jax.ShapeDtypeStruct((bsz * t, RWKV_WIDTH), BF16),
                     jax.ShapeDtypeStruct((bsz * t, RWKV_WIDTH), F32))
    return pl.pallas_call(
        functools.partial(_rwkv_body, has_vres),
        grid=(bsz, nc),
        in_specs=in_specs,
        out_specs=out_specs,
        out_shape=out_shape,
        scratch_shapes=[
            pltpu.VMEM((HALO + CHUNK, RWKV_WIDTH), F32),
            pltpu.VMEM((HALO + CHUNK, RWKV_WIDTH), F32),
            pltpu.VMEM((HALO + CHUNK, RWKV_WIDTH), F32),
            pltpu.VMEM((HALO + CHUNK, SMALLBLK), F32),
            pltpu.VMEM((ST_N, CHUNK, RWKV_WIDTH), F32),
            pltpu.VMEM((PAIRS, LANES, LANES), F32),
            pltpu.VMEM((CHUNK, RWKV_WIDTH), F32),
        ],
        compiler_params=_params(("parallel", "arbitrary")),
        name="rwkv_mixer",
    )(*args)


TOK_ROWS = D_MODEL // LANES


def _store_token_major(ref, val):
    bm = val.shape[0]
    for s in range(TOK_ROWS):
        ref[pl.ds(s, bm, stride=TOK_ROWS), :] = val[:, s * LANES:(s + 1) * LANES]


def _load_token_major(ref, first_row, bm):
    return [ref[pl.ds(first_row + s, bm, stride=TOK_ROWS), :] for s in range(TOK_ROWS)]


def _outproj_ln_body(emit_rows, ys_ref, yr_ref, wt_ref, wb_ref, x_ref, g_ref, b_ref, of_ref, ob_ref,
                     *rows_ref):
    mixed = _dot(ys_ref[...], wt_ref[...]) + _dot(yr_ref[...], wb_ref[...])
    out = _layer_norm(DEEPNORM_ALPHA * x_ref[...] + mixed, g_ref[...], b_ref[...])
    of_ref[...] = out
    ob_ref[...] = out.astype(BF16)
    if emit_rows:
        _store_token_major(rows_ref[0], out)


def _outproj_ln(ys, yr, w_out, x, g, b, bm, emit_rows):
    n = x.shape[0]
    half = w_out.shape[0] // 2
    row = lambda width: pl.BlockSpec((bm, width), lambda i: (i, 0))
    vec = pl.BlockSpec((1, D_MODEL), lambda i: (0, 0))
    out_specs = [row(D_MODEL), row(D_MODEL)]
    out_shape = [jax.ShapeDtypeStruct((n, D_MODEL), F32), jax.ShapeDtypeStruct((n, D_MODEL), BF16)]
    if emit_rows:
        out_specs.append(pl.BlockSpec((bm * TOK_ROWS, LANES), lambda i: (i, 0)))
        out_shape.append(jax.ShapeDtypeStruct((n * TOK_ROWS, LANES), F32))
    return pl.pallas_call(
        functools.partial(_outproj_ln_body, emit_rows),
        grid=(n // bm,),
        in_specs=[row(half), row(half),
                  pl.BlockSpec((half, D_MODEL), lambda i: (0, 0)),
                  pl.BlockSpec((half, D_MODEL), lambda i: (1, 0)),
                  row(D_MODEL), vec, vec],
        out_specs=tuple(out_specs),
        out_shape=tuple(out_shape),
        compiler_params=_params(("parallel",)),
        name="out_proj_ln",
    )(ys, yr, w_out, w_out, x, g, b)


def _down_ln_body(h_ref, w_ref, x_ref, g_ref, b_ref, of_ref, ob_ref, acc_ref):
    kk = pl.program_id(1)

    @pl.when(kk == 0)
    def _zero():
        acc_ref[...] = jnp.zeros_like(acc_ref)

    acc_ref[...] += _dot(h_ref[...], w_ref[...])

    @pl.when(kk == pl.num_programs(1) - 1)
    def _finish():
        out = _layer_norm(DEEPNORM_ALPHA * x_ref[...] + acc_ref[...], g_ref[...], b_ref[...])
        of_ref[...] = out
        ob_ref[...] = out.astype(BF16)


def _down_ln(h, w, x, g, b, bm, bk):
    n, kdim = h.shape
    row = pl.BlockSpec((bm, D_MODEL), lambda i, k: (i, 0))
    vec = pl.BlockSpec((1, D_MODEL), lambda i, k: (0, 0))
    return pl.pallas_call(
        _down_ln_body,
        grid=(n // bm, kdim // bk),
        in_specs=[pl.BlockSpec((bm, bk), lambda i, k: (i, k)),
                  pl.BlockSpec((bk, D_MODEL), lambda i, k: (k, 0)),
                  row, vec, vec],
        out_specs=(row, row),
        out_shape=(jax.ShapeDtypeStruct((n, D_MODEL), F32),
                   jax.ShapeDtypeStruct((n, D_MODEL), BF16)),
        scratch_shapes=[pltpu.VMEM((bm, D_MODEL), F32)],
        compiler_params=_params(("parallel", "arbitrary")),
        name="down_proj_ln",
    )(h, w, x, g, b)


def _swiglu_up_body(x_ref, w1_ref, w3_ref, o_ref):
    xb = x_ref[...]
    h1 = _dot(xb, w1_ref[...])
    h3 = _dot(xb, w3_ref[...])
    o_ref[...] = (h1 * jax.nn.sigmoid(h1) * h3).astype(o_ref.dtype)


def _swiglu_up(xb, w1, w3, bm, bn):
    n = xb.shape[0]
    f = w1.shape[1]
    return pl.pallas_call(
        _swiglu_up_body,
        grid=(n // bm, f // bn),
        in_specs=[pl.BlockSpec((bm, D_MODEL), lambda i, j: (i, 0)),
                  pl.BlockSpec((D_MODEL, bn), lambda i, j: (0, j)),
                  pl.BlockSpec((D_MODEL, bn), lambda i, j: (0, j))],
        out_specs=pl.BlockSpec((bm, bn), lambda i, j: (i, j)),
        out_shape=jax.ShapeDtypeStruct((n, f), BF16),
        compiler_params=_params(("parallel", "parallel")),
        name="swiglu_up",
    )(xb, w1, w3)


def _router_body(x_ref, wr3_ref, o_ref, cnt_ref, run_ref):
    i = pl.program_id(0)

    @pl.when(i == 0)
    def _init():
        run_ref[...] = jnp.zeros_like(run_ref)

    x_hi, x_lo = _split2(x_ref[...])
    logits = _dot(jnp.concatenate([x_hi, x_hi, x_lo], axis=1), wr3_ref[...])
    lane = lax.broadcasted_iota(jnp.int32, logits.shape, 1)
    neg = jnp.float32(-jnp.inf)
    logits = jnp.where(lane < N_EXPERTS, logits, neg)
    top1 = jnp.max(logits, axis=-1, keepdims=True)
    idx1 = jnp.min(jnp.where(logits == top1, lane, LANES), axis=-1, keepdims=True)
    rest = jnp.where(lane == idx1, neg, logits)
    top2 = jnp.max(rest, axis=-1, keepdims=True)
    idx2 = jnp.min(jnp.where(rest == top2, lane, LANES), axis=-1, keepdims=True)
    e2 = jnp.exp(top2 - top1)
    gate1 = 1.0 / (1.0 + e2)
    gate2 = e2 / (1.0 + e2)
    first = (lane == idx1).astype(F32)
    second = (lane == idx2).astype(F32)
    both = first + second
    bm = both.shape[0]
    later = (lax.broadcasted_iota(jnp.int32, (bm, bm), 0) > lax.broadcasted_iota(jnp.int32, (bm, bm), 1))
    earlier = _dot(later.astype(BF16), _bf(both)) + run_ref[0:1, :]
    rank1 = jnp.sum(first * earlier, axis=-1, keepdims=True)
    rank2 = jnp.sum(second * earlier, axis=-1, keepdims=True)
    run_ref[...] = run_ref[...] + jnp.sum(both, axis=0, keepdims=True)
    cnt_ref[...] = run_ref[...]
    o_ref[...] = (jnp.where(lane == 0, idx1.astype(F32), 0.0) + jnp.where(lane == 1, idx2.astype(F32), 0.0)
                  + jnp.where(lane == 2, gate1, 0.0) + jnp.where(lane == 3, gate2, 0.0)
                  + jnp.where(lane == 4, rank1, 0.0) + jnp.where(lane == 5, rank2, 0.0))


def _router(x, wr3, bm):
    n = x.shape[0]
    return pl.pallas_call(
        _router_body,
        grid=(n // bm,),
        in_specs=[pl.BlockSpec((bm, D_MODEL), lambda i: (i, 0)),
                  pl.BlockSpec((3 * D_MODEL, LANES), lambda i: (0, 0))],
        out_specs=(pl.BlockSpec((bm, LANES), lambda i: (i, 0)),
                   pl.BlockSpec((SUBLANES, LANES), lambda i: (0, 0))),
        out_shape=(jax.ShapeDtypeStruct((n, LANES), F32), jax.ShapeDtypeStruct((SUBLANES, LANES), F32)),
        scratch_shapes=[pltpu.VMEM((SUBLANES, LANES), F32)],
        compiler_params=_params(("arbitrary",)),
        name="moe_router",
    )(x, wr3)


MOE_BM = 512


def _row_copy(src_hbm, src_tok, dst_buf, dst_tok, sem):
    return pltpu.make_async_copy(
        src_hbm.at[pl.ds(pl.multiple_of(src_tok * TOK_ROWS, TOK_ROWS), TOK_ROWS), :],
        dst_buf.at[pl.ds(pl.multiple_of(dst_tok * TOK_ROWS, TOK_ROWS), TOK_ROWS), :], sem)


DISPATCH_BM = 512
DISPATCH_UNROLL = 4


def _moe_dispatch_body(n_tok, bm, dest_ref, pad_lo_ref, pad_hi_ref, x_ref, o_hbm, zero_buf, sem):
    i = pl.program_id(0)
    base = i * bm

    def copies(j):
        return [_row_copy(x_ref, j, o_hbm, dest_ref[slot * n_tok + base + j], sem) for slot in range(2)]

    def issue(j, carry):
        for cp in copies(j):
            cp.start()
        return carry

    def wait(j, carry):
        for cp in copies(j):
            cp.wait()
        return carry

    lax.fori_loop(0, bm, issue, 0, unroll=DISPATCH_UNROLL)
    lax.fori_loop(0, bm, wait, 0, unroll=DISPATCH_UNROLL)

    @pl.when(i == 0)
    def _zero_padding_rows():
        zero_buf[...] = jnp.zeros_like(zero_buf)

        def pad_copy(r):
            return pltpu.make_async_copy(
                zero_buf, o_hbm.at[pl.ds(pl.multiple_of(r * TOK_ROWS, TOK_ROWS), TOK_ROWS), :], sem)

        def z_issue(r, carry):
            pad_copy(r).start()
            return carry

        def z_wait(r, carry):
            pad_copy(r).wait()
            return carry

        for e in range(N_EXPERTS + 1):
            lax.fori_loop(pad_lo_ref[e], pad_hi_ref[e], z_issue, 0)
            lax.fori_loop(pad_lo_ref[e], pad_hi_ref[e], z_wait, 0)


def _moe_dispatch(dest, pad_lo, pad_hi, rows, n_tok, n_rows):
    bm = DISPATCH_BM if n_tok % DISPATCH_BM == 0 else n_tok
    return pl.pallas_call(
        functools.partial(_moe_dispatch_body, n_tok, bm),
        grid_spec=pltpu.PrefetchScalarGridSpec(
            num_scalar_prefetch=3,
            grid=(n_tok // bm,),
            in_specs=[pl.BlockSpec((bm * TOK_ROWS, LANES), lambda i, dest, lo, hi: (i, 0))],
            out_specs=pl.BlockSpec(memory_space=pl.ANY),
            scratch_shapes=[pltpu.VMEM((TOK_ROWS, LANES), F32), pltpu.SemaphoreType.DMA(())],
        ),
        out_shape=jax.ShapeDtypeStruct((n_rows * TOK_ROWS, LANES), F32),
        compiler_params=_params(("arbitrary",)),
        name="moe_dispatch",
    )(dest, pad_lo, pad_hi, rows)


def _moe_up_body(te_ref, valid_ref, x_ref, w1_ref, w3_ref, o_ref):
    i = pl.program_id(1)

    @pl.when(valid_ref[i] != 0)
    def _compute():
        xb = jnp.concatenate([_bf(p) for p in _load_token_major(x_ref, 0, MOE_BM)], axis=1)
        h1 = _dot(xb, w1_ref[...])
        h3 = _dot(xb, w3_ref[...])
        o_ref[...] = (h1 * jax.nn.sigmoid(h1) * h3).astype(o_ref.dtype)

    @pl.when(valid_ref[i] == 0)
    def _empty():
        o_ref[...] = jnp.zeros_like(o_ref)


def _moe_up(te, valid, xs_rows, w1, w3, layer, bn):
    n_rows = xs_rows.shape[0] // TOK_ROWS
    fe = w1.shape[3]
    wspec = pl.BlockSpec((None, None, D_MODEL, bn), lambda j, i, te, valid: (layer, te[i], 0, j))
    return pl.pallas_call(
        _moe_up_body,
        grid_spec=pltpu.PrefetchScalarGridSpec(
            num_scalar_prefetch=2,
            grid=(fe // bn, n_rows // MOE_BM),
            in_specs=[pl.BlockSpec((MOE_BM * TOK_ROWS, LANES), lambda j, i, te, valid: (i, 0)),
                      wspec, wspec],
            out_specs=pl.BlockSpec((MOE_BM, bn), lambda j, i, te, valid: (i, j)),
        ),
        out_shape=jax.ShapeDtypeStruct((n_rows, fe), BF16),
        compiler_params=_params(("arbitrary", "arbitrary")),
        name="moe_up",
    )(te, valid, xs_rows, w1, w3)


def _moe_down_body(te_ref, valid_ref, h_ref, w_ref, o_ref):
    i = pl.program_id(0)

    @pl.when(valid_ref[i] != 0)
    def _compute():
        _store_token_major(o_ref, _dot(h_ref[...], w_ref[...]))

    @pl.when(valid_ref[i] == 0)
    def _empty():
        o_ref[...] = jnp.zeros_like(o_ref)


def _moe_down(te, valid, h, w2, layer):
    n_rows, fe = h.shape
    return pl.pallas_call(
        _moe_down_body,
        grid_spec=pltpu.PrefetchScalarGridSpec(
            num_scalar_prefetch=2,
            grid=(n_rows // MOE_BM,),
            in_specs=[pl.BlockSpec((MOE_BM, fe), lambda i, te, valid: (i, 0)),
                      pl.BlockSpec((None, None, fe, D_MODEL), lambda i, te, valid: (layer, te[i], 0, 0))],
            out_specs=pl.BlockSpec((MOE_BM * TOK_ROWS, LANES), lambda i, te, valid: (i, 0)),
        ),
        out_shape=jax.ShapeDtypeStruct((n_rows * TOK_ROWS, LANES), F32),
        compiler_params=_params(("arbitrary",)),
        name="moe_down",
    )(te, valid, h, w2)


def _moe_combine_ln_body(n_tok, dest_ref, ys_hbm, route_ref, x_ref, g_ref, b_ref, of_ref, ob_ref, buf,
                         sem):
    bm = x_ref.shape[0]
    base = pl.program_id(0) * bm

    def copies(j):
        return [_row_copy(ys_hbm, dest_ref[slot * n_tok + base + j], buf, slot * bm + j, sem)
                for slot in range(2)]

    def issue(j, carry):
        for cp in copies(j):
            cp.start()
        return carry

    def wait(j, carry):
        for cp in copies(j):
            cp.wait()
        return carry

    lax.fori_loop(0, bm, issue, 0, unroll=DISPATCH_UNROLL)
    lax.fori_loop(0, bm, wait, 0, unroll=DISPATCH_UNROLL)
    route = route_ref[...]
    gate1, gate2 = route[:, 2:3], route[:, 3:4]
    first = _load_token_major(buf, 0, bm)
    second = _load_token_major(buf, bm * TOK_ROWS, bm)
    f = jnp.concatenate([gate1 * p + gate2 * q for p, q in zip(first, second)], axis=1)
    out = _layer_norm(DEEPNORM_ALPHA * x_ref[...] + f, g_ref[...], b_ref[...])
    of_ref[...] = out
    ob_ref[...] = out.astype(BF16)


def _moe_combine_ln(dest, ys, route, x, g, b, bm):
    n = x.shape[0]
    row = pl.BlockSpec((bm, D_MODEL), lambda i, dest: (i, 0))
    vec = pl.BlockSpec((1, D_MODEL), lambda i, dest: (0, 0))
    route_spec = pl.BlockSpec((bm, LANES), lambda i, dest: (i, 0))
    return pl.pallas_call(
        functools.partial(_moe_combine_ln_body, n),
        grid_spec=pltpu.PrefetchScalarGridSpec(
            num_scalar_prefetch=1,
            grid=(n // bm,),
            in_specs=[pl.BlockSpec(memory_space=pl.ANY), route_spec, row, vec, vec],
            out_specs=(row, row),
            scratch_shapes=[pltpu.VMEM((2 * bm * TOK_ROWS, LANES), F32), pltpu.SemaphoreType.DMA(())],
        ),
        out_shape=(jax.ShapeDtypeStruct((n, D_MODEL), F32), jax.ShapeDtypeStruct((n, D_MODEL), BF16)),
        compiler_params=_params(("arbitrary",)),
        name="moe_combine_ln",
    )(dest, ys, route, x, g, b)


def _moe_dispatch_plan(route, counts, n):
    n_rows = 2 * n + N_EXPERTS * MOE_BM
    expert = route[:, 0:2].astype(jnp.int32)
    rank = route[:, 4:6].astype(jnp.int32)
    count = counts[0, :N_EXPERTS].astype(jnp.int32)
    padded = (count + MOE_BM - 1) // MOE_BM * MOE_BM
    ends = jnp.cumsum(padded)
    starts = ends - padded
    dest = (jnp.take(starts, expert) + rank).T.reshape(-1)
    tile_start = jnp.arange(n_rows // MOE_BM, dtype=jnp.int32) * MOE_BM
    te = jnp.minimum(jnp.sum((tile_start[:, None] >= ends[None, :]).astype(jnp.int32), axis=1),
                     N_EXPERTS - 1)
    valid = (tile_start < ends[-1]).astype(jnp.int32)
    i32 = lambda v: v.astype(jnp.int32)
    pad_lo = jnp.concatenate([starts + count, ends[-1:]])
    pad_hi = jnp.concatenate([ends, jnp.full((1,), n_rows, ends.dtype)])
    return i32(dest), i32(te), valid, i32(pad_lo), i32(pad_hi), n_rows


def _pad_lanes(v, width):
    v = v.reshape(1, -1)
    return jnp.pad(v, ((0, 0), (0, width - v.shape[1])))


def _pick(n, pref):
    return pref if n % pref == 0 else n


def kernel(x, w_in, w_in_vres, ssd_conv_w, ssd_conv_b, ssd_dt_bias, ssd_a_log, ssd_d, ssd_norm_w, rw_mix, rw_vres_mix, rw_w0, rw_w_up, rw_a0, rw_a_up, rw_v0, rw_v_up, rw_g_up, rw_k_k, rw_k_a, rw_r_k, rw_ln_w, rw_ln_b, w_out, ln1_g, ln1_b, ln2_g, ln2_b, ffn_w1, ffn_w3, ffn_w2, moe_router, moe_w1, moe_w3, moe_w2):
    bsz, t, d = x.shape
    n = bsz * t
    depth = w_in.shape[0]
    assert d == D_MODEL and t % CHUNK == 0
    xf = x.reshape(n, d)
    xb = xf.astype(BF16)

    head_of_lane = jnp.arange(RWKV_WIDTH) // HEAD_DIM
    expand = (jnp.arange(LANES)[:, None] == head_of_lane[None, :]).astype(F32)
    blk_head = jnp.arange(MXU_WIDTH) // HEAD_DIM
    head_ones = (blk_head[:, None] == blk_head[None, :]).astype(BF16)
    head_ones = jnp.concatenate([head_ones, head_ones], axis=0)

    bm_big = _pick(n, 1024)
    bm_mid = _pick(n, 512)
    bm_small = _pick(n, 256)
    row = lambda v: v.reshape(1, -1)
    zcols = lambda k: jnp.zeros((d, k), F32)

    moe_w1_b, moe_w3_b, moe_w2_b = moe_w1.astype(BF16), moe_w3.astype(BF16), moe_w2.astype(BF16)

    vfirst = None
    for l in range(depth):
        w = w_in[l]
        o = 0
        segs = {}
        for name, width in (("z", SSD_WIDTH), ("xs", SSD_WIDTH), ("bc", 2 * SSD_GROUPS * SSD_STATE),
                            ("dt", SSD_HEADS), ("r", RWKV_WIDTH), ("k", RWKV_WIDTH), ("v", RWKV_WIDTH),
                            ("w_lo", W_LORA), ("a_lo", A_LORA), ("g_lo", G_LORA)):
            segs[name] = w[:, o:o + width]
            o += width
        v_lo_cols = w_in_vres[l - 1] if l > 0 else zcols(V_LORA)
        misc_cols = jnp.concatenate(
            [segs["dt"], zcols(MISC_WA - SSD_HEADS), segs["w_lo"], segs["a_lo"], segs["g_lo"], v_lo_cols,
             zcols(SMALLBLK - MISC_GV - G_LORA - V_LORA)], axis=1)
        w_comb = jnp.concatenate([segs["z"], segs["xs"], segs["r"], segs["k"], segs["v"], segs["bc"],
                                  misc_cols], axis=1).astype(BF16)

        proj = _matmul(xb, w_comb, bm_big, COLBLK, F32)

        cw, cb = ssd_conv_w[l], ssd_conv_b[l]
        y_ssd = _ssd_mixer(
            proj, bsz, t, cw[:, :SSD_WIDTH], row(cb[:SSD_WIDTH]), cw[:, SSD_WIDTH:], row(cb[SSD_WIDTH:]),
            _pad_lanes(ssd_dt_bias[l], LANES), _pad_lanes(ssd_a_log[l], LANES),
            row(jnp.repeat(ssd_d[l], HEAD_DIM)), row(ssd_norm_w[l]), expand.astype(BF16))

        mix = rw_mix[l]
        v_mix = rw_vres_mix[l - 1] if l > 0 else jnp.zeros((V_LORA,), F32)
        mix_misc = jnp.concatenate([jnp.zeros((MISC_WA,), F32), mix[3 * RWKV_WIDTH:], v_mix,
                                    jnp.zeros((SMALLBLK - MISC_GV - G_LORA - V_LORA,), F32)])
        wup = jnp.pad(rw_w_up[l], ((0, LANES - W_LORA), (0, 0)))
        wup_hi = wup.astype(BF16)
        wup_lo = (wup - wup_hi.astype(F32)).astype(BF16)
        wup3 = jnp.concatenate([wup_hi, wup_lo, wup_hi], axis=0)
        aup = jnp.pad(rw_a_up[l], ((W_LORA, 0), (0, 0))).astype(BF16)
        gup = jnp.pad(rw_g_up[l], ((0, 2 * LANES - G_LORA), (0, 0))).astype(BF16)
        if l > 0:
            v0 = row(rw_v0[l - 1])
            vup = jnp.pad(rw_v_up[l - 1], ((G_LORA, 2 * LANES - G_LORA - V_LORA), (0, 0))).astype(BF16)
        else:
            v0 = vup = None
        res = _rwkv_mixer(
            proj, bsz, t, vfirst, row(mix[:3 * RWKV_WIDTH]), row(mix_misc), row(rw_w0[l]), wup3,
            row(rw_a0[l]), aup, gup, v0, vup, row(rw_k_k[l]), row(rw_k_a[l]), row(rw_r_k[l]),
            row(rw_ln_w[l]), row(rw_ln_b[l]), head_ones)
        if l == 0:
            y_rw, vfirst = res
        else:
            y_rw = res

        is_moe = l % 2 == 1
        res = _outproj_ln(y_ssd, y_rw, w_out[l].astype(BF16), xf, row(ln1_g[l]), row(ln1_b[l]), bm_small,
                          emit_rows=is_moe)
        i = l // 2
        if not is_moe:
            xf, xb = res
            h = _swiglu_up(xb, ffn_w1[i].astype(BF16), ffn_w3[i].astype(BF16), bm_big, 512)
            xf, xb = _down_ln(h, ffn_w2[i].astype(BF16), xf, row(ln2_g[l]), row(ln2_b[l]), bm_mid, 1408)
        else:
            xf, xb, x_rows = res
            wr = jnp.pad(moe_router[i], ((0, 0), (0, LANES - N_EXPERTS)))
            wr_hi = wr.astype(BF16)
            wr_lo = (wr - wr_hi.astype(F32)).astype(BF16)
            route, counts = _router(xf, jnp.concatenate([wr_hi, wr_lo, wr_hi], axis=0), bm_mid)
            dest, te, valid, pad_lo, pad_hi, n_rows = _moe_dispatch_plan(route, counts, n)
            xs_rows = _moe_dispatch(dest, pad_lo, pad_hi, x_rows, n, n_rows)
            h = _moe_up(te, valid, xs_rows, moe_w1_b, moe_w3_b, i, 1408)
            ys = _moe_down(te, valid, h, moe_w2_b, i)
            xf, xb = _moe_combine_ln(dest, ys, route, xf, row(ln2_g[l]), row(ln2_b[l]), bm_small)

    return xf.reshape(bsz, t, d)
```

```python
import functools

import jax
import jax.numpy as jnp
from jax import lax
from jax.experimental import pallas as pl
from jax.experimental.pallas import tpu as pltpu

F32 = jnp.float32
BF16 = jnp.bfloat16

D_MODEL = 2048
HEAD_DIM = 64
SSD_WIDTH = 1024
RWKV_WIDTH = 1024
SSD_HEADS = SSD_WIDTH // HEAD_DIM
SSD_GROUPS = 2
SSD_STATE = 128
SSD_CONV_WIDTH = 4
RWKV_HEADS = RWKV_WIDTH // HEAD_DIM
W_LORA = 64
A_LORA = 64
V_LORA = 32
G_LORA = 160
N_EXPERTS = 8
DEPTH = 4
DEEPNORM_ALPHA = (2 * DEPTH) ** 0.25
LN_EPS = 1e-5
RMS_EPS = 1e-5
RWKV_GN_EPS = 64e-5
L2_EPS = 1e-12

LANES = 128
SUBLANES = 8
VMEM_LIMIT = 56 * 1024 * 1024

CHUNK = 128
HALO = SUBLANES

COLBLK = 1024
BLK_Z, BLK_XS, BLK_R, BLK_K, BLK_V = 0, 1, 2, 3, 4
SMALLBLK = 512
BLK_BC = 10
BLK_MISC = 11
PROJ_WIDTH = 6 * COLBLK
MISC_WA = 128
MISC_GV = 256


def _params(sem, vmem=VMEM_LIMIT):
    return pltpu.CompilerParams(dimension_semantics=sem, vmem_limit_bytes=vmem)


def _dot(a, b):
    return jnp.dot(a, b, preferred_element_type=F32)


def _dot_nt(a, b):
    return lax.dot_general(a, b, (((1,), (1,)), ((), ())), preferred_element_type=F32)


def _bf(x):
    return x.astype(BF16)


def _softplus(x):
    return jnp.maximum(x, 0.0) + jnp.log1p(jnp.exp(-jnp.abs(x)))


def _layer_norm(y, g, b):
    mu = jnp.mean(y, axis=-1, keepdims=True)
    d = y - mu
    var = jnp.mean(d * d, axis=-1, keepdims=True)
    return d * lax.rsqrt(var + LN_EPS) * g + b


def _mm_body(x_ref, w_ref, o_ref):
    o_ref[...] = _dot(_bf(x_ref[...]), w_ref[...]).astype(o_ref.dtype)


def _matmul(x, w, bm, bn, out_dtype):
    m, k = x.shape
    n = w.shape[1]
    return pl.pallas_call(
        _mm_body,
        grid=(m // bm, n // bn),
        in_specs=[pl.BlockSpec((bm, k), lambda i, j: (i, 0)),
                  pl.BlockSpec((k, bn), lambda i, j: (0, j))],
        out_specs=pl.BlockSpec((bm, bn), lambda i, j: (i, j)),
        out_shape=jax.ShapeDtypeStruct((m, n), out_dtype),
        compiler_params=_params(("parallel", "parallel")),
        name="in_proj",
    )(x, w)


def _ssd_body(z_ref, xs_ref, bc_ref, misc_ref, cwx_ref, cbx_ref, cwb_ref, cbb_ref, dtb_ref,
              alog_ref, dskip_ref, nw_ref, e_ref, o_ref, xs_buf, bc_buf, h_ref, y_ref):
    L = CHUNK
    c = pl.program_id(1)

    @pl.when(c == 0)
    def _init():
        xs_buf[0:HALO, :] = jnp.zeros((HALO, xs_buf.shape[1]), F32)
        bc_buf[0:HALO, :] = jnp.zeros((HALO, bc_buf.shape[1]), F32)
        h_ref[...] = jnp.zeros_like(h_ref)

    xs_buf[HALO:HALO + L, :] = xs_ref[...]
    bc_buf[HALO:HALO + L, :] = bc_ref[...]

    def conv_silu(buf, w_ref, b_ref):
        acc = b_ref[...]
        for i in range(SSD_CONV_WIDTH):
            acc = acc + buf[pl.ds(HALO - (SSD_CONV_WIDTH - 1) + i, L), :] * w_ref[i:i + 1, :]
        return acc * jax.nn.sigmoid(acc)

    xs = conv_silu(xs_buf, cwx_ref, cbx_ref)
    bcv = conv_silu(bc_buf, cwb_ref, cbb_ref)
    xs_buf[0:HALO, :] = xs_buf[L:L + HALO, :]
    bc_buf[0:HALO, :] = bc_buf[L:L + HALO, :]

    lane = lax.broadcasted_iota(jnp.int32, (1, LANES), 1)
    dt = _softplus(misc_ref[:, 0:LANES] + dtb_ref[...])
    a_neg = jnp.where(lane < SSD_HEADS, -jnp.exp(alog_ref[...]), 0.0)
    d_a = dt * a_neg
    row = lax.broadcasted_iota(jnp.int32, (L, L), 0)
    col = lax.broadcasted_iota(jnp.int32, (L, L), 1)
    causal = row >= col
    tri = causal.astype(BF16)
    a_cum = sum(_dot(tri, t) for t in _split3(d_a))
    a_cum_t = a_cum.T
    a_last = a_cum[L - 1:L, :]
    per_head = jnp.concatenate([dt, jnp.exp(a_last - a_cum), jnp.exp(a_cum)], axis=0)
    head_to_lanes = e_ref[...]
    full = sum(_dot(t, head_to_lanes) for t in _split3(per_head))
    dt_f, dte_f, ea_f = full[0:L], full[L:2 * L], full[2 * L:3 * L]
    xdt = xs * dt_f
    xdt_b = _bf(xdt)
    xdte_b = _bf(xdt * dte_f)
    lane_l = lax.broadcasted_iota(jnp.int32, (L, LANES), 1)
    lower_half = lane_l < HEAD_DIM
    gw = SSD_WIDTH // SSD_GROUPS
    heads_per_group = SSD_HEADS // SSD_GROUPS
    for g in range(SSD_GROUPS):
        b_g = bcv[:, g * SSD_STATE:(g + 1) * SSD_STATE]
        c_g = _bf(bcv[:, (SSD_GROUPS + g) * SSD_STATE:(SSD_GROUPS + g + 1) * SSD_STATE])
        cb = _dot_nt(c_g, _bf(b_g))
        h_g = h_ref[g]
        y_off = _dot(c_g, _bf(h_g)) * ea_f[:, g * gw:(g + 1) * gw]
        new_state = _dot(_bf(b_g.T), xdte_b[:, g * gw:(g + 1) * gw])
        for pr in range(heads_per_group // 2):
            ms = []
            for e in range(2):
                h = g * heads_per_group + pr * 2 + e
                seg = a_cum[:, h:h + 1] - a_cum_t[h:h + 1, :]
                dec = jnp.where(causal, jnp.exp(jnp.minimum(seg, 0.0)), 0.0)
                ms.append(_bf(cb * dec))
            lhs = jnp.concatenate(ms, axis=1)
            lo = g * gw + pr * LANES
            xp = xdt_b[:, lo:lo + LANES]
            zero = jnp.zeros_like(xp)
            rhs = jnp.concatenate([jnp.where(lower_half, xp, zero),
                                   jnp.where(lower_half, zero, xp)], axis=0)
            y_ref[:, lo:lo + LANES] = _dot(lhs, rhs) + y_off[:, pr * LANES:(pr + 1) * LANES]
        h_ref[g] = h_g * ea_f[L - 1:L, g * gw:(g + 1) * gw] + new_state

    y = y_ref[...] + xs * dskip_ref[...]
    zz = z_ref[...]
    u = y * (zz * jax.nn.sigmoid(zz))
    for g in range(SSD_GROUPS):
        ug = u[:, g * gw:(g + 1) * gw]
        ms_ = jnp.mean(ug * ug, axis=-1, keepdims=True)
        o_ref[:, g * gw:(g + 1) * gw] = (
            ug * lax.rsqrt(ms_ + RMS_EPS) * nw_ref[:, g * gw:(g + 1) * gw]).astype(o_ref.dtype)


def _ssd_mixer(proj, bsz, t, cwx, cbx, cwb, cbb, dtb, alog, dskip, nw, expand):
    nc = t // CHUNK
    rows = lambda b, c: b * nc + c
    const = lambda shape: pl.BlockSpec(shape, lambda b, c: (0,) * len(shape))
    return pl.pallas_call(
        _ssd_body,
        grid=(bsz, nc),
        in_specs=[
            pl.BlockSpec((CHUNK, COLBLK), lambda b, c: (rows(b, c), BLK_Z)),
            pl.BlockSpec((CHUNK, COLBLK), lambda b, c: (rows(b, c), BLK_XS)),
            pl.BlockSpec((CHUNK, SMALLBLK), lambda b, c: (rows(b, c), BLK_BC)),
            pl.BlockSpec((CHUNK, SMALLBLK), lambda b, c: (rows(b, c), BLK_MISC)),
            const(cwx.shape), const(cbx.shape), const(cwb.shape), const(cbb.shape),
            const(dtb.shape), const(alog.shape), const(dskip.shape), const(nw.shape),
            const(expand.shape),
        ],
        out_specs=pl.BlockSpec((CHUNK, SSD_WIDTH), lambda b, c: (rows(b, c), 0)),
        out_shape=jax.ShapeDtypeStruct((bsz * t, SSD_WIDTH), BF16),
        scratch_shapes=[
            pltpu.VMEM((HALO + CHUNK, SSD_WIDTH), F32),
            pltpu.VMEM((HALO + CHUNK, SMALLBLK), F32),
            pltpu.VMEM((SSD_GROUPS, SSD_STATE, SSD_WIDTH // SSD_GROUPS), F32),
            pltpu.VMEM((CHUNK, SSD_WIDTH), F32),
        ],
        compiler_params=_params(("parallel", "arbitrary")),
        name="ssd_mixer",
    )(proj, proj, proj, proj, cwx, cbx, cwb, cbb, dtb, alog, dskip, nw, expand)


PAIRS = RWKV_HEADS // 2
MXU_WIDTH = 256
ST_AMID, ST_RMID, ST_BMID, ST_KMID, ST_AST, ST_RST, ST_BEND, ST_KEND, ST_V, ST_N = range(10)


def _split2(x):
    hi = x.astype(BF16)
    return hi, (x - hi.astype(F32)).astype(BF16)


def _split3(x):
    hi = x.astype(BF16)
    r1 = x - hi.astype(F32)
    mid = r1.astype(BF16)
    return hi, mid, (r1 - mid.astype(F32)).astype(BF16)


def _rwkv_body(has_vres, *refs):
    if has_vres:
        (r_ref, k_ref, v_ref, misc_ref, vfirst_ref, mix_ref, mixm_ref, w0_ref, wup3_ref,
         a0_ref, aup_ref, gup_ref, v0_ref, vup_ref, kk_ref, ka_ref, rk_ref, lnw_ref, lnb_ref, bd_ref,
         o_ref, r_buf, k_buf, v_buf, m_buf, st_ref, h_ref, y_ref) = refs
    else:
        (r_ref, k_ref, v_ref, misc_ref, mix_ref, mixm_ref, w0_ref, wup3_ref,
         a0_ref, aup_ref, gup_ref, kk_ref, ka_ref, rk_ref, lnw_ref, lnb_ref, bd_ref,
         o_ref, vfirst_out_ref, r_buf, k_buf, v_buf, m_buf, st_ref, h_ref, y_ref) = refs
    L = CHUNK
    c = pl.program_id(1)

    @pl.when(c == 0)
    def _init():
        for buf in (r_buf, k_buf, v_buf, m_buf):
            buf[0:HALO, :] = jnp.zeros((HALO, buf.shape[1]), F32)
        h_ref[...] = jnp.zeros_like(h_ref)

    def shift_lerp(ref, buf, mix):
        cur = ref[...]
        buf[HALO:HALO + L, :] = cur
        prev = buf[pl.ds(HALO - 1, L), :]
        buf[0:HALO, :] = buf[L:L + HALO, :]
        return cur + (prev - cur) * mix

    w_ = RWKV_WIDTH
    r = shift_lerp(r_ref, r_buf, mix_ref[:, 0:w_])
    k = shift_lerp(k_ref, k_buf, mix_ref[:, w_:2 * w_])
    v = shift_lerp(v_ref, v_buf, mix_ref[:, 2 * w_:3 * w_])
    m = shift_lerp(misc_ref, m_buf, mixm_ref[...])
    wa = m[:, MISC_WA:MISC_WA + LANES]
    gv = m[:, MISC_GV:MISC_GV + 2 * LANES]

    th_hi, th_lo = _split2(jnp.tanh(wa))
    w_lin = w0_ref[...] + _dot(jnp.concatenate([th_hi, th_hi, th_lo], axis=1), wup3_ref[...])
    log_w = -jnp.exp(-_softplus(-w_lin) - 0.5)
    a = jax.nn.sigmoid(a0_ref[...] + _dot(_bf(wa), aup_ref[...]))
    g = _dot(_bf(jax.nn.sigmoid(gv)), gup_ref[...])
    if has_vres:
        v = v + (vfirst_ref[...] - v) * jax.nn.sigmoid(v0_ref[...] + _dot(_bf(gv), vup_ref[...]))
    else:
        vfirst_out_ref[...] = v

    n_blk = RWKV_WIDTH // MXU_WIDTH

    def head_sum(x):
        hi, lo = _split2(x)
        blocks = [jnp.concatenate([t[:, j * MXU_WIDTH:(j + 1) * MXU_WIDTH] for t in (hi, lo)], axis=1)
                  for j in range(n_blk)]
        s = _dot(jnp.concatenate(blocks, axis=0), bd_ref[...])
        return jnp.concatenate([s[j * L:(j + 1) * L] for j in range(n_blk)], axis=1)

    kx = k * kk_ref[...]
    kk = kx / jnp.maximum(jnp.sqrt(head_sum(kx * kx)), L2_EPS)
    k2 = k * (1.0 + (a - 1.0) * ka_ref[...])
    av = -kk
    bv = kk * a

    row = lax.broadcasted_iota(jnp.int32, (L, L), 0)
    col = lax.broadcasted_iota(jnp.int32, (L, L), 1)
    incl = row >= col
    strict = row > col
    tri = incl.astype(BF16)
    cum = _dot(jnp.concatenate([tri, tri, tri], axis=1), jnp.concatenate(_split3(log_w), axis=0))
    cum_prev = cum - log_w
    tot = cum[L - 1:L, :]
    mid = cum[L // 2 - 1:L // 2, :]
    st_ref[ST_AMID] = av * jnp.exp(cum_prev - mid)
    st_ref[ST_RMID] = r * jnp.exp(cum - mid)
    inv_mid = jnp.exp(mid - cum)
    st_ref[ST_BMID] = bv * inv_mid
    st_ref[ST_KMID] = k2 * inv_mid
    st_ref[ST_AST] = av * jnp.exp(cum_prev)
    st_ref[ST_RST] = r * jnp.exp(cum)
    to_end = jnp.exp(tot - cum)
    st_ref[ST_BEND] = bv * to_end
    st_ref[ST_KEND] = k2 * to_end
    st_ref[ST_V] = v
    e_tot_all = jnp.exp(tot)

    lane_l = lax.broadcasted_iota(jnp.int32, (L, LANES), 1)
    lower = lane_l < HEAD_DIM
    r2 = lax.broadcasted_iota(jnp.int32, (LANES, LANES), 0)
    c2 = lax.broadcasted_iota(jnp.int32, (LANES, LANES), 1)
    same_head = (r2 < HEAD_DIM) == (c2 < HEAD_DIM)
    diag = r2 == c2
    n_rounds = L.bit_length() - 1

    lanes_of = [slice(p * LANES, (p + 1) * LANES) for p in range(PAIRS)]
    masks = (lower, jnp.logical_not(lower))
    heads = [(p, h) for p in range(PAIRS) for h in range(2)]
    v_b, v_sw, bk_t = [], [], []
    for sl in lanes_of:
        v_p = st_ref[ST_V, :, sl]
        v_b.append(_bf(v_p))
        v_sw.append(_bf(pltpu.roll(v_p, HEAD_DIM, 1)))
        bk_t.append(_bf(jnp.concatenate([st_ref[ST_BMID, :, sl].T, st_ref[ST_KMID, :, sl].T], axis=1)))
    zero_b = jnp.zeros_like(v_sw[0])

    mm, a_ak, a_rbk = [], [], []
    for p, h in heads:
        sl, mh = lanes_of[p], masks[h]
        ar = _bf(jnp.concatenate([jnp.where(mh, st_ref[ST_AMID, :, sl], 0.0),
                                  jnp.where(mh, st_ref[ST_RMID, :, sl], 0.0)], axis=0))
        aa = _dot(ar, bk_t[p])
        mm.append(_bf(jnp.where(strict, aa[0:L, 0:L], 0.0)))
        a_ak.append(_bf(jnp.where(strict, aa[0:L, L:2 * L], 0.0)))
        a_rbk.append(_bf(jnp.concatenate([jnp.where(incl, aa[L:2 * L, 0:L], 0.0),
                                          jnp.where(incl, aa[L:2 * L, L:2 * L], 0.0)], axis=1)))
    xs_ = [jnp.where(masks[h], st_ref[ST_AST, :, lanes_of[p]], _dot(a_ak[i], v_sw[p]))
           for i, (p, h) in enumerate(heads)]
    for rnd in range(n_rounds - 1):
        prods = [_dot(m_, jnp.concatenate([m_, _bf(x)], axis=1)) for x, m_ in zip(xs_, mm)]
        xs_ = [x + pr[:, L:L + LANES] for x, pr in zip(xs_, prods)]
        mm = [_bf(pr[:, 0:L]) for pr in prods]
    xs_ = [x + _dot(m_, _bf(x)) for x, m_ in zip(xs_, mm)]
    yz = [_dot(a_rbk[i], jnp.concatenate([_bf(xs_[i]), jnp.where(masks[h], zero_b, v_sw[p])], axis=0))
          for i, (p, h) in enumerate(heads)]

    for p, sl in enumerate(lanes_of):
        x0, x1, yz0, yz1 = xs_[2 * p], xs_[2 * p + 1], yz[2 * p], yz[2 * p + 1]
        a_bar = jnp.where(lower, x0, x1)
        uv = pltpu.roll(jnp.where(lower, x1, x0), HEAD_DIM, 1)
        r_bar = st_ref[ST_RST, :, sl] + jnp.where(lower, yz0, yz1)
        yv = pltpu.roll(jnp.where(lower, yz1, yz0), HEAD_DIM, 1)
        b_end_t = _bf(st_ref[ST_BEND, :, sl].T)
        k_end_t = _bf(st_ref[ST_KEND, :, sl].T)
        g_mat = (jnp.where(same_head, _dot(b_end_t, _bf(a_bar)), 0.0)
                 + jnp.where(diag, e_tot_all[:, sl], 0.0))
        h_add = jnp.where(same_head, _dot(b_end_t, _bf(uv)) + _dot(k_end_t, v_b[p]), 0.0)
        h_old = _bf(h_ref[p])
        y_ref[:, sl] = _dot(_bf(r_bar), h_old) + yv
        h_ref[p] = _dot(_bf(g_mat), h_old) + h_add

    y = y_ref[...]
    inv_n = 1.0 / HEAD_DIM
    mu = head_sum(y) * inv_n
    d = y - mu
    var = head_sum(d * d) * inv_n
    y = d * lax.rsqrt(var + RWKV_GN_EPS) * lnw_ref[...] + lnb_ref[...]
    y = y + head_sum(r * k2 * rk_ref[...]) * v
    o_ref[...] = (y * g).astype(o_ref.dtype)


def _rwkv_mixer(proj, bsz, t, vfirst, mix, mixm, w0, wup3, a0, aup, gup, v0, vup, k_k, k_a,
                r_k, ln_w, ln_b, bd):
    nc = t // CHUNK
    rows = lambda b, c: b * nc + c
    const = lambda x: pl.BlockSpec(x.shape, lambda b, c: (0,) * x.ndim)
    has_vres = vfirst is not None
    act_specs = [
        pl.BlockSpec((CHUNK, COLBLK), lambda b, c: (rows(b, c), BLK_R)),
        pl.BlockSpec((CHUNK, COLBLK), lambda b, c: (rows(b, c), BLK_K)),
        pl.BlockSpec((CHUNK, COLBLK), lambda b, c: (rows(b, c), BLK_V)),
        pl.BlockSpec((CHUNK, SMALLBLK), lambda b, c: (rows(b, c), BLK_MISC)),
    ]
    wide_spec = pl.BlockSpec((CHUNK, RWKV_WIDTH), lambda b, c: (rows(b, c), 0))
    if has_vres:
        args = [proj, proj, proj, proj, vfirst, mix, mixm, w0, wup3, a0, aup, gup, v0, vup,
                k_k, k_a, r_k, ln_w, ln_b, bd]
        in_specs = act_specs + [wide_spec] + [const(x) for x in args[5:]]
        out_specs = wide_spec
        out_shape = jax.ShapeDtypeStruct((bsz * t, RWKV_WIDTH), BF16)
    else:
        args = [proj, proj, proj, proj, mix, mixm, w0, wup3, a0, aup, gup,
                k_k, k_a, r_k, ln_w, ln_b, bd]
        in_specs = act_specs + [const(x) for x in args[4:]]
        out_specs = (wide_spec, wide_spec)
        out_shape = (jax.ShapeDtypeStruct((bsz * t, RWKV_WIDTH), BF16),
                     jax.ShapeDtypeStruct((bsz * t, RWKV_WIDTH), F32))
    return pl.pallas_call(
        functools.partial(_rwkv_body, has_vres),
        grid=(bsz, nc),
        in_specs=in_specs,
        out_specs=out_specs,
        out_shape=out_shape,
        scratch_shapes=[
            pltpu.VMEM((HALO + CHUNK, RWKV_WIDTH), F32),
            pltpu.VMEM((HALO + CHUNK, RWKV_WIDTH), F32),
            pltpu.VMEM((HALO + CHUNK, RWKV_WIDTH), F32),
            pltpu.VMEM((HALO + CHUNK, SMALLBLK), F32),
            pltpu.VMEM((ST_N, CHUNK, RWKV_WIDTH), F32),
            pltpu.VMEM((PAIRS, LANES, LANES), F32),
            pltpu.VMEM((CHUNK, RWKV_WIDTH), F32),
        ],
        compiler_params=_params(("parallel", "arbitrary")),
        name="rwkv_mixer",
    )(*args)


TOK_ROWS = D_MODEL // LANES


def _store_token_major(ref, val):
    bm = val.shape[0]
    for s in range(TOK_ROWS):
        ref[pl.ds(s, bm, stride=TOK_ROWS), :] = val[:, s * LANES:(s + 1) * LANES]


def _load_token_major(ref, first_row, bm):
    return [ref[pl.ds(first_row + s, bm, stride=TOK_ROWS), :] for s in range(TOK_ROWS)]


def _outproj_ln_body(emit_rows, ys_ref, yr_ref, wt_ref, wb_ref, x_ref, g_ref, b_ref, of_ref, ob_ref,
                     *rows_ref):
    mixed = _dot(ys_ref[...], wt_ref[...]) + _dot(yr_ref[...], wb_ref[...])
    out = _layer_norm(DEEPNORM_ALPHA * x_ref[...] + mixed, g_ref[...], b_ref[...])
    of_ref[...] = out
    ob_ref[...] = out.astype(BF16)
    if emit_rows:
        _store_token_major(rows_ref[0], out)


def _outproj_ln(ys, yr, w_out, x, g, b, bm, emit_rows):
    n = x.shape[0]
    half = w_out.shape[0] // 2
    row = lambda width: pl.BlockSpec((bm, width), lambda i: (i, 0))
    vec = pl.BlockSpec((1, D_MODEL), lambda i: (0, 0))
    out_specs = [row(D_MODEL), row(D_MODEL)]
    out_shape = [jax.ShapeDtypeStruct((n, D_MODEL), F32), jax.ShapeDtypeStruct((n, D_MODEL), BF16)]
    if emit_rows:
        out_specs.append(pl.BlockSpec((bm * TOK_ROWS, LANES), lambda i: (i, 0)))
        out_shape.append(jax.ShapeDtypeStruct((n * TOK_ROWS, LANES), F32))
    return pl.pallas_call(
        functools.partial(_outproj_ln_body, emit_rows),
        grid=(n // bm,),
        in_specs=[row(half), row(half),
                  pl.BlockSpec((half, D_MODEL), lambda i: (0, 0), pipeline_mode=pl.Buffered(1)),
                  pl.BlockSpec((half, D_MODEL), lambda i: (1, 0), pipeline_mode=pl.Buffered(1)),
                  row(D_MODEL), vec, vec],
        out_specs=tuple(out_specs),
        out_shape=tuple(out_shape),
        compiler_params=_params(("parallel",)),
        name="out_proj_ln",
    )(ys, yr, w_out, w_out, x, g, b)


def _down_ln_body(h_ref, w_ref, x_ref, g_ref, b_ref, of_ref, ob_ref, acc_ref):
    kk = pl.program_id(1)

    @pl.when(kk == 0)
    def _zero():
        acc_ref[...] = jnp.zeros_like(acc_ref)

    acc_ref[...] += _dot(h_ref[...], w_ref[...])

    @pl.when(kk == pl.num_programs(1) - 1)
    def _finish():
        out = _layer_norm(DEEPNORM_ALPHA * x_ref[...] + acc_ref[...], g_ref[...], b_ref[...])
        of_ref[...] = out
        ob_ref[...] = out.astype(BF16)


def _down_ln(h, w, x, g, b, bm, bk):
    n, kdim = h.shape
    row = pl.BlockSpec((bm, D_MODEL), lambda i, k: (i, 0))
    vec = pl.BlockSpec((1, D_MODEL), lambda i, k: (0, 0))
    return pl.pallas_call(
        _down_ln_body,
        grid=(n // bm, kdim // bk),
        in_specs=[pl.BlockSpec((bm, bk), lambda i, k: (i, k)),
                  pl.BlockSpec((bk, D_MODEL), lambda i, k: (k, 0)),
                  row, vec, vec],
        out_specs=(row, row),
        out_shape=(jax.ShapeDtypeStruct((n, D_MODEL), F32),
                   jax.ShapeDtypeStruct((n, D_MODEL), BF16)),
        scratch_shapes=[pltpu.VMEM((bm, D_MODEL), F32)],
        compiler_params=_params(("parallel", "arbitrary")),
        name="down_proj_ln",
    )(h, w, x, g, b)


def _swiglu_up_body(x_ref, w1_ref, w3_ref, o_ref):
    xb = x_ref[...]
    h1 = _dot(xb, w1_ref[...])
    h3 = _dot(xb, w3_ref[...])
    o_ref[...] = (h1 * jax.nn.sigmoid(h1) * h3).astype(o_ref.dtype)


def _swiglu_up(xb, w1, w3, bm, bn):
    n = xb.shape[0]
    f = w1.shape[1]
    return pl.pallas_call(
        _swiglu_up_body,
        grid=(n // bm, f // bn),
        in_specs=[pl.BlockSpec((bm, D_MODEL), lambda i, j: (i, 0)),
                  pl.BlockSpec((D_MODEL, bn), lambda i, j: (0, j)),
                  pl.BlockSpec((D_MODEL, bn), lambda i, j: (0, j))],
        out_specs=pl.BlockSpec((bm, bn), lambda i, j: (i, j)),
        out_shape=jax.ShapeDtypeStruct((n, f), BF16),
        compiler_params=_params(("parallel", "parallel")),
        name="swiglu_up",
    )(xb, w1, w3)


def _router_body(x_ref, wr3_ref, o_ref, cnt_ref, run_ref):
    i = pl.program_id(0)

    @pl.when(i == 0)
    def _init():
        run_ref[...] = jnp.zeros_like(run_ref)

    x_hi, x_lo = _split2(x_ref[...])
    logits = _dot(jnp.concatenate([x_hi, x_hi, x_lo], axis=1), wr3_ref[...])
    lane = lax.broadcasted_iota(jnp.int32, logits.shape, 1)
    neg = jnp.float32(-jnp.inf)
    logits = jnp.where(lane < N_EXPERTS, logits, neg)
    top1 = jnp.max(logits, axis=-1, keepdims=True)
    idx1 = jnp.min(jnp.where(logits == top1, lane, LANES), axis=-1, keepdims=True)
    rest = jnp.where(lane == idx1, neg, logits)
    top2 = jnp.max(rest, axis=-1, keepdims=True)
    idx2 = jnp.min(jnp.where(rest == top2, lane, LANES), axis=-1, keepdims=True)
    e2 = jnp.exp(top2 - top1)
    gate1 = 1.0 / (1.0 + e2)
    gate2 = e2 / (1.0 + e2)
    first = (lane == idx1).astype(F32)
    second = (lane == idx2).astype(F32)
    both = first + second
    bm = both.shape[0]
    later = (lax.broadcasted_iota(jnp.int32, (bm, bm), 0) > lax.broadcasted_iota(jnp.int32, (bm, bm), 1))
    earlier = _dot(later.astype(BF16), _bf(both)) + run_ref[0:1, :]
    rank1 = jnp.sum(first * earlier, axis=-1, keepdims=True)
    rank2 = jnp.sum(second * earlier, axis=-1, keepdims=True)
    run_ref[...] = run_ref[...] + jnp.sum(both, axis=0, keepdims=True)
    cnt_ref[...] = run_ref[...]
    o_ref[...] = (jnp.where(lane == 0, idx1.astype(F32), 0.0) + jnp.where(lane == 1, idx2.astype(F32), 0.0)
                  + jnp.where(lane == 2, gate1, 0.0) + jnp.where(lane == 3, gate2, 0.0)
                  + jnp.where(lane == 4, rank1, 0.0) + jnp.where(lane == 5, rank2, 0.0))


def _router(x, wr3, bm):
    n = x.shape[0]
    return pl.pallas_call(
        _router_body,
        grid=(n // bm,),
        in_specs=[pl.BlockSpec((bm, D_MODEL), lambda i: (i, 0)),
                  pl.BlockSpec((3 * D_MODEL, LANES), lambda i: (0, 0))],
        out_specs=(pl.BlockSpec((bm, LANES), lambda i: (i, 0)),
                   pl.BlockSpec((SUBLANES, LANES), lambda i: (0, 0))),
        out_shape=(jax.ShapeDtypeStruct((n, LANES), F32), jax.ShapeDtypeStruct((SUBLANES, LANES), F32)),
        scratch_shapes=[pltpu.VMEM((SUBLANES, LANES), F32)],
        compiler_params=_params(("arbitrary",)),
        name="moe_router",
    )(x, wr3)


MOE_BM = 512


def _row_copy(src_hbm, src_tok, dst_buf, dst_tok, sem):
    return pltpu.make_async_copy(
        src_hbm.at[pl.ds(pl.multiple_of(src_tok * TOK_ROWS, TOK_ROWS), TOK_ROWS), :],
        dst_buf.at[pl.ds(pl.multiple_of(dst_tok * TOK_ROWS, TOK_ROWS), TOK_ROWS), :], sem)


DISPATCH_BM = 512
DISPATCH_UNROLL = 8


def _moe_dispatch_body(n_tok, bm, dest_ref, pad_lo_ref, pad_hi_ref, x_ref, o_hbm, zero_buf, sem):
    i = pl.program_id(0)
    base = i * bm

    def copies(j):
        return [_row_copy(x_ref, j, o_hbm, dest_ref[slot * n_tok + base + j], sem) for slot in range(2)]

    def issue(j, carry):
        for cp in copies(j):
            cp.start()
        return carry

    def wait(j, carry):
        for cp in copies(j):
            cp.wait()
        return carry

    lax.fori_loop(0, bm, issue, 0, unroll=DISPATCH_UNROLL)
    lax.fori_loop(0, bm, wait, 0, unroll=DISPATCH_UNROLL)

    @pl.when(i == 0)
    def _zero_padding_rows():
        zero_buf[...] = jnp.zeros_like(zero_buf)

        def pad_copy(r):
            return pltpu.make_async_copy(
                zero_buf, o_hbm.at[pl.ds(pl.multiple_of(r * TOK_ROWS, TOK_ROWS), TOK_ROWS), :], sem)

        def z_issue(r, carry):
            pad_copy(r).start()
            return carry

        def z_wait(r, carry):
            pad_copy(r).wait()
            return carry

        for e in range(N_EXPERTS + 1):
            lax.fori_loop(pad_lo_ref[e], pad_hi_ref[e], z_issue, 0)
            lax.fori_loop(pad_lo_ref[e], pad_hi_ref[e], z_wait, 0)


def _moe_dispatch(dest, pad_lo, pad_hi, rows, n_tok, n_rows):
    bm = DISPATCH_BM if n_tok % DISPATCH_BM == 0 else n_tok
    return pl.pallas_call(
        functools.partial(_moe_dispatch_body, n_tok, bm),
        grid_spec=pltpu.PrefetchScalarGridSpec(
            num_scalar_prefetch=3,
            grid=(n_tok // bm,),
            in_specs=[pl.BlockSpec((bm * TOK_ROWS, LANES), lambda i, dest, lo, hi: (i, 0))],
            out_specs=pl.BlockSpec(memory_space=pl.ANY),
            scratch_shapes=[pltpu.VMEM((TOK_ROWS, LANES), F32), pltpu.SemaphoreType.DMA(())],
        ),
        out_shape=jax.ShapeDtypeStruct((n_rows * TOK_ROWS, LANES), F32),
        compiler_params=_params(("arbitrary",)),
        name="moe_dispatch",
    )(dest, pad_lo, pad_hi, rows)


def _moe_up_body(te_ref, valid_ref, x_ref, w1_ref, w3_ref, o_ref):
    i = pl.program_id(1)

    @pl.when(valid_ref[i] != 0)
    def _compute():
        xb = jnp.concatenate([_bf(p) for p in _load_token_major(x_ref, 0, MOE_BM)], axis=1)
        h1 = _dot(xb, w1_ref[...])
        h3 = _dot(xb, w3_ref[...])
        o_ref[...] = (h1 * jax.nn.sigmoid(h1) * h3).astype(o_ref.dtype)

    @pl.when(valid_ref[i] == 0)
    def _empty():
        o_ref[...] = jnp.zeros_like(o_ref)


def _moe_up(te, valid, xs_rows, w1, w3, layer, bn):
    n_rows = xs_rows.shape[0] // TOK_ROWS
    fe = w1.shape[3]
    wspec = pl.BlockSpec((None, None, D_MODEL, bn), lambda j, i, te, valid: (layer, te[i], 0, j))
    return pl.pallas_call(
        _moe_up_body,
        grid_spec=pltpu.PrefetchScalarGridSpec(
            num_scalar_prefetch=2,
            grid=(fe // bn, n_rows // MOE_BM),
            in_specs=[pl.BlockSpec((MOE_BM * TOK_ROWS, LANES), lambda j, i, te, valid: (i, 0)),
                      wspec, wspec],
            out_specs=pl.BlockSpec((MOE_BM, bn), lambda j, i, te, valid: (i, j)),
        ),
        out_shape=jax.ShapeDtypeStruct((n_rows, fe), BF16),
        compiler_params=_params(("arbitrary", "arbitrary")),
        name="moe_up",
    )(te, valid, xs_rows, w1, w3)


def _moe_down_body(te_ref, valid_ref, h_ref, w_ref, o_ref):
    i = pl.program_id(0)

    @pl.when(valid_ref[i] != 0)
    def _compute():
        _store_token_major(o_ref, _dot(h_ref[...], w_ref[...]))

    @pl.when(valid_ref[i] == 0)
    def _empty():
        o_ref[...] = jnp.zeros_like(o_ref)


def _moe_down(te, valid, h, w2, layer):
    n_rows, fe = h.shape
    return pl.pallas_call(
        _moe_down_body,
        grid_spec=pltpu.PrefetchScalarGridSpec(
            num_scalar_prefetch=2,
            grid=(n_rows // MOE_BM,),
            in_specs=[pl.BlockSpec((MOE_BM, fe), lambda i, te, valid: (i, 0)),
                      pl.BlockSpec((None, None, fe, D_MODEL), lambda i, te, valid: (layer, te[i], 0, 0))],
            out_specs=pl.BlockSpec((MOE_BM * TOK_ROWS, LANES), lambda i, te, valid: (i, 0)),
        ),
        out_shape=jax.ShapeDtypeStruct((n_rows * TOK_ROWS, LANES), F32),
        compiler_params=_params(("arbitrary",)),
        name="moe_down",
    )(te, valid, h, w2)


def _moe_combine_ln_body(n_tok, dest_ref, ys_hbm, route_ref, x_ref, g_ref, b_ref, of_ref, ob_ref, buf,
                         sem):
    bm = x_ref.shape[0]
    base = pl.program_id(0) * bm

    def copies(j):
        return [_row_copy(ys_hbm, dest_ref[slot * n_tok + base + j], buf, slot * bm + j, sem)
                for slot in range(2)]

    def issue(j, carry):
        for cp in copies(j):
            cp.start()
        return carry

    def wait(j, carry):
        for cp in copies(j):
            cp.wait()
        return carry

    lax.fori_loop(0, bm, issue, 0, unroll=DISPATCH_UNROLL)
    lax.fori_loop(0, bm, wait, 0, unroll=DISPATCH_UNROLL)
    route = route_ref[...]
    gate1, gate2 = route[:, 2:3], route[:, 3:4]
    first = _load_token_major(buf, 0, bm)
    second = _load_token_major(buf, bm * TOK_ROWS, bm)
    f = jnp.concatenate([gate1 * p + gate2 * q for p, q in zip(first, second)], axis=1)
    out = _layer_norm(DEEPNORM_ALPHA * x_ref[...] + f, g_ref[...], b_ref[...])
    of_ref[...] = out
    ob_ref[...] = out.astype(BF16)


def _moe_combine_ln(dest, ys, route, x, g, b, bm):
    n = x.shape[0]
    row = pl.BlockSpec((bm, D_MODEL), lambda i, dest: (i, 0))
    vec = pl.BlockSpec((1, D_MODEL), lambda i, dest: (0, 0))
    route_spec = pl.BlockSpec((bm, LANES), lambda i, dest: (i, 0))
    return pl.pallas_call(
        functools.partial(_moe_combine_ln_body, n),
        grid_spec=pltpu.PrefetchScalarGridSpec(
            num_scalar_prefetch=1,
            grid=(n // bm,),
            in_specs=[pl.BlockSpec(memory_space=pl.ANY), route_spec, row, vec, vec],
            out_specs=(row, row),
            scratch_shapes=[pltpu.VMEM((2 * bm * TOK_ROWS, LANES), F32), pltpu.SemaphoreType.DMA(())],
        ),
        out_shape=(jax.ShapeDtypeStruct((n, D_MODEL), F32), jax.ShapeDtypeStruct((n, D_MODEL), BF16)),
        compiler_params=_params(("arbitrary",)),
        name="moe_combine_ln",
    )(dest, ys, route, x, g, b)


def _moe_dispatch_plan(route, counts, n):
    n_rows = 2 * n + N_EXPERTS * MOE_BM
    expert = route[:, 0:2].astype(jnp.int32)
    rank = route[:, 4:6].astype(jnp.int32)
    count = counts[0, :N_EXPERTS].astype(jnp.int32)
    padded = (count + MOE_BM - 1) // MOE_BM * MOE_BM
    ends = jnp.cumsum(padded)
    starts = ends - padded
    dest = (jnp.take(starts, expert) + rank).T.reshape(-1)
    tile_start = jnp.arange(n_rows // MOE_BM, dtype=jnp.int32) * MOE_BM
    te = jnp.minimum(jnp.sum((tile_start[:, None] >= ends[None, :]).astype(jnp.int32), axis=1),
                     N_EXPERTS - 1)
    valid = (tile_start < ends[-1]).astype(jnp.int32)
    i32 = lambda v: v.astype(jnp.int32)
    pad_lo = jnp.concatenate([starts + count, ends[-1:]])
    pad_hi = jnp.concatenate([ends, jnp.full((1,), n_rows, ends.dtype)])
    return i32(dest), i32(te), valid, i32(pad_lo), i32(pad_hi), n_rows


def _pad_lanes(v, width):
    v = v.reshape(1, -1)
    return jnp.pad(v, ((0, 0), (0, width - v.shape[1])))


def _pick(n, pref):
    return pref if n % pref == 0 else n


def kernel(x, w_in, w_in_vres, ssd_conv_w, ssd_conv_b, ssd_dt_bias, ssd_a_log, ssd_d, ssd_norm_w, rw_mix, rw_vres_mix, rw_w0, rw_w_up, rw_a0, rw_a_up, rw_v0, rw_v_up, rw_g_up, rw_k_k, rw_k_a, rw_r_k, rw_ln_w, rw_ln_b, w_out, ln1_g, ln1_b, ln2_g, ln2_b, ffn_w1, ffn_w3, ffn_w2, moe_router, moe_w1, moe_w3, moe_w2):
    bsz, t, d = x.shape
    n = bsz * t
    depth = w_in.shape[0]
    assert d == D_MODEL and t % CHUNK == 0
    xf = x.reshape(n, d)
    xb = xf

    head_of_lane = jnp.arange(RWKV_WIDTH) // HEAD_DIM
    expand = (jnp.arange(LANES)[:, None] == head_of_lane[None, :]).astype(F32)
    blk_head = jnp.arange(MXU_WIDTH) // HEAD_DIM
    head_ones = (blk_head[:, None] == blk_head[None, :]).astype(BF16)
    head_ones = jnp.concatenate([head_ones, head_ones], axis=0)

    bm_big = _pick(n, 1024)
    bm_mid = _pick(n, 512)
    bm_small = _pick(n, 256)
    row = lambda v: v.reshape(1, -1)
    zcols = lambda k: jnp.zeros((d, k), F32)

    moe_w1_b, moe_w3_b, moe_w2_b = moe_w1.astype(BF16), moe_w3.astype(BF16), moe_w2.astype(BF16)

    vfirst = None
    for l in range(depth):
        w = w_in[l]
        o = 0
        segs = {}
        for name, width in (("z", SSD_WIDTH), ("xs", SSD_WIDTH), ("bc", 2 * SSD_GROUPS * SSD_STATE),
                            ("dt", SSD_HEADS), ("r", RWKV_WIDTH), ("k", RWKV_WIDTH), ("v", RWKV_WIDTH),
                            ("w_lo", W_LORA), ("a_lo", A_LORA), ("g_lo", G_LORA)):
            segs[name] = w[:, o:o + width]
            o += width
        v_lo_cols = w_in_vres[l - 1] if l > 0 else zcols(V_LORA)
        misc_cols = jnp.concatenate(
            [segs["dt"], zcols(MISC_WA - SSD_HEADS), segs["w_lo"], segs["a_lo"], segs["g_lo"], v_lo_cols,
             zcols(SMALLBLK - MISC_GV - G_LORA - V_LORA)], axis=1)
        w_comb = jnp.concatenate([segs["z"], segs["xs"], segs["r"], segs["k"], segs["v"], segs["bc"],
                                  misc_cols], axis=1).astype(BF16)

        proj = _matmul(xb, w_comb, bm_big, COLBLK, F32)

        cw, cb = ssd_conv_w[l], ssd_conv_b[l]
        y_ssd = _ssd_mixer(
            proj, bsz, t, cw[:, :SSD_WIDTH], row(cb[:SSD_WIDTH]), cw[:, SSD_WIDTH:], row(cb[SSD_WIDTH:]),
            _pad_lanes(ssd_dt_bias[l], LANES), _pad_lanes(ssd_a_log[l], LANES),
            row(jnp.repeat(ssd_d[l], HEAD_DIM)), row(ssd_norm_w[l]), expand.astype(BF16))

        mix = rw_mix[l]
        v_mix = rw_vres_mix[l - 1] if l > 0 else jnp.zeros((V_LORA,), F32)
        mix_misc = jnp.concatenate([jnp.zeros((MISC_WA,), F32), mix[3 * RWKV_WIDTH:], v_mix,
                                    jnp.zeros((SMALLBLK - MISC_GV - G_LORA - V_LORA,), F32)])
        wup = jnp.pad(rw_w_up[l], ((0, LANES - W_LORA), (0, 0)))
        wup_hi = wup.astype(BF16)
        wup_lo = (wup - wup_hi.astype(F32)).astype(BF16)
        wup3 = jnp.concatenate([wup_hi, wup_lo, wup_hi], axis=0)
        aup = jnp.pad(rw_a_up[l], ((W_LORA, 0), (0, 0))).astype(BF16)
        gup = jnp.pad(rw_g_up[l], ((0, 2 * LANES - G_LORA), (0, 0))).astype(BF16)
        if l > 0:
            v0 = row(rw_v0[l - 1])
            vup = jnp.pad(rw_v_up[l - 1], ((G_LORA, 2 * LANES - G_LORA - V_LORA), (0, 0))).astype(BF16)
        else:
            v0 = vup = None
        res = _rwkv_mixer(
            proj, bsz, t, vfirst, row(mix[:3 * RWKV_WIDTH]), row(mix_misc), row(rw_w0[l]), wup3,
            row(rw_a0[l]), aup, gup, v0, vup, row(rw_k_k[l]), row(rw_k_a[l]), row(rw_r_k[l]),
            row(rw_ln_w[l]), row(rw_ln_b[l]), head_ones)
        if l == 0:
            y_rw, vfirst = res
        else:
            y_rw = res

        is_moe = l % 2 == 1
        res = _outproj_ln(y_ssd, y_rw, w_out[l].astype(BF16), xf, row(ln1_g[l]), row(ln1_b[l]), bm_mid,
                          emit_rows=is_moe)
        i = l // 2
        if not is_moe:
            xf, xb = res
            h = _swiglu_up(xb, ffn_w1[i].astype(BF16), ffn_w3[i].astype(BF16), bm_big, 512)
            xf, xb = _down_ln(h, ffn_w2[i].astype(BF16), xf, row(ln2_g[l]), row(ln2_b[l]), bm_mid, 1408)
        else:
            xf, xb, x_rows = res
            wr = jnp.pad(moe_router[i], ((0, 0), (0, LANES - N_EXPERTS)))
            wr_hi = wr.astype(BF16)
            wr_lo = (wr - wr_hi.astype(F32)).astype(BF16)
            route, counts = _router(xf, jnp.concatenate([wr_hi, wr_lo, wr_hi], axis=0), bm_mid)
            dest, te, valid, pad_lo, pad_hi, n_rows = _moe_dispatch_plan(route, counts, n)
            xs_rows = _moe_dispatch(dest, pad_lo, pad_hi, x_rows, n, n_rows)
            h = _moe_up(te, valid, xs_rows, moe_w1_b, moe_w3_b, i, 1408)
            ys = _moe_down(te, valid, h, moe_w2_b, i)
            xf, xb = _moe_combine_ln(dest, ys, route, xf, row(ln2_g[l]), row(ln2_b[l]), bm_small)

    return xf.reshape(bsz, t, d)
```

```python
import functools

import jax
import jax.numpy as jnp
from jax import lax
from jax.experimental import pallas as pl
from jax.experimental.pallas import tpu as pltpu

F32 = jnp.float32
BF16 = jnp.bfloat16

D_MODEL = 2048
HEAD_DIM = 64
SSD_WIDTH = 1024
RWKV_WIDTH = 1024
SSD_HEADS = SSD_WIDTH // HEAD_DIM
SSD_GROUPS = 2
SSD_STATE = 128
SSD_CONV_WIDTH = 4
RWKV_HEADS = RWKV_WIDTH // HEAD_DIM
W_LORA = 64
A_LORA = 64
V_LORA = 32
G_LORA = 160
N_EXPERTS = 8
DEPTH = 4
DEEPNORM_ALPHA = (2 * DEPTH) ** 0.25
LN_EPS = 1e-5
RMS_EPS = 1e-5
RWKV_GN_EPS = 64e-5
L2_EPS = 1e-12

LANES = 128
SUBLANES = 8
VMEM_LIMIT = 56 * 1024 * 1024

CHUNK = 128
HALO = SUBLANES

COLBLK = 1024
BLK_Z, BLK_XS, BLK_R, BLK_K, BLK_V = 0, 1, 2, 3, 4
SMALLBLK = 512
BLK_BC = 10
BLK_MISC = 11
PROJ_WIDTH = 6 * COLBLK
MISC_WA = 128
MISC_GV = 256


def _params(sem, vmem=VMEM_LIMIT):
    return pltpu.CompilerParams(dimension_semantics=sem, vmem_limit_bytes=vmem)


def _dot(a, b):
    return jnp.dot(a, b, preferred_element_type=F32)


def _dot_nt(a, b):
    return lax.dot_general(a, b, (((1,), (1,)), ((), ())), preferred_element_type=F32)


def _bf(x):
    return x.astype(BF16)


def _softplus(x):
    return jnp.maximum(x, 0.0) + jnp.log1p(jnp.exp(-jnp.abs(x)))


def _layer_norm(y, g, b):
    mu = jnp.mean(y, axis=-1, keepdims=True)
    d = y - mu
    var = jnp.mean(d * d, axis=-1, keepdims=True)
    return d * lax.rsqrt(var + LN_EPS) * g + b


def _mm_body(x_ref, w_ref, o_ref):
    o_ref[...] = _dot(_bf(x_ref[...]), w_ref[...]).astype(o_ref.dtype)


def _matmul(x, w, bm, bn, out_dtype):
    m, k = x.shape
    n = w.shape[1]
    return pl.pallas_call(
        _mm_body,
        grid=(m // bm, n // bn),
        in_specs=[pl.BlockSpec((bm, k), lambda i, j: (i, 0)),
                  pl.BlockSpec((k, bn), lambda i, j: (0, j))],
        out_specs=pl.BlockSpec((bm, bn), lambda i, j: (i, j)),
        out_shape=jax.ShapeDtypeStruct((m, n), out_dtype),
        compiler_params=_params(("parallel", "parallel")),
        name="in_proj",
    )(x, w)


def _ssd_body(z_ref, xs_ref, bc_ref, misc_ref, cwx_ref, cbx_ref, cwb_ref, cbb_ref, dtb_ref,
              alog_ref, dskip_ref, nw_ref, e_ref, o_ref, xs_buf, bc_buf, h_ref, y_ref):
    L = CHUNK
    c = pl.program_id(1)

    @pl.when(c == 0)
    def _init():
        xs_buf[0:HALO, :] = jnp.zeros((HALO, xs_buf.shape[1]), F32)
        bc_buf[0:HALO, :] = jnp.zeros((HALO, bc_buf.shape[1]), F32)
        h_ref[...] = jnp.zeros_like(h_ref)

    xs_buf[HALO:HALO + L, :] = xs_ref[...]
    bc_buf[HALO:HALO + L, :] = bc_ref[...]

    def conv_silu(buf, w_ref, b_ref):
        acc = b_ref[...]
        for i in range(SSD_CONV_WIDTH):
            acc = acc + buf[pl.ds(HALO - (SSD_CONV_WIDTH - 1) + i, L), :] * w_ref[i:i + 1, :]
        return acc * jax.nn.sigmoid(acc)

    xs = conv_silu(xs_buf, cwx_ref, cbx_ref)
    bcv = conv_silu(bc_buf, cwb_ref, cbb_ref)
    xs_buf[0:HALO, :] = xs_buf[L:L + HALO, :]
    bc_buf[0:HALO, :] = bc_buf[L:L + HALO, :]

    lane = lax.broadcasted_iota(jnp.int32, (1, LANES), 1)
    dt = _softplus(misc_ref[:, 0:LANES] + dtb_ref[...])
    a_neg = jnp.where(lane < SSD_HEADS, -jnp.exp(alog_ref[...]), 0.0)
    d_a = dt * a_neg
    row = lax.broadcasted_iota(jnp.int32, (L, L), 0)
    col = lax.broadcasted_iota(jnp.int32, (L, L), 1)
    causal = row >= col
    tri = causal.astype(BF16)
    a_cum = sum(_dot(tri, t) for t in _split3(d_a))
    a_cum_t = a_cum.T
    a_last = a_cum[L - 1:L, :]
    per_head = jnp.concatenate([dt, jnp.exp(a_last - a_cum), jnp.exp(a_cum)], axis=0)
    head_to_lanes = e_ref[...]
    full = sum(_dot(t, head_to_lanes) for t in _split3(per_head))
    dt_f, dte_f, ea_f = full[0:L], full[L:2 * L], full[2 * L:3 * L]
    xdt = xs * dt_f
    xdt_b = _bf(xdt)
    xdte_b = _bf(xdt * dte_f)
    lane_l = lax.broadcasted_iota(jnp.int32, (L, LANES), 1)
    lower_half = lane_l < HEAD_DIM
    gw = SSD_WIDTH // SSD_GROUPS
    heads_per_group = SSD_HEADS // SSD_GROUPS
    for g in range(SSD_GROUPS):
        b_g = bcv[:, g * SSD_STATE:(g + 1) * SSD_STATE]
        c_g = _bf(bcv[:, (SSD_GROUPS + g) * SSD_STATE:(SSD_GROUPS + g + 1) * SSD_STATE])
        cb = _dot_nt(c_g, _bf(b_g))
        h_g = h_ref[g]
        y_off = _dot(c_g, _bf(h_g)) * ea_f[:, g * gw:(g + 1) * gw]
        new_state = _dot(_bf(b_g.T), xdte_b[:, g * gw:(g + 1) * gw])
        for pr in range(heads_per_group // 2):
            ms = []
            for e in range(2):
                h = g * heads_per_group + pr * 2 + e
                seg = a_cum[:, h:h + 1] - a_cum_t[h:h + 1, :]
                dec = jnp.where(causal, jnp.exp(jnp.minimum(seg, 0.0)), 0.0)
                ms.append(_bf(cb * dec))
            lhs = jnp.concatenate(ms, axis=1)
            lo = g * gw + pr * LANES
            xp = xdt_b[:, lo:lo + LANES]
            zero = jnp.zeros_like(xp)
            rhs = jnp.concatenate([jnp.where(lower_half, xp, zero),
                                   jnp.where(lower_half, zero, xp)], axis=0)
            y_ref[:, lo:lo + LANES] = _dot(lhs, rhs) + y_off[:, pr * LANES:(pr + 1) * LANES]
        h_ref[g] = h_g * ea_f[L - 1:L, g * gw:(g + 1) * gw] + new_state

    y = y_ref[...] + xs * dskip_ref[...]
    zz = z_ref[...]
    u = y * (zz * jax.nn.sigmoid(zz))
    for g in range(SSD_GROUPS):
        ug = u[:, g * gw:(g + 1) * gw]
        ms_ = jnp.mean(ug * ug, axis=-1, keepdims=True)
        o_ref[:, g * gw:(g + 1) * gw] = (
            ug * lax.rsqrt(ms_ + RMS_EPS) * nw_ref[:, g * gw:(g + 1) * gw]).astype(o_ref.dtype)


def _ssd_mixer(proj, bsz, t, cwx, cbx, cwb, cbb, dtb, alog, dskip, nw, expand):
    nc = t // CHUNK
    rows = lambda b, c: b * nc + c
    const = lambda shape: pl.BlockSpec(shape, lambda b, c: (0,) * len(shape))
    return pl.pallas_call(
        _ssd_body,
        grid=(bsz, nc),
        in_specs=[
            pl.BlockSpec((CHUNK, COLBLK), lambda b, c: (rows(b, c), BLK_Z)),
            pl.BlockSpec((CHUNK, COLBLK), lambda b, c: (rows(b, c), BLK_XS)),
            pl.BlockSpec((CHUNK, SMALLBLK), lambda b, c: (rows(b, c), BLK_BC)),
            pl.BlockSpec((CHUNK, SMALLBLK), lambda b, c: (rows(b, c), BLK_MISC)),
            const(cwx.shape), const(cbx.shape), const(cwb.shape), const(cbb.shape),
            const(dtb.shape), const(alog.shape), const(dskip.shape), const(nw.shape),
            const(expand.shape),
        ],
        out_specs=pl.BlockSpec((CHUNK, SSD_WIDTH), lambda b, c: (rows(b, c), 0)),
        out_shape=jax.ShapeDtypeStruct((bsz * t, SSD_WIDTH), BF16),
        scratch_shapes=[
            pltpu.VMEM((HALO + CHUNK, SSD_WIDTH), F32),
            pltpu.VMEM((HALO + CHUNK, SMALLBLK), F32),
            pltpu.VMEM((SSD_GROUPS, SSD_STATE, SSD_WIDTH // SSD_GROUPS), F32),
            pltpu.VMEM((CHUNK, SSD_WIDTH), F32),
        ],
        compiler_params=_params(("parallel", "arbitrary")),
        name="ssd_mixer",
    )(proj, proj, proj, proj, cwx, cbx, cwb, cbb, dtb, alog, dskip, nw, expand)


PAIRS = RWKV_HEADS // 2
MXU_WIDTH = 256
ST_AMID, ST_RMID, ST_BMID, ST_KMID, ST_AST, ST_RST, ST_BEND, ST_KEND, ST_V, ST_N = range(10)


def _split2(x):
    hi = x.astype(BF16)
    return hi, (x - hi.astype(F32)).astype(BF16)


def _split3(x):
    hi = x.astype(BF16)
    r1 = x - hi.astype(F32)
    mid = r1.astype(BF16)
    return hi, mid, (r1 - mid.astype(F32)).astype(BF16)


def _dot3(a, b):
    a_hi, a_lo = _split2(a)
    b_hi, b_lo = _split2(b)
    return _dot(jnp.concatenate([a_hi, a_hi, a_lo], axis=1), jnp.concatenate([b_hi, b_lo, b_hi], axis=0))


SOLVE_BASE_BLOCK = 8


def _unit_lower_solve(ns, xs):
    L = ns[0].shape[0]
    row = lax.broadcasted_iota(jnp.int32, (L, L), 0)
    col = lax.broadcasted_iota(jnp.int32, (L, L), 1)
    same_block = lambda b: (row // b) == (col // b)
    powers = [jnp.where(same_block(SOLVE_BASE_BLOCK), n, 0.0) for n in ns]
    eye = (row == col).astype(F32)
    invs = [eye + d for d in powers]
    k = 2
    while k < SOLVE_BASE_BLOCK:
        powers = [_dot3(d, d) for d in powers]
        invs = [t + _dot3(t, d) for t, d in zip(invs, powers)]
        k *= 2
    size = SOLVE_BASE_BLOCK
    while size < L:
        joined = jnp.logical_and(same_block(2 * size), jnp.logical_not(same_block(size)))
        offs = [_bf(jnp.where(joined, n, 0.0)) for n in ns]
        invs_b = [_bf(t) for t in invs]
        invs = [t + _dot(_bf(_dot(tb, o)), tb) for t, tb, o in zip(invs, invs_b, offs)]
        size *= 2
    return [_dot(_bf(t), _bf(x)) for t, x in zip(invs, xs)]


def _rwkv_body(has_vres, *refs):
    if has_vres:
        (r_ref, k_ref, v_ref, misc_ref, vfirst_ref, mix_ref, mixm_ref, w0_ref, wup3_ref,
         a0_ref, aup_ref, gup_ref, v0_ref, vup_ref, kk_ref, ka_ref, rk_ref, lnw_ref, lnb_ref, bd_ref,
         o_ref, r_buf, k_buf, v_buf, m_buf, st_ref, h_ref, y_ref) = refs
    else:
        (r_ref, k_ref, v_ref, misc_ref, mix_ref, mixm_ref, w0_ref, wup3_ref,
         a0_ref, aup_ref, gup_ref, kk_ref, ka_ref, rk_ref, lnw_ref, lnb_ref, bd_ref,
         o_ref, vfirst_out_ref, r_buf, k_buf, v_buf, m_buf, st_ref, h_ref, y_ref) = refs
    L = CHUNK
    c = pl.program_id(1)

    @pl.when(c == 0)
    def _init():
        for buf in (r_buf, k_buf, v_buf, m_buf):
            buf[0:HALO, :] = jnp.zeros((HALO, buf.shape[1]), F32)
        h_ref[...] = jnp.zeros_like(h_ref)

    def shift_lerp(ref, buf, mix):
        cur = ref[...]
        buf[HALO:HALO + L, :] = cur
        prev = buf[pl.ds(HALO - 1, L), :]
        buf[0:HALO, :] = buf[L:L + HALO, :]
        return cur + (prev - cur) * mix

    w_ = RWKV_WIDTH
    r = shift_lerp(r_ref, r_buf, mix_ref[:, 0:w_])
    k = shift_lerp(k_ref, k_buf, mix_ref[:, w_:2 * w_])
    v = shift_lerp(v_ref, v_buf, mix_ref[:, 2 * w_:3 * w_])
    m = shift_lerp(misc_ref, m_buf, mixm_ref[...])
    wa = m[:, MISC_WA:MISC_WA + LANES]
    gv = m[:, MISC_GV:MISC_GV + 2 * LANES]

    th_hi, th_lo = _split2(jnp.tanh(wa))
    w_lin = w0_ref[...] + _dot(jnp.concatenate([th_hi, th_hi, th_lo], axis=1), wup3_ref[...])
    log_w = -jnp.exp(-_softplus(-w_lin) - 0.5)
    a = jax.nn.sigmoid(a0_ref[...] + _dot(_bf(wa), aup_ref[...]))
    g = _dot(_bf(jax.nn.sigmoid(gv)), gup_ref[...])
    if has_vres:
        v = v + (vfirst_ref[...] - v) * jax.nn.sigmoid(v0_ref[...] + _dot(_bf(gv), vup_ref[...]))
    else:
        vfirst_out_ref[...] = v

    n_blk = RWKV_WIDTH // MXU_WIDTH

    def head_sum(x):
        hi, lo = _split2(x)
        blocks = [jnp.concatenate([t[:, j * MXU_WIDTH:(j + 1) * MXU_WIDTH] for t in (hi, lo)], axis=1)
                  for j in range(n_blk)]
        s = _dot(jnp.concatenate(blocks, axis=0), bd_ref[...])
        return jnp.concatenate([s[j * L:(j + 1) * L] for j in range(n_blk)], axis=1)

    kx = k * kk_ref[...]
    kk = kx / jnp.maximum(jnp.sqrt(head_sum(kx * kx)), L2_EPS)
    k2 = k * (1.0 + (a - 1.0) * ka_ref[...])
    av = -kk
    bv = kk * a

    row = lax.broadcasted_iota(jnp.int32, (L, L), 0)
    col = lax.broadcasted_iota(jnp.int32, (L, L), 1)
    incl = row >= col
    strict = row > col
    tri = incl.astype(BF16)
    cum = _dot(jnp.concatenate([tri, tri, tri], axis=1), jnp.concatenate(_split3(log_w), axis=0))
    cum_prev = cum - log_w
    tot = cum[L - 1:L, :]
    mid = cum[L // 2 - 1:L // 2, :]
    st_ref[ST_AMID] = av * jnp.exp(cum_prev - mid)
    st_ref[ST_RMID] = r * jnp.exp(cum - mid)
    inv_mid = jnp.exp(mid - cum)
    st_ref[ST_BMID] = bv * inv_mid
    st_ref[ST_KMID] = k2 * inv_mid
    st_ref[ST_AST] = av * jnp.exp(cum_prev)
    st_ref[ST_RST] = r * jnp.exp(cum)
    to_end = jnp.exp(tot - cum)
    st_ref[ST_BEND] = bv * to_end
    st_ref[ST_KEND] = k2 * to_end
    st_ref[ST_V] = v
    e_tot_all = jnp.exp(tot)

    lane_l = lax.broadcasted_iota(jnp.int32, (L, LANES), 1)
    lower = lane_l < HEAD_DIM
    r2 = lax.broadcasted_iota(jnp.int32, (LANES, LANES), 0)
    c2 = lax.broadcasted_iota(jnp.int32, (LANES, LANES), 1)
    same_head = (r2 < HEAD_DIM) == (c2 < HEAD_DIM)
    diag = r2 == c2
    n_rounds = L.bit_length() - 1

    lanes_of = [slice(p * LANES, (p + 1) * LANES) for p in range(PAIRS)]
    masks = (lower, jnp.logical_not(lower))
    heads = [(p, h) for p in range(PAIRS) for h in range(2)]
    v_b, v_sw, bk_t = [], [], []
    for sl in lanes_of:
        v_p = st_ref[ST_V, :, sl]
        v_b.append(_bf(v_p))
        v_sw.append(_bf(pltpu.roll(v_p, HEAD_DIM, 1)))
        bk_t.append(_bf(jnp.concatenate([st_ref[ST_BMID, :, sl].T, st_ref[ST_KMID, :, sl].T], axis=1)))
    zero_b = jnp.zeros_like(v_sw[0])

    mm, a_ak, a_rbk = [], [], []
    for p, h in heads:
        sl, mh = lanes_of[p], masks[h]
        ar = _bf(jnp.concatenate([jnp.where(mh, st_ref[ST_AMID, :, sl], 0.0),
                                  jnp.where(mh, st_ref[ST_RMID, :, sl], 0.0)], axis=0))
        aa = _dot(ar, bk_t[p])
        mm.append(jnp.where(strict, aa[0:L, 0:L], 0.0))
        a_ak.append(_bf(jnp.where(strict, aa[0:L, L:2 * L], 0.0)))
        a_rbk.append(_bf(jnp.concatenate([jnp.where(incl, aa[L:2 * L, 0:L], 0.0),
                                          jnp.where(incl, aa[L:2 * L, L:2 * L], 0.0)], axis=1)))
    xs_ = [jnp.where(masks[h], st_ref[ST_AST, :, lanes_of[p]], _dot(a_ak[i], v_sw[p]))
           for i, (p, h) in enumerate(heads)]
    xs_ = _unit_lower_solve(mm, xs_)
    yz =[_dot(a_rbk[i], jnp.concatenate([_bf(xs_[i]), jnp.where(masks[h], zero_b, v_sw[p])], axis=0))
          for i, (p, h) in enumerate(heads)]

    for p, sl in enumerate(lanes_of):
        x0, x1, yz0, yz1 = xs_[2 * p], xs_[2 * p + 1], yz[2 * p], yz[2 * p + 1]
        a_bar = jnp.where(lower, x0, x1)
        uv = pltpu.roll(jnp.where(lower, x1, x0), HEAD_DIM, 1)
        r_bar = st_ref[ST_RST, :, sl] + jnp.where(lower, yz0, yz1)
        yv = pltpu.roll(jnp.where(lower, yz1, yz0), HEAD_DIM, 1)
        b_end_t = _bf(st_ref[ST_BEND, :, sl].T)
        k_end_t = _bf(st_ref[ST_KEND, :, sl].T)
        g_mat = (jnp.where(same_head, _dot(b_end_t, _bf(a_bar)), 0.0)
                 + jnp.where(diag, e_tot_all[:, sl], 0.0))
        h_add = jnp.where(same_head, _dot(b_end_t, _bf(uv)) + _dot(k_end_t, v_b[p]), 0.0)
        h_old = _bf(h_ref[p])
        y_ref[:, sl] = _dot(_bf(r_bar), h_old) + yv
        h_ref[p] = _dot(_bf(g_mat), h_old) + h_add

    y = y_ref[...]
    inv_n = 1.0 / HEAD_DIM
    mu = head_sum(y) * inv_n
    d = y - mu
    var = head_sum(d * d) * inv_n
    y = d * lax.rsqrt(var + RWKV_GN_EPS) * lnw_ref[...] + lnb_ref[...]
    y = y + head_sum(r * k2 * rk_ref[...]) * v
    o_ref[...] = (y * g).astype(o_ref.dtype)


def _rwkv_mixer(proj, bsz, t, vfirst, mix, mixm, w0, wup3, a0, aup, gup, v0, vup, k_k, k_a,
                r_k, ln_w, ln_b, bd):
    nc = t // CHUNK
    rows = lambda b, c: b * nc + c
    const = lambda x: pl.BlockSpec(x.shape, lambda b, c: (0,) * x.ndim)
    has_vres = vfirst is not None
    act_specs = [
        pl.BlockSpec((CHUNK, COLBLK), lambda b, c: (rows(b, c), BLK_R)),
        pl.BlockSpec((CHUNK, COLBLK), lambda b, c: (rows(b, c), BLK_K)),
        pl.BlockSpec((CHUNK, COLBLK), lambda b, c: (rows(b, c), BLK_V)),
        pl.BlockSpec((CHUNK, SMALLBLK), lambda b, c: (rows(b, c), BLK_MISC)),
    ]
    wide_spec = pl.BlockSpec((CHUNK, RWKV_WIDTH), lambda b, c: (rows(b, c), 0))
    if has_vres:
        args = [proj, proj, proj, proj, vfirst, mix, mixm, w0, wup3, a0, aup, gup, v0, vup,
                k_k, k_a, r_k, ln_w, ln_b, bd]
        in_specs = act_specs + [wide_spec] + [const(x) for x in args[5:]]
        out_specs = wide_spec
        out_shape = jax.ShapeDtypeStruct((bsz * t, RWKV_WIDTH), BF16)
    else:
        args = [proj, proj, proj, proj, mix, mixm, w0, wup3, a0, aup, gup,
                k_k, k_a, r_k, ln_w, ln_b, bd]
        in_specs = act_specs + [const(x) for x in args[4:]]
        out_specs = (wide_spec, wide_spec)
        out_shape = (jax.ShapeDtypeStruct((bsz * t, RWKV_WIDTH), BF16),
                     jax.ShapeDtypeStruct((bsz * t, RWKV_WIDTH), F32))
    return pl.pallas_call(
        functools.partial(_rwkv_body, has_vres),
        grid=(bsz, nc),
        in_specs=in_specs,
        out_specs=out_specs,
        out_shape=out_shape,
        scratch_shapes=[
            pltpu.VMEM((HALO + CHUNK, RWKV_WIDTH), F32),
            pltpu.VMEM((HALO + CHUNK, RWKV_WIDTH), F32),
            pltpu.VMEM((HALO + CHUNK, RWKV_WIDTH), F32),
            pltpu.VMEM((HALO + CHUNK, SMALLBLK), F32),
            pltpu.VMEM((ST_N, CHUNK, RWKV_WIDTH), F32),
            pltpu.VMEM((PAIRS, LANES, LANES), F32),
            pltpu.VMEM((CHUNK, RWKV_WIDTH), F32),
        ],
        compiler_params=_params(("parallel", "arbitrary")),
        name="rwkv_mixer",
    )(*args)


TOK_ROWS = D_MODEL // LANES


def _store_token_major(ref, val):
    bm = val.shape[0]
    for s in range(TOK_ROWS):
        ref[pl.ds(s, bm, stride=TOK_ROWS), :] = val[:, s * LANES:(s + 1) * LANES]


def _load_token_major(ref, first_row, bm):
    return [ref[pl.ds(first_row + s, bm, stride=TOK_ROWS), :] for s in range(TOK_ROWS)]


def _outproj_ln_body(emit_rows, ys_ref, yr_ref, wt_ref, wb_ref, x_ref, g_ref, b_ref, of_ref, ob_ref,
                     *rows_ref):
    mixed = _dot(ys_ref[...], wt_ref[...]) + _dot(yr_ref[...], wb_ref[...])
    out = _layer_norm(DEEPNORM_ALPHA * x_ref[...] + mixed, g_ref[...], b_ref[...])
    of_ref[...] = out
    ob_ref[...] = out.astype(BF16)
    if emit_rows:
        _store_token_major(rows_ref[0], out)


def _outproj_ln(ys, yr, w_out, x, g, b, bm, emit_rows):
    n = x.shape[0]
    half = w_out.shape[0] // 2
    row = lambda width: pl.BlockSpec((bm, width), lambda i: (i, 0))
    vec = pl.BlockSpec((1, D_MODEL), lambda i: (0, 0))
    out_specs = [row(D_MODEL), row(D_MODEL)]
    out_shape = [jax.ShapeDtypeStruct((n, D_MODEL), F32), jax.ShapeDtypeStruct((n, D_MODEL), BF16)]
    if emit_rows:
        out_specs.append(pl.BlockSpec((bm * TOK_ROWS, LANES), lambda i: (i, 0)))
        out_shape.append(jax.ShapeDtypeStruct((n * TOK_ROWS, LANES), F32))
    return pl.pallas_call(
        functools.partial(_outproj_ln_body, emit_rows),
        grid=(n // bm,),
        in_specs=[row(half), row(half),
                  pl.BlockSpec((half, D_MODEL), lambda i: (0, 0), pipeline_mode=pl.Buffered(1)),
                  pl.BlockSpec((half, D_MODEL), lambda i: (1, 0), pipeline_mode=pl.Buffered(1)),
                  row(D_MODEL), vec, vec],
        out_specs=tuple(out_specs),
        out_shape=tuple(out_shape),
        compiler_params=_params(("parallel",)),
        name="out_proj_ln",
    )(ys, yr, w_out, w_out, x, g, b)


def _down_ln_body(h_ref, w_ref, x_ref, g_ref, b_ref, of_ref, ob_ref, acc_ref):
    kk = pl.program_id(1)

    @pl.when(kk == 0)
    def _zero():
        acc_ref[...] = jnp.zeros_like(acc_ref)

    acc_ref[...] += _dot(h_ref[...], w_ref[...])

    @pl.when(kk == pl.num_programs(1) - 1)
    def _finish():
        out = _layer_norm(DEEPNORM_ALPHA * x_ref[...] + acc_ref[...], g_ref[...], b_ref[...])
        of_ref[...] = out
        ob_ref[...] = out.astype(BF16)


def _down_ln(h, w, x, g, b, bm, bk):
    n, kdim = h.shape
    row = pl.BlockSpec((bm, D_MODEL), lambda i, k: (i, 0))
    vec = pl.BlockSpec((1, D_MODEL), lambda i, k: (0, 0))
    return pl.pallas_call(
        _down_ln_body,
        grid=(n // bm, kdim // bk),
        in_specs=[pl.BlockSpec((bm, bk), lambda i, k: (i, k)),
                  pl.BlockSpec((bk, D_MODEL), lambda i, k: (k, 0)),
                  row, vec, vec],
        out_specs=(row, row),
        out_shape=(jax.ShapeDtypeStruct((n, D_MODEL), F32),
                   jax.ShapeDtypeStruct((n, D_MODEL), BF16)),
        scratch_shapes=[pltpu.VMEM((bm, D_MODEL), F32)],
        compiler_params=_params(("parallel", "arbitrary")),
        name="down_proj_ln",
    )(h, w, x, g, b)


def _swiglu_up_body(x_ref, w1_ref, w3_ref, o_ref):
    xb = x_ref[...]
    h1 = _dot(xb, w1_ref[...])
    h3 = _dot(xb, w3_ref[...])
    o_ref[...] = (h1 * jax.nn.sigmoid(h1) * h3).astype(o_ref.dtype)


def _swiglu_up(xb, w1, w3, bm, bn):
    n = xb.shape[0]
    f = w1.shape[1]
    return pl.pallas_call(
        _swiglu_up_body,
        grid=(n // bm, f // bn),
        in_specs=[pl.BlockSpec((bm, D_MODEL), lambda i, j: (i, 0)),
                  pl.BlockSpec((D_MODEL, bn), lambda i, j: (0, j)),
                  pl.BlockSpec((D_MODEL, bn), lambda i, j: (0, j))],
        out_specs=pl.BlockSpec((bm, bn), lambda i, j: (i, j)),
        out_shape=jax.ShapeDtypeStruct((n, f), BF16),
        compiler_params=_params(("parallel", "parallel")),
        name="swiglu_up",
    )(xb, w1, w3)


def _router_body(x_ref, wr3_ref, o_ref, cnt_ref, run_ref):
    i = pl.program_id(0)

    @pl.when(i == 0)
    def _init():
        run_ref[...] = jnp.zeros_like(run_ref)

    x_hi, x_lo = _split2(x_ref[...])
    logits = _dot(jnp.concatenate([x_hi, x_hi, x_lo], axis=1), wr3_ref[...])
    lane = lax.broadcasted_iota(jnp.int32, logits.shape, 1)
    neg = jnp.float32(-jnp.inf)
    logits = jnp.where(lane < N_EXPERTS, logits, neg)
    top1 = jnp.max(logits, axis=-1, keepdims=True)
    idx1 = jnp.min(jnp.where(logits == top1, lane, LANES), axis=-1, keepdims=True)
    rest = jnp.where(lane == idx1, neg, logits)
    top2 = jnp.max(rest, axis=-1, keepdims=True)
    idx2 = jnp.min(jnp.where(rest == top2, lane, LANES), axis=-1, keepdims=True)
    e2 = jnp.exp(top2 - top1)
    gate1 = 1.0 / (1.0 + e2)
    gate2 = e2 / (1.0 + e2)
    first = (lane == idx1).astype(F32)
    second = (lane == idx2).astype(F32)
    both = first + second
    bm = both.shape[0]
    later = (lax.broadcasted_iota(jnp.int32, (bm, bm), 0) > lax.broadcasted_iota(jnp.int32, (bm, bm), 1))
    earlier = _dot(later.astype(BF16), _bf(both)) + run_ref[0:1, :]
    rank1 = jnp.sum(first * earlier, axis=-1, keepdims=True)
    rank2 = jnp.sum(second * earlier, axis=-1, keepdims=True)
    run_ref[...] = run_ref[...] + jnp.sum(both, axis=0, keepdims=True)
    cnt_ref[...] = run_ref[...]
    o_ref[...] = (jnp.where(lane == 0, idx1.astype(F32), 0.0) + jnp.where(lane == 1, idx2.astype(F32), 0.0)
                  + jnp.where(lane == 2, gate1, 0.0) + jnp.where(lane == 3, gate2, 0.0)
                  + jnp.where(lane == 4, rank1, 0.0) + jnp.where(lane == 5, rank2, 0.0))


def _router(x, wr3, bm):
    n = x.shape[0]
    return pl.pallas_call(
        _router_body,
        grid=(n // bm,),
        in_specs=[pl.BlockSpec((bm, D_MODEL), lambda i: (i, 0)),
                  pl.BlockSpec((3 * D_MODEL, LANES), lambda i: (0, 0))],
        out_specs=(pl.BlockSpec((bm, LANES), lambda i: (i, 0)),
                   pl.BlockSpec((SUBLANES, LANES), lambda i: (0, 0))),
        out_shape=(jax.ShapeDtypeStruct((n, LANES), F32), jax.ShapeDtypeStruct((SUBLANES, LANES), F32)),
        scratch_shapes=[pltpu.VMEM((SUBLANES, LANES), F32)],
        compiler_params=_params(("arbitrary",)),
        name="moe_router",
    )(x, wr3)


MOE_BM = 512


def _row_copy(src_hbm, src_tok, dst_buf, dst_tok, sem):
    return pltpu.make_async_copy(
        src_hbm.at[pl.ds(pl.multiple_of(src_tok * TOK_ROWS, TOK_ROWS), TOK_ROWS), :],
        dst_buf.at[pl.ds(pl.multiple_of(dst_tok * TOK_ROWS, TOK_ROWS), TOK_ROWS), :], sem)


DISPATCH_BM = 512
DISPATCH_UNROLL = 8


def _moe_dispatch_body(n_tok, bm, dest_ref, pad_lo_ref, pad_hi_ref, x_ref, o_hbm, zero_buf, sem):
    i = pl.program_id(0)
    base = i * bm

    def copies(j):
        return [_row_copy(x_ref, j, o_hbm, dest_ref[slot * n_tok + base + j], sem) for slot in range(2)]

    def issue(j, carry):
        for cp in copies(j):
            cp.start()
        return carry

    def wait(j, carry):
        for cp in copies(j):
            cp.wait()
        return carry

    lax.fori_loop(0, bm, issue, 0, unroll=DISPATCH_UNROLL)
    lax.fori_loop(0, bm, wait, 0, unroll=DISPATCH_UNROLL)

    @pl.when(i == 0)
    def _zero_padding_rows():
        zero_buf[...] = jnp.zeros_like(zero_buf)

        def pad_copy(r):
            return pltpu.make_async_copy(
                zero_buf, o_hbm.at[pl.ds(pl.multiple_of(r * TOK_ROWS, TOK_ROWS), TOK_ROWS), :], sem)

        def z_issue(r, carry):
            pad_copy(r).start()
            return carry

        def z_wait(r, carry):
            pad_copy(r).wait()
            return carry

        for e in range(N_EXPERTS + 1):
            lax.fori_loop(pad_lo_ref[e], pad_hi_ref[e], z_issue, 0)
            lax.fori_loop(pad_lo_ref[e], pad_hi_ref[e], z_wait, 0)


def _moe_dispatch(dest, pad_lo, pad_hi, rows, n_tok, n_rows):
    bm = DISPATCH_BM if n_tok % DISPATCH_BM == 0 else n_tok
    return pl.pallas_call(
        functools.partial(_moe_dispatch_body, n_tok, bm),
        grid_spec=pltpu.PrefetchScalarGridSpec(
            num_scalar_prefetch=3,
            grid=(n_tok // bm,),
            in_specs=[pl.BlockSpec((bm * TOK_ROWS, LANES), lambda i, dest, lo, hi: (i, 0))],
            out_specs=pl.BlockSpec(memory_space=pl.ANY),
            scratch_shapes=[pltpu.VMEM((TOK_ROWS, LANES), F32), pltpu.SemaphoreType.DMA(())],
        ),
        out_shape=jax.ShapeDtypeStruct((n_rows * TOK_ROWS, LANES), F32),
        compiler_params=_params(("arbitrary",)),
        name="moe_dispatch",
    )(dest, pad_lo, pad_hi, rows)


def _moe_up_body(te_ref, valid_ref, x_ref, w1_ref, w3_ref, o_ref):
    i = pl.program_id(1)

    @pl.when(valid_ref[i] != 0)
    def _compute():
        xb = jnp.concatenate([_bf(p) for p in _load_token_major(x_ref, 0, MOE_BM)], axis=1)
        h1 = _dot(xb, w1_ref[...])
        h3 = _dot(xb, w3_ref[...])
        o_ref[...] = (h1 * jax.nn.sigmoid(h1) * h3).astype(o_ref.dtype)

    @pl.when(valid_ref[i] == 0)
    def _empty():
        o_ref[...] = jnp.zeros_like(o_ref)


def _moe_up(te, valid, xs_rows, w1, w3, layer, bn):
    n_rows = xs_rows.shape[0] // TOK_ROWS
    fe = w1.shape[3]
    wspec = pl.BlockSpec((None, None, D_MODEL, bn), lambda j, i, te, valid: (layer, te[i], 0, j))
    return pl.pallas_call(
        _moe_up_body,
        grid_spec=pltpu.PrefetchScalarGridSpec(
            num_scalar_prefetch=2,
            grid=(fe // bn, n_rows // MOE_BM),
            in_specs=[pl.BlockSpec((MOE_BM * TOK_ROWS, LANES), lambda j, i, te, valid: (i, 0)),
                      wspec, wspec],
            out_specs=pl.BlockSpec((MOE_BM, bn), lambda j, i, te, valid: (i, j)),
        ),
        out_shape=jax.ShapeDtypeStruct((n_rows, fe), BF16),
        compiler_params=_params(("arbitrary", "arbitrary")),
        name="moe_up",
    )(te, valid, xs_rows, w1, w3)


def _moe_down_body(te_ref, valid_ref, h_ref, w_ref, o_ref):
    i = pl.program_id(0)

    @pl.when(valid_ref[i] != 0)
    def _compute():
        _store_token_major(o_ref, _dot(h_ref[...], w_ref[...]))

    @pl.when(valid_ref[i] == 0)
    def _empty():
        o_ref[...] = jnp.zeros_like(o_ref)


def _moe_down(te, valid, h, w2, layer):
    n_rows, fe = h.shape
    return pl.pallas_call(
        _moe_down_body,
        grid_spec=pltpu.PrefetchScalarGridSpec(
            num_scalar_prefetch=2,
            grid=(n_rows // MOE_BM,),
            in_specs=[pl.BlockSpec((MOE_BM, fe), lambda i, te, valid: (i, 0)),
                      pl.BlockSpec((None, None, fe, D_MODEL), lambda i, te, valid: (layer, te[i], 0, 0))],
            out_specs=pl.BlockSpec((MOE_BM * TOK_ROWS, LANES), lambda i, te, valid: (i, 0)),
        ),
        out_shape=jax.ShapeDtypeStruct((n_rows * TOK_ROWS, LANES), F32),
        compiler_params=_params(("arbitrary",)),
        name="moe_down",
    )(te, valid, h, w2)


def _moe_combine_ln_body(n_tok, dest_ref, ys_hbm, route_ref, x_ref, g_ref, b_ref, of_ref, ob_ref, buf,
                         sem):
    bm = x_ref.shape[0]
    base = pl.program_id(0) * bm

    def copies(j):
        return [_row_copy(ys_hbm, dest_ref[slot * n_tok + base + j], buf, slot * bm + j, sem)
                for slot in range(2)]

    def issue(j, carry):
        for cp in copies(j):
            cp.start()
        return carry

    def wait(j, carry):
        for cp in copies(j):
            cp.wait()
        return carry

    lax.fori_loop(0, bm, issue, 0, unroll=DISPATCH_UNROLL)
    lax.fori_loop(0, bm, wait, 0, unroll=DISPATCH_UNROLL)
    route = route_ref[...]
    gate1, gate2 = route[:, 2:3], route[:, 3:4]
    first = _load_token_major(buf, 0, bm)
    second = _load_token_major(buf, bm * TOK_ROWS, bm)
    f = jnp.concatenate([gate1 * p + gate2 * q for p, q in zip(first, second)], axis=1)
    out = _layer_norm(DEEPNORM_ALPHA * x_ref[...] + f, g_ref[...], b_ref[...])
    of_ref[...] = out
    ob_ref[...] = out.astype(BF16)


def _moe_combine_ln(dest, ys, route, x, g, b, bm):
    n = x.shape[0]
    row = pl.BlockSpec((bm, D_MODEL), lambda i, dest: (i, 0))
    vec = pl.BlockSpec((1, D_MODEL), lambda i, dest: (0, 0))
    route_spec = pl.BlockSpec((bm, LANES), lambda i, dest: (i, 0))
    return pl.pallas_call(
        functools.partial(_moe_combine_ln_body, n),
        grid_spec=pltpu.PrefetchScalarGridSpec(
            num_scalar_prefetch=1,
            grid=(n // bm,),
            in_specs=[pl.BlockSpec(memory_space=pl.ANY), route_spec, row, vec, vec],
            out_specs=(row, row),
            scratch_shapes=[pltpu.VMEM((2 * bm * TOK_ROWS, LANES), F32), pltpu.SemaphoreType.DMA(())],
        ),
        out_shape=(jax.ShapeDtypeStruct((n, D_MODEL), F32), jax.ShapeDtypeStruct((n, D_MODEL), BF16)),
        compiler_params=_params(("arbitrary",)),
        name="moe_combine_ln",
    )(dest, ys, route, x, g, b)


def _moe_dispatch_plan(route, counts, n):
    n_rows = 2 * n + N_EXPERTS * MOE_BM
    expert = route[:, 0:2].astype(jnp.int32)
    rank = route[:, 4:6].astype(jnp.int32)
    count = counts[0, :N_EXPERTS].astype(jnp.int32)
    padded = (count + MOE_BM - 1) // MOE_BM * MOE_BM
    ends = jnp.cumsum(padded)
    starts = ends - padded
    dest = (jnp.take(starts, expert) + rank).T.reshape(-1)
    tile_start = jnp.arange(n_rows // MOE_BM, dtype=jnp.int32) * MOE_BM
    te = jnp.minimum(jnp.sum((tile_start[:, None] >= ends[None, :]).astype(jnp.int32), axis=1),
                     N_EXPERTS - 1)
    valid = (tile_start < ends[-1]).astype(jnp.int32)
    i32 = lambda v: v.astype(jnp.int32)
    pad_lo = jnp.concatenate([starts + count, ends[-1:]])
    pad_hi = jnp.concatenate([ends, jnp.full((1,), n_rows, ends.dtype)])
    return i32(dest), i32(te), valid, i32(pad_lo), i32(pad_hi), n_rows


def _pad_lanes(v, width):
    v = v.reshape(1, -1)
    return jnp.pad(v, ((0, 0), (0, width - v.shape[1])))


def _pick(n, pref):
    return pref if n % pref == 0 else n


def kernel(x, w_in, w_in_vres, ssd_conv_w, ssd_conv_b, ssd_dt_bias, ssd_a_log, ssd_d, ssd_norm_w, rw_mix, rw_vres_mix, rw_w0, rw_w_up, rw_a0, rw_a_up, rw_v0, rw_v_up, rw_g_up, rw_k_k, rw_k_a, rw_r_k, rw_ln_w, rw_ln_b, w_out, ln1_g, ln1_b, ln2_g, ln2_b, ffn_w1, ffn_w3, ffn_w2, moe_router, moe_w1, moe_w3, moe_w2):
    bsz, t, d = x.shape
    n = bsz * t
    depth = w_in.shape[0]
    assert d == D_MODEL and t % CHUNK == 0
    xf = x.reshape(n, d)
    xb = xf

    head_of_lane = jnp.arange(RWKV_WIDTH) // HEAD_DIM
    expand = (jnp.arange(LANES)[:, None] == head_of_lane[None, :]).astype(F32)
    blk_head = jnp.arange(MXU_WIDTH) // HEAD_DIM
    head_ones = (blk_head[:, None] == blk_head[None, :]).astype(BF16)
    head_ones = jnp.concatenate([head_ones, head_ones], axis=0)

    bm_big = _pick(n, 1024)
    bm_mid = _pick(n, 512)
    bm_small = _pick(n, 256)
    row = lambda v: v.reshape(1, -1)
    zcols = lambda k: jnp.zeros((d, k), F32)

    moe_w1_b, moe_w3_b, moe_w2_b = moe_w1.astype(BF16), moe_w3.astype(BF16), moe_w2.astype(BF16)

    vfirst = None
    for l in range(depth):
        w = w_in[l]
        o = 0
        segs = {}
        for name, width in (("z", SSD_WIDTH), ("xs", SSD_WIDTH), ("bc", 2 * SSD_GROUPS * SSD_STATE),
                            ("dt", SSD_HEADS), ("r", RWKV_WIDTH), ("k", RWKV_WIDTH), ("v", RWKV_WIDTH),
                            ("w_lo", W_LORA), ("a_lo", A_LORA), ("g_lo", G_LORA)):
            segs[name] = w[:, o:o + width]
            o += width
        v_lo_cols = w_in_vres[l - 1] if l > 0 else zcols(V_LORA)
        misc_cols = jnp.concatenate(
            [segs["dt"], zcols(MISC_WA - SSD_HEADS), segs["w_lo"], segs["a_lo"], segs["g_lo"], v_lo_cols,
             zcols(SMALLBLK - MISC_GV - G_LORA - V_LORA)], axis=1)
        w_comb = jnp.concatenate([segs["z"], segs["xs"], segs["r"], segs["k"], segs["v"], segs["bc"],
                                  misc_cols], axis=1).astype(BF16)

        proj = _matmul(xb, w_comb, bm_big, COLBLK, F32)

        cw, cb = ssd_conv_w[l], ssd_conv_b[l]
        y_ssd = _ssd_mixer(
            proj, bsz, t, cw[:, :SSD_WIDTH], row(cb[:SSD_WIDTH]), cw[:, SSD_WIDTH:], row(cb[SSD_WIDTH:]),
            _pad_lanes(ssd_dt_bias[l], LANES), _pad_lanes(ssd_a_log[l], LANES),
            row(jnp.repeat(ssd_d[l], HEAD_DIM)), row(ssd_norm_w[l]), expand.astype(BF16))

        mix = rw_mix[l]
        v_mix = rw_vres_mix[l - 1] if l > 0 else jnp.zeros((V_LORA,), F32)
        mix_misc = jnp.concatenate([jnp.zeros((MISC_WA,), F32), mix[3 * RWKV_WIDTH:], v_mix,
                                    jnp.zeros((SMALLBLK - MISC_GV - G_LORA - V_LORA,), F32)])
        wup = jnp.pad(rw_w_up[l], ((0, LANES - W_LORA), (0, 0)))
        wup_hi = wup.astype(BF16)
        wup_lo = (wup - wup_hi.astype(F32)).astype(BF16)
        wup3 = jnp.concatenate([wup_hi, wup_lo, wup_hi], axis=0)
        aup = jnp.pad(rw_a_up[l], ((W_LORA, 0), (0, 0))).astype(BF16)
        gup = jnp.pad(rw_g_up[l], ((0, 2 * LANES - G_LORA), (0, 0))).astype(BF16)
        if l > 0:
            v0 = row(rw_v0[l - 1])
            vup = jnp.pad(rw_v_up[l - 1], ((G_LORA, 2 * LANES - G_LORA - V_LORA), (0, 0))).astype(BF16)
        else:
            v0 = vup = None
        res = _rwkv_mixer(
            proj, bsz, t, vfirst, row(mix[:3 * RWKV_WIDTH]), row(mix_misc), row(rw_w0[l]), wup3,
            row(rw_a0[l]), aup, gup, v0, vup, row(rw_k_k[l]), row(rw_k_a[l]), row(rw_r_k[l]),
            row(rw_ln_w[l]), row(rw_ln_b[l]), head_ones)
        if l == 0:
            y_rw, vfirst = res
        else:
            y_rw = res

        is_moe = l % 2 == 1
        res = _outproj_ln(y_ssd, y_rw, w_out[l].astype(BF16), xf, row(ln1_g[l]), row(ln1_b[l]), bm_mid,
                          emit_rows=is_moe)
        i = l // 2
        if not is_moe:
            xf, xb = res
            h = _swiglu_up(xb, ffn_w1[i].astype(BF16), ffn_w3[i].astype(BF16), bm_big, 512)
            xf, xb = _down_ln(h, ffn_w2[i].astype(BF16), xf, row(ln2_g[l]), row(ln2_b[l]), bm_mid, 1408)
        else:
            xf, xb, x_rows = res
            wr = jnp.pad(moe_router[i], ((0, 0), (0, LANES - N_EXPERTS)))
            wr_hi = wr.astype(BF16)
            wr_lo = (wr - wr_hi.astype(F32)).astype(BF16)
            route, counts = _router(xf, jnp.concatenate([wr_hi, wr_lo, wr_hi], axis=0), bm_mid)
            dest, te, valid, pad_lo, pad_hi, n_rows = _moe_dispatch_plan(route, counts, n)
            xs_rows = _moe_dispatch(dest, pad_lo, pad_hi, x_rows, n, n_rows)
            h = _moe_up(te, valid, xs_rows, moe_w1_b, moe_w3_b, i, 1408)
            ys = _moe_down(te, valid, h, moe_w2_b, i)
            xf, xb = _moe_combine_ln(dest, ys, route, xf, row(ln2_g[l]), row(ln2_b[l]), bm_small)

    return xf.reshape(bsz, t, d)
```

```python
import functools

import jax
import jax.numpy as jnp
from jax import lax
from jax.experimental import pallas as pl
from jax.experimental.pallas import tpu as pltpu

F32 = jnp.float32
BF16 = jnp.bfloat16

D_MODEL = 2048
HEAD_DIM = 64
SSD_WIDTH = 1024
RWKV_WIDTH = 1024
SSD_HEADS = SSD_WIDTH // HEAD_DIM
SSD_GROUPS = 2
SSD_STATE = 128
SSD_CONV_WIDTH = 4
RWKV_HEADS = RWKV_WIDTH // HEAD_DIM
W_LORA = 64
A_LORA = 64
V_LORA = 32
G_LORA = 160
N_EXPERTS = 8
DEPTH = 4
DEEPNORM_ALPHA = (2 * DEPTH) ** 0.25
LN_EPS = 1e-5
RMS_EPS = 1e-5
RWKV_GN_EPS = 64e-5
L2_EPS = 1e-12

LANES = 128
SUBLANES = 8
VMEM_LIMIT = 56 * 1024 * 1024

CHUNK = 128
HALO = SUBLANES

COLBLK = 1024
BLK_Z, BLK_XS, BLK_R, BLK_K, BLK_V = 0, 1, 2, 3, 4
SMALLBLK = 512
BLK_BC = 10
BLK_MISC = 11
PROJ_WIDTH = 6 * COLBLK
MISC_WA = 128
MISC_GV = 256


def _params(sem, vmem=VMEM_LIMIT):
    return pltpu.CompilerParams(dimension_semantics=sem, vmem_limit_bytes=vmem)


def _dot(a, b):
    return jnp.dot(a, b, preferred_element_type=F32)


def _dot_nt(a, b):
    return lax.dot_general(a, b, (((1,), (1,)), ((), ())), preferred_element_type=F32)


def _bf(x):
    return x.astype(BF16)


def _softplus(x):
    return jnp.maximum(x, 0.0) + jnp.log1p(jnp.exp(-jnp.abs(x)))


def _layer_norm(y, g, b):
    mu = jnp.mean(y, axis=-1, keepdims=True)
    d = y - mu
    var = jnp.mean(d * d, axis=-1, keepdims=True)
    return d * lax.rsqrt(var + LN_EPS) * g + b


def _mm_body(x_ref, w_ref, o_ref):
    o_ref[...] = _dot(_bf(x_ref[...]), w_ref[...]).astype(o_ref.dtype)


def _matmul(x, w, bm, bn, out_dtype):
    m, k = x.shape
    n = w.shape[1]
    return pl.pallas_call(
        _mm_body,
        grid=(m // bm, n // bn),
        in_specs=[pl.BlockSpec((bm, k), lambda i, j: (i, 0)),
                  pl.BlockSpec((k, bn), lambda i, j: (0, j))],
        out_specs=pl.BlockSpec((bm, bn), lambda i, j: (i, j)),
        out_shape=jax.ShapeDtypeStruct((m, n), out_dtype),
        compiler_params=_params(("parallel", "parallel")),
        name="in_proj",
    )(x, w)


def _ssd_body(z_ref, xs_ref, bc_ref, misc_ref, cwx_ref, cbx_ref, cwb_ref, cbb_ref, dtb_ref,
              alog_ref, dskip_ref, nw_ref, e_ref, o_ref, xs_buf, bc_buf, h_ref, y_ref):
    L = CHUNK
    c = pl.program_id(1)

    @pl.when(c == 0)
    def _init():
        xs_buf[0:HALO, :] = jnp.zeros((HALO, xs_buf.shape[1]), F32)
        bc_buf[0:HALO, :] = jnp.zeros((HALO, bc_buf.shape[1]), F32)
        h_ref[...] = jnp.zeros_like(h_ref)

    xs_buf[HALO:HALO + L, :] = xs_ref[...]
    bc_buf[HALO:HALO + L, :] = bc_ref[...]

    def conv_silu(buf, w_ref, b_ref):
        acc = b_ref[...]
        for i in range(SSD_CONV_WIDTH):
            acc = acc + buf[pl.ds(HALO - (SSD_CONV_WIDTH - 1) + i, L), :] * w_ref[i:i + 1, :]
        return acc * jax.nn.sigmoid(acc)

    xs = conv_silu(xs_buf, cwx_ref, cbx_ref)
    bcv = conv_silu(bc_buf, cwb_ref, cbb_ref)
    xs_buf[0:HALO, :] = xs_buf[L:L + HALO, :]
    bc_buf[0:HALO, :] = bc_buf[L:L + HALO, :]

    lane = lax.broadcasted_iota(jnp.int32, (1, LANES), 1)
    dt = _softplus(misc_ref[:, 0:LANES] + dtb_ref[...])
    a_neg = jnp.where(lane < SSD_HEADS, -jnp.exp(alog_ref[...]), 0.0)
    d_a = dt * a_neg
    row = lax.broadcasted_iota(jnp.int32, (L, L), 0)
    col = lax.broadcasted_iota(jnp.int32, (L, L), 1)
    causal = row >= col
    tri = causal.astype(BF16)
    a_cum = sum(_dot(tri, t) for t in _split3(d_a))
    a_cum_t = a_cum.T
    a_last = a_cum[L - 1:L, :]
    per_head = jnp.concatenate([dt, jnp.exp(a_last - a_cum), jnp.exp(a_cum)], axis=0)
    head_to_lanes = e_ref[...]
    full = sum(_dot(t, head_to_lanes) for t in _split3(per_head))
    dt_f, dte_f, ea_f = full[0:L], full[L:2 * L], full[2 * L:3 * L]
    xdt = xs * dt_f
    xdt_b = _bf(xdt)
    xdte_b = _bf(xdt * dte_f)
    lane_l = lax.broadcasted_iota(jnp.int32, (L, LANES), 1)
    lower_half = lane_l < HEAD_DIM
    gw = SSD_WIDTH // SSD_GROUPS
    heads_per_group = SSD_HEADS // SSD_GROUPS
    for g in range(SSD_GROUPS):
        b_g = bcv[:, g * SSD_STATE:(g + 1) * SSD_STATE]
        c_g = _bf(bcv[:, (SSD_GROUPS + g) * SSD_STATE:(SSD_GROUPS + g + 1) * SSD_STATE])
        cb = _dot_nt(c_g, _bf(b_g))
        h_g = h_ref[g]
        y_off = _dot(c_g, _bf(h_g)) * ea_f[:, g * gw:(g + 1) * gw]
        new_state = _dot(_bf(b_g.T), xdte_b[:, g * gw:(g + 1) * gw])
        for pr in range(heads_per_group // 2):
            ms = []
            for e in range(2):
                h = g * heads_per_group + pr * 2 + e
                seg = a_cum[:, h:h + 1] - a_cum_t[h:h + 1, :]
                dec = jnp.where(causal, jnp.exp(jnp.minimum(seg, 0.0)), 0.0)
                ms.append(_bf(cb * dec))
            lhs = jnp.concatenate(ms, axis=1)
            lo = g * gw + pr * LANES
            xp = xdt_b[:, lo:lo + LANES]
            zero = jnp.zeros_like(xp)
            rhs = jnp.concatenate([jnp.where(lower_half, xp, zero),
                                   jnp.where(lower_half, zero, xp)], axis=0)
            y_ref[:, lo:lo + LANES] = _dot(lhs, rhs) + y_off[:, pr * LANES:(pr + 1) * LANES]
        h_ref[g] = h_g * ea_f[L - 1:L, g * gw:(g + 1) * gw] + new_state

    y = y_ref[...] + xs * dskip_ref[...]
    zz = z_ref[...]
    u = y * (zz * jax.nn.sigmoid(zz))
    for g in range(SSD_GROUPS):
        ug = u[:, g * gw:(g + 1) * gw]
        ms_ = jnp.mean(ug * ug, axis=-1, keepdims=True)
        o_ref[:, g * gw:(g + 1) * gw] = (
            ug * lax.rsqrt(ms_ + RMS_EPS) * nw_ref[:, g * gw:(g + 1) * gw]).astype(o_ref.dtype)


def _ssd_mixer(proj, bsz, t, cwx, cbx, cwb, cbb, dtb, alog, dskip, nw, expand):
    nc = t // CHUNK
    rows = lambda b, c: b * nc + c
    const = lambda shape: pl.BlockSpec(shape, lambda b, c: (0,) * len(shape))
    return pl.pallas_call(
        _ssd_body,
        grid=(bsz, nc),
        in_specs=[
            pl.BlockSpec((CHUNK, COLBLK), lambda b, c: (rows(b, c), BLK_Z)),
            pl.BlockSpec((CHUNK, COLBLK), lambda b, c: (rows(b, c), BLK_XS)),
            pl.BlockSpec((CHUNK, SMALLBLK), lambda b, c: (rows(b, c), BLK_BC)),
            pl.BlockSpec((CHUNK, SMALLBLK), lambda b, c: (rows(b, c), BLK_MISC)),
            const(cwx.shape), const(cbx.shape), const(cwb.shape), const(cbb.shape),
            const(dtb.shape), const(alog.shape), const(dskip.shape), const(nw.shape),
            const(expand.shape),
        ],
        out_specs=pl.BlockSpec((CHUNK, SSD_WIDTH), lambda b, c: (rows(b, c), 0)),
        out_shape=jax.ShapeDtypeStruct((bsz * t, SSD_WIDTH), BF16),
        scratch_shapes=[
            pltpu.VMEM((HALO + CHUNK, SSD_WIDTH), F32),
            pltpu.VMEM((HALO + CHUNK, SMALLBLK), F32),
            pltpu.VMEM((SSD_GROUPS, SSD_STATE, SSD_WIDTH // SSD_GROUPS), F32),
            pltpu.VMEM((CHUNK, SSD_WIDTH), F32),
        ],
        compiler_params=_params(("parallel", "arbitrary")),
        name="ssd_mixer",
    )(proj, proj, proj, proj, cwx, cbx, cwb, cbb, dtb, alog, dskip, nw, expand)


PAIRS = RWKV_HEADS // 2
MXU_WIDTH = 256
ST_AMID, ST_RMID, ST_BMID, ST_KMID, ST_AST, ST_RST, ST_BEND, ST_KEND, ST_V, ST_N = range(10)


def _split2(x):
    hi = x.astype(BF16)
    return hi, (x - hi.astype(F32)).astype(BF16)


def _split3(x):
    hi = x.astype(BF16)
    r1 = x - hi.astype(F32)
    mid = r1.astype(BF16)
    return hi, mid, (r1 - mid.astype(F32)).astype(BF16)


def _dot3(a, b):
    a_hi, a_lo = _split2(a)
    b_hi, b_lo = _split2(b)
    return _dot(jnp.concatenate([a_hi, a_hi, a_lo], axis=1), jnp.concatenate([b_hi, b_lo, b_hi], axis=0))


SOLVE_BASE_BLOCK = 2


def _unit_lower_solve(ns, xs):
    L = ns[0].shape[0]
    row = lax.broadcasted_iota(jnp.int32, (L, L), 0)
    col = lax.broadcasted_iota(jnp.int32, (L, L), 1)
    same_block = lambda b: (row // b) == (col // b)
    powers = [jnp.where(same_block(SOLVE_BASE_BLOCK), n, 0.0) for n in ns]
    eye = (row == col).astype(F32)
    invs = [eye + d for d in powers]
    k = 2
    while k < SOLVE_BASE_BLOCK:
        powers = [_dot3(d, d) for d in powers]
        invs = [t + _dot3(t, d) for t, d in zip(invs, powers)]
        k *= 2
    size = SOLVE_BASE_BLOCK
    while size < L:
        joined = jnp.logical_and(same_block(2 * size), jnp.logical_not(same_block(size)))
        offs = [_bf(jnp.where(joined, n, 0.0)) for n in ns]
        invs_b = [_bf(t) for t in invs]
        invs = [t + _dot(_bf(_dot(tb, o)), tb) for t, tb, o in zip(invs, invs_b, offs)]
        size *= 2
    return [_dot(_bf(t), _bf(x)) for t, x in zip(invs, xs)]


def _rwkv_body(has_vres, *refs):
    if has_vres:
        (r_ref, k_ref, v_ref, misc_ref, vfirst_ref, mix_ref, mixm_ref, w0_ref, wup3_ref,
         a0_ref, aup_ref, gup_ref, v0_ref, vup_ref, kk_ref, ka_ref, rk_ref, lnw_ref, lnb_ref, bd_ref,
         o_ref, r_buf, k_buf, v_buf, m_buf, st_ref, h_ref, y_ref) = refs
    else:
        (r_ref, k_ref, v_ref, misc_ref, mix_ref, mixm_ref, w0_ref, wup3_ref,
         a0_ref, aup_ref, gup_ref, kk_ref, ka_ref, rk_ref, lnw_ref, lnb_ref, bd_ref,
         o_ref, vfirst_out_ref, r_buf, k_buf, v_buf, m_buf, st_ref, h_ref, y_ref) = refs
    L = CHUNK
    c = pl.program_id(1)

    @pl.when(c == 0)
    def _init():
        for buf in (r_buf, k_buf, v_buf, m_buf):
            buf[0:HALO, :] = jnp.zeros((HALO, buf.shape[1]), F32)
        h_ref[...] = jnp.zeros_like(h_ref)

    def shift_lerp(ref, buf, mix):
        cur = ref[...]
        buf[HALO:HALO + L, :] = cur
        prev = buf[pl.ds(HALO - 1, L), :]
        buf[0:HALO, :] = buf[L:L + HALO, :]
        return cur + (prev - cur) * mix

    w_ = RWKV_WIDTH
    r = shift_lerp(r_ref, r_buf, mix_ref[:, 0:w_])
    k = shift_lerp(k_ref, k_buf, mix_ref[:, w_:2 * w_])
    v = shift_lerp(v_ref, v_buf, mix_ref[:, 2 * w_:3 * w_])
    m = shift_lerp(misc_ref, m_buf, mixm_ref[...])
    wa = m[:, MISC_WA:MISC_WA + LANES]
    gv = m[:, MISC_GV:MISC_GV + 2 * LANES]

    th_hi, th_lo = _split2(jnp.tanh(wa))
    w_lin = w0_ref[...] + _dot(jnp.concatenate([th_hi, th_hi, th_lo], axis=1), wup3_ref[...])
    log_w = -jnp.exp(-_softplus(-w_lin) - 0.5)
    a = jax.nn.sigmoid(a0_ref[...] + _dot(_bf(wa), aup_ref[...]))
    g = _dot(_bf(jax.nn.sigmoid(gv)), gup_ref[...])
    if has_vres:
        v = v + (vfirst_ref[...] - v) * jax.nn.sigmoid(v0_ref[...] + _dot(_bf(gv), vup_ref[...]))
    else:
        vfirst_out_ref[...] = v

    n_blk = RWKV_WIDTH // MXU_WIDTH

    def head_sum(x):
        hi, lo = _split2(x)
        blocks = [jnp.concatenate([t[:, j * MXU_WIDTH:(j + 1) * MXU_WIDTH] for t in (hi, lo)], axis=1)
                  for j in range(n_blk)]
        s = _dot(jnp.concatenate(blocks, axis=0), bd_ref[...])
        return jnp.concatenate([s[j * L:(j + 1) * L] for j in range(n_blk)], axis=1)

    kx = k * kk_ref[...]
    kk = kx / jnp.maximum(jnp.sqrt(head_sum(kx * kx)), L2_EPS)
    k2 = k * (1.0 + (a - 1.0) * ka_ref[...])
    av = -kk
    bv = kk * a

    row = lax.broadcasted_iota(jnp.int32, (L, L), 0)
    col = lax.broadcasted_iota(jnp.int32, (L, L), 1)
    incl = row >= col
    strict = row > col
    tri = incl.astype(BF16)
    cum = _dot(jnp.concatenate([tri, tri, tri], axis=1), jnp.concatenate(_split3(log_w), axis=0))
    cum_prev = cum - log_w
    tot = cum[L - 1:L, :]
    mid = cum[L // 2 - 1:L // 2, :]
    st_ref[ST_AMID] = av * jnp.exp(cum_prev - mid)
    st_ref[ST_RMID] = r * jnp.exp(cum - mid)
    inv_mid = jnp.exp(mid - cum)
    st_ref[ST_BMID] = bv * inv_mid
    st_ref[ST_KMID] = k2 * inv_mid
    st_ref[ST_AST] = av * jnp.exp(cum_prev)
    st_ref[ST_RST] = r * jnp.exp(cum)
    to_end = jnp.exp(tot - cum)
    st_ref[ST_BEND] = bv * to_end
    st_ref[ST_KEND] = k2 * to_end
    st_ref[ST_V] = v
    e_tot_all = jnp.exp(tot)

    lane_l = lax.broadcasted_iota(jnp.int32, (L, LANES), 1)
    lower = lane_l < HEAD_DIM
    r2 = lax.broadcasted_iota(jnp.int32, (LANES, LANES), 0)
    c2 = lax.broadcasted_iota(jnp.int32, (LANES, LANES), 1)
    same_head = (r2 < HEAD_DIM) == (c2 < HEAD_DIM)
    diag = r2 == c2
    n_rounds = L.bit_length() - 1

    lanes_of = [slice(p * LANES, (p + 1) * LANES) for p in range(PAIRS)]
    masks = (lower, jnp.logical_not(lower))
    heads = [(p, h) for p in range(PAIRS) for h in range(2)]
    v_b, v_sw, bk_t = [], [], []
    for sl in lanes_of:
        v_p = st_ref[ST_V, :, sl]
        v_b.append(_bf(v_p))
        v_sw.append(_bf(pltpu.roll(v_p, HEAD_DIM, 1)))
        bk_t.append(_bf(jnp.concatenate([st_ref[ST_BMID, :, sl].T, st_ref[ST_KMID, :, sl].T], axis=1)))
    zero_b = jnp.zeros_like(v_sw[0])

    mm, a_ak, a_rbk = [], [], []
    for p, h in heads:
        sl, mh = lanes_of[p], masks[h]
        ar = _bf(jnp.concatenate([jnp.where(mh, st_ref[ST_AMID, :, sl], 0.0),
                                  jnp.where(mh, st_ref[ST_RMID, :, sl], 0.0)], axis=0))
        aa = _dot(ar, bk_t[p])
        mm.append(jnp.where(strict, aa[0:L, 0:L], 0.0))
        a_ak.append(_bf(jnp.where(strict, aa[0:L, L:2 * L], 0.0)))
        a_rbk.append(_bf(jnp.concatenate([jnp.where(incl, aa[L:2 * L, 0:L], 0.0),
                                          jnp.where(incl, aa[L:2 * L, L:2 * L], 0.0)], axis=1)))
    xs_ = [jnp.where(masks[h], st_ref[ST_AST, :, lanes_of[p]], _dot(a_ak[i], v_sw[p]))
           for i, (p, h) in enumerate(heads)]
    xs_ = _unit_lower_solve(mm, xs_)
    yz =[_dot(a_rbk[i], jnp.concatenate([_bf(xs_[i]), jnp.where(masks[h], zero_b, v_sw[p])], axis=0))
          for i, (p, h) in enumerate(heads)]

    for p, sl in enumerate(lanes_of):
        x0, x1, yz0, yz1 = xs_[2 * p], xs_[2 * p + 1], yz[2 * p], yz[2 * p + 1]
        a_bar = jnp.where(lower, x0, x1)
        uv = pltpu.roll(jnp.where(lower, x1, x0), HEAD_DIM, 1)
        r_bar = st_ref[ST_RST, :, sl] + jnp.where(lower, yz0, yz1)
        yv = pltpu.roll(jnp.where(lower, yz1, yz0), HEAD_DIM, 1)
        b_end_t = _bf(st_ref[ST_BEND, :, sl].T)
        k_end_t = _bf(st_ref[ST_KEND, :, sl].T)
        g_mat = (jnp.where(same_head, _dot(b_end_t, _bf(a_bar)), 0.0)
                 + jnp.where(diag, e_tot_all[:, sl], 0.0))
        h_add = jnp.where(same_head, _dot(b_end_t, _bf(uv)) + _dot(k_end_t, v_b[p]), 0.0)
        h_old = _bf(h_ref[p])
        y_ref[:, sl] = _dot(_bf(r_bar), h_old) + yv
        h_ref[p] = _dot(_bf(g_mat), h_old) + h_add

    y = y_ref[...]
    inv_n = 1.0 / HEAD_DIM
    mu = head_sum(y) * inv_n
    d = y - mu
    var = head_sum(d * d) * inv_n
    y = d * lax.rsqrt(var + RWKV_GN_EPS) * lnw_ref[...] + lnb_ref[...]
    y = y + head_sum(r * k2 * rk_ref[...]) * v
    o_ref[...] = (y * g).astype(o_ref.dtype)


def _rwkv_mixer(proj, bsz, t, vfirst, mix, mixm, w0, wup3, a0, aup, gup, v0, vup, k_k, k_a,
                r_k, ln_w, ln_b, bd):
    nc = t // CHUNK
    rows = lambda b, c: b * nc + c
    const = lambda x: pl.BlockSpec(x.shape, lambda b, c: (0,) * x.ndim)
    has_vres = vfirst is not None
    act_specs = [
        pl.BlockSpec((CHUNK, COLBLK), lambda b, c: (rows(b, c), BLK_R)),
        pl.BlockSpec((CHUNK, COLBLK), lambda b, c: (rows(b, c), BLK_K)),
        pl.BlockSpec((CHUNK, COLBLK), lambda b, c: (rows(b, c), BLK_V)),
        pl.BlockSpec((CHUNK, SMALLBLK), lambda b, c: (rows(b, c), BLK_MISC)),
    ]
    wide_spec = pl.BlockSpec((CHUNK, RWKV_WIDTH), lambda b, c: (rows(b, c), 0))
    if has_vres:
        args = [proj, proj, proj, proj, vfirst, mix, mixm, w0, wup3, a0, aup, gup, v0, vup,
                k_k, k_a, r_k, ln_w, ln_b, bd]
        in_specs = act_specs + [wide_spec] + [const(x) for x in args[5:]]
        out_specs = wide_spec
        out_shape = jax.ShapeDtypeStruct((bsz * t, RWKV_WIDTH), BF16)
    else:
        args = [proj, proj, proj, proj, mix, mixm, w0, wup3, a0, aup, gup,
                k_k, k_a, r_k, ln_w, ln_b, bd]
        in_specs = act_specs + [const(x) for x in args[4:]]
        out_specs = (wide_spec, wide_spec)
        out_shape = (jax.ShapeDtypeStruct((bsz * t, RWKV_WIDTH), BF16),
                     jax.ShapeDtypeStruct((bsz * t, RWKV_WIDTH), F32))
    return pl.pallas_call(
        functools.partial(_rwkv_body, has_vres),
        grid=(bsz, nc),
        in_specs=in_specs,
        out_specs=out_specs,
        out_shape=out_shape,
        scratch_shapes=[
            pltpu.VMEM((HALO + CHUNK, RWKV_WIDTH), F32),
            pltpu.VMEM((HALO + CHUNK, RWKV_WIDTH), F32),
            pltpu.VMEM((HALO + CHUNK, RWKV_WIDTH), F32),
            pltpu.VMEM((HALO + CHUNK, SMALLBLK), F32),
            pltpu.VMEM((ST_N, CHUNK, RWKV_WIDTH), F32),
            pltpu.VMEM((PAIRS, LANES, LANES), F32),
            pltpu.VMEM((CHUNK, RWKV_WIDTH), F32),
        ],
        compiler_params=_params(("parallel", "arbitrary")),
        name="rwkv_mixer",
    )(*args)


TOK_ROWS = D_MODEL // LANES


def _store_token_major(ref, val):
    bm = val.shape[0]
    for s in range(TOK_ROWS):
        ref[pl.ds(s, bm, stride=TOK_ROWS), :] = val[:, s * LANES:(s + 1) * LANES]


def _load_token_major(ref, first_row, bm):
    return [ref[pl.ds(first_row + s, bm, stride=TOK_ROWS), :] for s in range(TOK_ROWS)]


def _outproj_ln_body(emit_rows, ys_ref, yr_ref, wt_ref, wb_ref, x_ref, g_ref, b_ref, of_ref, ob_ref,
                     *rows_ref):
    mixed = _dot(ys_ref[...], wt_ref[...]) + _dot(yr_ref[...], wb_ref[...])
    out = _layer_norm(DEEPNORM_ALPHA * x_ref[...] + mixed, g_ref[...], b_ref[...])
    of_ref[...] = out
    ob_ref[...] = out.astype(BF16)
    if emit_rows:
        _store_token_major(rows_ref[0], out)


def _outproj_ln(ys, yr, w_out, x, g, b, bm, emit_rows):
    n = x.shape[0]
    half = w_out.shape[0] // 2
    row = lambda width: pl.BlockSpec((bm, width), lambda i: (i, 0))
    vec = pl.BlockSpec((1, D_MODEL), lambda i: (0, 0))
    out_specs = [row(D_MODEL), row(D_MODEL)]
    out_shape = [jax.ShapeDtypeStruct((n, D_MODEL), F32), jax.ShapeDtypeStruct((n, D_MODEL), BF16)]
    if emit_rows:
        out_specs.append(pl.BlockSpec((bm * TOK_ROWS, LANES), lambda i: (i, 0)))
        out_shape.append(jax.ShapeDtypeStruct((n * TOK_ROWS, LANES), F32))
    return pl.pallas_call(
        functools.partial(_outproj_ln_body, emit_rows),
        grid=(n // bm,),
        in_specs=[row(half), row(half),
                  pl.BlockSpec((half, D_MODEL), lambda i: (0, 0), pipeline_mode=pl.Buffered(1)),
                  pl.BlockSpec((half, D_MODEL), lambda i: (1, 0), pipeline_mode=pl.Buffered(1)),
                  row(D_MODEL), vec, vec],
        out_specs=tuple(out_specs),
        out_shape=tuple(out_shape),
        compiler_params=_params(("parallel",)),
        name="out_proj_ln",
    )(ys, yr, w_out, w_out, x, g, b)


def _down_ln_body(h_ref, w_ref, x_ref, g_ref, b_ref, of_ref, ob_ref, acc_ref):
    kk = pl.program_id(1)

    @pl.when(kk == 0)
    def _zero():
        acc_ref[...] = jnp.zeros_like(acc_ref)

    acc_ref[...] += _dot(h_ref[...], w_ref[...])

    @pl.when(kk == pl.num_programs(1) - 1)
    def _finish():
        out = _layer_norm(DEEPNORM_ALPHA * x_ref[...] + acc_ref[...], g_ref[...], b_ref[...])
        of_ref[...] = out
        ob_ref[...] = out.astype(BF16)


def _down_ln(h, w, x, g, b, bm, bk):
    n, kdim = h.shape
    row = pl.BlockSpec((bm, D_MODEL), lambda i, k: (i, 0))
    vec = pl.BlockSpec((1, D_MODEL), lambda i, k: (0, 0))
    return pl.pallas_call(
        _down_ln_body,
        grid=(n // bm, kdim // bk),
        in_specs=[pl.BlockSpec((bm, bk), lambda i, k: (i, k)),
                  pl.BlockSpec((bk, D_MODEL), lambda i, k: (k, 0)),
                  row, vec, vec],
        out_specs=(row, row),
        out_shape=(jax.ShapeDtypeStruct((n, D_MODEL), F32),
                   jax.ShapeDtypeStruct((n, D_MODEL), BF16)),
        scratch_shapes=[pltpu.VMEM((bm, D_MODEL), F32)],
        compiler_params=_params(("parallel", "arbitrary")),
        name="down_proj_ln",
    )(h, w, x, g, b)


def _swiglu_up_body(x_ref, w1_ref, w3_ref, o_ref):
    xb = x_ref[...]
    h1 = _dot(xb, w1_ref[...])
    h3 = _dot(xb, w3_ref[...])
    o_ref[...] = (h1 * jax.nn.sigmoid(h1) * h3).astype(o_ref.dtype)


def _swiglu_up(xb, w1, w3, bm, bn):
    n = xb.shape[0]
    f = w1.shape[1]
    return pl.pallas_call(
        _swiglu_up_body,
        grid=(n // bm, f // bn),
        in_specs=[pl.BlockSpec((bm, D_MODEL), lambda i, j: (i, 0)),
                  pl.BlockSpec((D_MODEL, bn), lambda i, j: (0, j)),
                  pl.BlockSpec((D_MODEL, bn), lambda i, j: (0, j))],
        out_specs=pl.BlockSpec((bm, bn), lambda i, j: (i, j)),
        out_shape=jax.ShapeDtypeStruct((n, f), BF16),
        compiler_params=_params(("parallel", "parallel")),
        name="swiglu_up",
    )(xb, w1, w3)


def _router_body(x_ref, wr3_ref, o_ref, cnt_ref, run_ref):
    i = pl.program_id(0)

    @pl.when(i == 0)
    def _init():
        run_ref[...] = jnp.zeros_like(run_ref)

    x_hi, x_lo = _split2(x_ref[...])
    logits = _dot(jnp.concatenate([x_hi, x_hi, x_lo], axis=1), wr3_ref[...])
    lane = lax.broadcasted_iota(jnp.int32, logits.shape, 1)
    neg = jnp.float32(-jnp.inf)
    logits = jnp.where(lane < N_EXPERTS, logits, neg)
    top1 = jnp.max(logits, axis=-1, keepdims=True)
    idx1 = jnp.min(jnp.where(logits == top1, lane, LANES), axis=-1, keepdims=True)
    rest = jnp.where(lane == idx1, neg, logits)
    top2 = jnp.max(rest, axis=-1, keepdims=True)
    idx2 = jnp.min(jnp.where(rest == top2, lane, LANES), axis=-1, keepdims=True)
    e2 = jnp.exp(top2 - top1)
    gate1 = 1.0 / (1.0 + e2)
    gate2 = e2 / (1.0 + e2)
    first = (lane == idx1).astype(F32)
    second = (lane == idx2).astype(F32)
    both = first + second
    bm = both.shape[0]
    later = (lax.broadcasted_iota(jnp.int32, (bm, bm), 0) > lax.broadcasted_iota(jnp.int32, (bm, bm), 1))
    earlier = _dot(later.astype(BF16), _bf(both)) + run_ref[0:1, :]
    rank1 = jnp.sum(first * earlier, axis=-1, keepdims=True)
    rank2 = jnp.sum(second * earlier, axis=-1, keepdims=True)
    run_ref[...] = run_ref[...] + jnp.sum(both, axis=0, keepdims=True)
    cnt_ref[...] = run_ref[...]
    o_ref[...] = (jnp.where(lane == 0, idx1.astype(F32), 0.0) + jnp.where(lane == 1, idx2.astype(F32), 0.0)
                  + jnp.where(lane == 2, gate1, 0.0) + jnp.where(lane == 3, gate2, 0.0)
                  + jnp.where(lane == 4, rank1, 0.0) + jnp.where(lane == 5, rank2, 0.0))


def _router(x, wr3, bm):
    n = x.shape[0]
    return pl.pallas_call(
        _router_body,
        grid=(n // bm,),
        in_specs=[pl.BlockSpec((bm, D_MODEL), lambda i: (i, 0)),
                  pl.BlockSpec((3 * D_MODEL, LANES), lambda i: (0, 0))],
        out_specs=(pl.BlockSpec((bm, LANES), lambda i: (i, 0)),
                   pl.BlockSpec((SUBLANES, LANES), lambda i: (0, 0))),
        out_shape=(jax.ShapeDtypeStruct((n, LANES), F32), jax.ShapeDtypeStruct((SUBLANES, LANES), F32)),
        scratch_shapes=[pltpu.VMEM((SUBLANES, LANES), F32)],
        compiler_params=_params(("arbitrary",)),
        name="moe_router",
    )(x, wr3)


MOE_BM = 512


def _row_copy(src_hbm, src_tok, dst_buf, dst_tok, sem):
    return pltpu.make_async_copy(
        src_hbm.at[pl.ds(pl.multiple_of(src_tok * TOK_ROWS, TOK_ROWS), TOK_ROWS), :],
        dst_buf.at[pl.ds(pl.multiple_of(dst_tok * TOK_ROWS, TOK_ROWS), TOK_ROWS), :], sem)


DISPATCH_BM = 512
DISPATCH_UNROLL = 8


def _moe_dispatch_body(n_tok, bm, dest_ref, pad_lo_ref, pad_hi_ref, x_ref, o_hbm, zero_buf, sem):
    i = pl.program_id(0)
    base = i * bm

    def copies(j):
        return [_row_copy(x_ref, j, o_hbm, dest_ref[slot * n_tok + base + j], sem) for slot in range(2)]

    def issue(j, carry):
        for cp in copies(j):
            cp.start()
        return carry

    def wait(j, carry):
        for cp in copies(j):
            cp.wait()
        return carry

    lax.fori_loop(0, bm, issue, 0, unroll=DISPATCH_UNROLL)
    lax.fori_loop(0, bm, wait, 0, unroll=DISPATCH_UNROLL)

    @pl.when(i == 0)
    def _zero_padding_rows():
        zero_buf[...] = jnp.zeros_like(zero_buf)

        def pad_copy(r):
            return pltpu.make_async_copy(
                zero_buf, o_hbm.at[pl.ds(pl.multiple_of(r * TOK_ROWS, TOK_ROWS), TOK_ROWS), :], sem)

        def z_issue(r, carry):
            pad_copy(r).start()
            return carry

        def z_wait(r, carry):
            pad_copy(r).wait()
            return carry

        for e in range(N_EXPERTS + 1):
            lax.fori_loop(pad_lo_ref[e], pad_hi_ref[e], z_issue, 0)
            lax.fori_loop(pad_lo_ref[e], pad_hi_ref[e], z_wait, 0)


def _moe_dispatch(dest, pad_lo, pad_hi, rows, n_tok, n_rows):
    bm = DISPATCH_BM if n_tok % DISPATCH_BM == 0 else n_tok
    return pl.pallas_call(
        functools.partial(_moe_dispatch_body, n_tok, bm),
        grid_spec=pltpu.PrefetchScalarGridSpec(
            num_scalar_prefetch=3,
            grid=(n_tok // bm,),
            in_specs=[pl.BlockSpec((bm * TOK_ROWS, LANES), lambda i, dest, lo, hi: (i, 0))],
            out_specs=pl.BlockSpec(memory_space=pl.ANY),
            scratch_shapes=[pltpu.VMEM((TOK_ROWS, LANES), F32), pltpu.SemaphoreType.DMA(())],
        ),
        out_shape=jax.ShapeDtypeStruct((n_rows * TOK_ROWS, LANES), F32),
        compiler_params=_params(("arbitrary",)),
        name="moe_dispatch",
    )(dest, pad_lo, pad_hi, rows)


def _moe_up_body(te_ref, valid_ref, x_ref, w1_ref, w3_ref, o_ref):
    i = pl.program_id(1)

    @pl.when(valid_ref[i] != 0)
    def _compute():
        xb = jnp.concatenate([_bf(p) for p in _load_token_major(x_ref, 0, MOE_BM)], axis=1)
        h1 = _dot(xb, w1_ref[...])
        h3 = _dot(xb, w3_ref[...])
        o_ref[...] = (h1 * jax.nn.sigmoid(h1) * h3).astype(o_ref.dtype)

    @pl.when(valid_ref[i] == 0)
    def _empty():
        o_ref[...] = jnp.zeros_like(o_ref)


def _moe_up(te, valid, xs_rows, w1, w3, layer, bn):
    n_rows = xs_rows.shape[0] // TOK_ROWS
    fe = w1.shape[3]
    wspec = pl.BlockSpec((None, None, D_MODEL, bn), lambda j, i, te, valid: (layer, te[i], 0, j))
    return pl.pallas_call(
        _moe_up_body,
        grid_spec=pltpu.PrefetchScalarGridSpec(
            num_scalar_prefetch=2,
            grid=(fe // bn, n_rows // MOE_BM),
            in_specs=[pl.BlockSpec((MOE_BM * TOK_ROWS, LANES), lambda j, i, te, valid: (i, 0)),
                      wspec, wspec],
            out_specs=pl.BlockSpec((MOE_BM, bn), lambda j, i, te, valid: (i, j)),
        ),
        out_shape=jax.ShapeDtypeStruct((n_rows, fe), BF16),
        compiler_params=_params(("arbitrary", "arbitrary")),
        name="moe_up",
    )(te, valid, xs_rows, w1, w3)


def _moe_down_body(te_ref, valid_ref, h_ref, w_ref, o_ref):
    i = pl.program_id(0)

    @pl.when(valid_ref[i] != 0)
    def _compute():
        _store_token_major(o_ref, _dot(h_ref[...], w_ref[...]))

    @pl.when(valid_ref[i] == 0)
    def _empty():
        o_ref[...] = jnp.zeros_like(o_ref)


def _moe_down(te, valid, h, w2, layer):
    n_rows, fe = h.shape
    return pl.pallas_call(
        _moe_down_body,
        grid_spec=pltpu.PrefetchScalarGridSpec(
            num_scalar_prefetch=2,
            grid=(n_rows // MOE_BM,),
            in_specs=[pl.BlockSpec((MOE_BM, fe), lambda i, te, valid: (i, 0)),
                      pl.BlockSpec((None, None, fe, D_MODEL), lambda i, te, valid: (layer, te[i], 0, 0))],
            out_specs=pl.BlockSpec((MOE_BM * TOK_ROWS, LANES), lambda i, te, valid: (i, 0)),
        ),
        out_shape=jax.ShapeDtypeStruct((n_rows * TOK_ROWS, LANES), F32),
        compiler_params=_params(("arbitrary",)),
        name="moe_down",
    )(te, valid, h, w2)


def _moe_combine_ln_body(n_tok, dest_ref, ys_hbm, route_ref, x_ref, g_ref, b_ref, of_ref, ob_ref, buf,
                         sem):
    bm = x_ref.shape[0]
    base = pl.program_id(0) * bm

    def copies(j):
        return [_row_copy(ys_hbm, dest_ref[slot * n_tok + base + j], buf, slot * bm + j, sem)
                for slot in range(2)]

    def issue(j, carry):
        for cp in copies(j):
            cp.start()
        return carry

    def wait(j, carry):
        for cp in copies(j):
            cp.wait()
        return carry

    lax.fori_loop(0, bm, issue, 0, unroll=DISPATCH_UNROLL)
    lax.fori_loop(0, bm, wait, 0, unroll=DISPATCH_UNROLL)
    route = route_ref[...]
    gate1, gate2 = route[:, 2:3], route[:, 3:4]
    first = _load_token_major(buf, 0, bm)
    second = _load_token_major(buf, bm * TOK_ROWS, bm)
    f = jnp.concatenate([gate1 * p + gate2 * q for p, q in zip(first, second)], axis=1)
    out = _layer_norm(DEEPNORM_ALPHA * x_ref[...] + f, g_ref[...], b_ref[...])
    of_ref[...] = out
    ob_ref[...] = out.astype(BF16)


def _moe_combine_ln(dest, ys, route, x, g, b, bm):
    n = x.shape[0]
    row = pl.BlockSpec((bm, D_MODEL), lambda i, dest: (i, 0))
    vec = pl.BlockSpec((1, D_MODEL), lambda i, dest: (0, 0))
    route_spec = pl.BlockSpec((bm, LANES), lambda i, dest: (i, 0))
    return pl.pallas_call(
        functools.partial(_moe_combine_ln_body, n),
        grid_spec=pltpu.PrefetchScalarGridSpec(
            num_scalar_prefetch=1,
            grid=(n // bm,),
            in_specs=[pl.BlockSpec(memory_space=pl.ANY), route_spec, row, vec, vec],
            out_specs=(row, row),
            scratch_shapes=[pltpu.VMEM((2 * bm * TOK_ROWS, LANES), F32), pltpu.SemaphoreType.DMA(())],
        ),
        out_shape=(jax.ShapeDtypeStruct((n, D_MODEL), F32), jax.ShapeDtypeStruct((n, D_MODEL), BF16)),
        compiler_params=_params(("arbitrary",)),
        name="moe_combine_ln",
    )(dest, ys, route, x, g, b)


def _moe_dispatch_plan(route, counts, n):
    n_rows = 2 * n + N_EXPERTS * MOE_BM
    expert = route[:, 0:2].astype(jnp.int32)
    rank = route[:, 4:6].astype(jnp.int32)
    count = counts[0, :N_EXPERTS].astype(jnp.int32)
    padded = (count + MOE_BM - 1) // MOE_BM * MOE_BM
    ends = jnp.cumsum(padded)
    starts = ends - padded
    dest = (jnp.take(starts, expert) + rank).T.reshape(-1)
    tile_start = jnp.arange(n_rows // MOE_BM, dtype=jnp.int32) * MOE_BM
    te = jnp.minimum(jnp.sum((tile_start[:, None] >= ends[None, :]).astype(jnp.int32), axis=1),
                     N_EXPERTS - 1)
    valid = (tile_start < ends[-1]).astype(jnp.int32)
    i32 = lambda v: v.astype(jnp.int32)
    pad_lo = jnp.concatenate([starts + count, ends[-1:]])
    pad_hi = jnp.concatenate([ends, jnp.full((1,), n_rows, ends.dtype)])
    return i32(dest), i32(te), valid, i32(pad_lo), i32(pad_hi), n_rows


def _pad_lanes(v, width):
    v = v.reshape(1, -1)
    return jnp.pad(v, ((0, 0), (0, width - v.shape[1])))


def _pick(n, pref):
    return pref if n % pref == 0 else n


def kernel(x, w_in, w_in_vres, ssd_conv_w, ssd_conv_b, ssd_dt_bias, ssd_a_log, ssd_d, ssd_norm_w, rw_mix, rw_vres_mix, rw_w0, rw_w_up, rw_a0, rw_a_up, rw_v0, rw_v_up, rw_g_up, rw_k_k, rw_k_a, rw_r_k, rw_ln_w, rw_ln_b, w_out, ln1_g, ln1_b, ln2_g, ln2_b, ffn_w1, ffn_w3, ffn_w2, moe_router, moe_w1, moe_w3, moe_w2):
    bsz, t, d = x.shape
    n = bsz * t
    depth = w_in.shape[0]
    assert d == D_MODEL and t % CHUNK == 0
    xf = x.reshape(n, d)
    xb = xf

    head_of_lane = jnp.arange(RWKV_WIDTH) // HEAD_DIM
    expand = (jnp.arange(LANES)[:, None] == head_of_lane[None, :]).astype(F32)
    blk_head = jnp.arange(MXU_WIDTH) // HEAD_DIM
    head_ones = (blk_head[:, None] == blk_head[None, :]).astype(BF16)
    head_ones = jnp.concatenate([head_ones, head_ones], axis=0)

    bm_big = _pick(n, 1024)
    bm_mid = _pick(n, 512)
    bm_small = _pick(n, 256)
    row = lambda v: v.reshape(1, -1)
    zcols = lambda k: jnp.zeros((d, k), F32)

    moe_w1_b, moe_w3_b, moe_w2_b = moe_w1.astype(BF16), moe_w3.astype(BF16), moe_w2.astype(BF16)

    vfirst = None
    for l in range(depth):
        w = w_in[l]
        o = 0
        segs = {}
        for name, width in (("z", SSD_WIDTH), ("xs", SSD_WIDTH), ("bc", 2 * SSD_GROUPS * SSD_STATE),
                            ("dt", SSD_HEADS), ("r", RWKV_WIDTH), ("k", RWKV_WIDTH), ("v", RWKV_WIDTH),
                            ("w_lo", W_LORA), ("a_lo", A_LORA), ("g_lo", G_LORA)):
            segs[name] = w[:, o:o + width]
            o += width
        v_lo_cols = w_in_vres[l - 1] if l > 0 else zcols(V_LORA)
        misc_cols = jnp.concatenate(
            [segs["dt"], zcols(MISC_WA - SSD_HEADS), segs["w_lo"], segs["a_lo"], segs["g_lo"], v_lo_cols,
             zcols(SMALLBLK - MISC_GV - G_LORA - V_LORA)], axis=1)
        w_comb = jnp.concatenate([segs["z"], segs["xs"], segs["r"], segs["k"], segs["v"], segs["bc"],
                                  misc_cols], axis=1).astype(BF16)

        proj = _matmul(xb, w_comb, bm_big, COLBLK, F32)

        cw, cb = ssd_conv_w[l], ssd_conv_b[l]
        y_ssd = _ssd_mixer(
            proj, bsz, t, cw[:, :SSD_WIDTH], row(cb[:SSD_WIDTH]), cw[:, SSD_WIDTH:], row(cb[SSD_WIDTH:]),
            _pad_lanes(ssd_dt_bias[l], LANES), _pad_lanes(ssd_a_log[l], LANES),
            row(jnp.repeat(ssd_d[l], HEAD_DIM)), row(ssd_norm_w[l]), expand.astype(BF16))

        mix = rw_mix[l]
        v_mix = rw_vres_mix[l - 1] if l > 0 else jnp.zeros((V_LORA,), F32)
        mix_misc = jnp.concatenate([jnp.zeros((MISC_WA,), F32), mix[3 * RWKV_WIDTH:], v_mix,
                                    jnp.zeros((SMALLBLK - MISC_GV - G_LORA - V_LORA,), F32)])
        wup = jnp.pad(rw_w_up[l], ((0, LANES - W_LORA), (0, 0)))
        wup_hi = wup.astype(BF16)
        wup_lo = (wup - wup_hi.astype(F32)).astype(BF16)
        wup3 = jnp.concatenate([wup_hi, wup_lo, wup_hi], axis=0)
        aup = jnp.pad(rw_a_up[l], ((W_LORA, 0), (0, 0))).astype(BF16)
        gup = jnp.pad(rw_g_up[l], ((0, 2 * LANES - G_LORA), (0, 0))).astype(BF16)
        if l > 0:
            v0 = row(rw_v0[l - 1])
            vup = jnp.pad(rw_v_up[l - 1], ((G_LORA, 2 * LANES - G_LORA - V_LORA), (0, 0))).astype(BF16)
        else:
            v0 = vup = None
        res = _rwkv_mixer(
            proj, bsz, t, vfirst, row(mix[:3 * RWKV_WIDTH]), row(mix_misc), row(rw_w0[l]), wup3,
            row(rw_a0[l]), aup, gup, v0, vup, row(rw_k_k[l]), row(rw_k_a[l]), row(rw_r_k[l]),
            row(rw_ln_w[l]), row(rw_ln_b[l]), head_ones)
        if l == 0:
            y_rw, vfirst = res
        else:
            y_rw = res

        is_moe = l % 2 == 1
        res = _outproj_ln(y_ssd, y_rw, w_out[l].astype(BF16), xf, row(ln1_g[l]), row(ln1_b[l]), bm_mid,
                          emit_rows=is_moe)
        i = l // 2
        if not is_moe:
            xf, xb = res
            h = _swiglu_up(xb, ffn_w1[i].astype(BF16), ffn_w3[i].astype(BF16), bm_big, 512)
            xf, xb = _down_ln(h, ffn_w2[i].astype(BF16), xf, row(ln2_g[l]), row(ln2_b[l]), bm_mid, 1408)
        else:
            xf, xb, x_rows = res
            wr = jnp.pad(moe_router[i], ((0, 0), (0, LANES - N_EXPERTS)))
            wr_hi = wr.astype(BF16)
            wr_lo = (wr - wr_hi.astype(F32)).astype(BF16)
            route, counts = _router(xf, jnp.concatenate([wr_hi, wr_lo, wr_hi], axis=0), bm_mid)
            dest, te, valid, pad_lo, pad_hi, n_rows = _moe_dispatch_plan(route, counts, n)
            xs_rows = _moe_dispatch(dest, pad_lo, pad_hi, x_rows, n, n_rows)
            h = _moe_up(te, valid, xs_rows, moe_w1_b, moe_w3_b, i, 1408)
            ys = _moe_down(te, valid, h, moe_w2_b, i)
            xf, xb = _moe_combine_ln(dest, ys, route, xf, row(ln2_g[l]), row(ln2_b[l]), bm_small)

    return xf.reshape(bsz, t, d)
```
